```python
import math
import jax, jax.numpy as jnp
from jax import lax
import numpy as np

D_MODEL = 1024
BATCH = 8
SEQ = 4096
DEPTH = 2

N_A_LAYERS = DEPTH // 2
N_B_LAYERS = DEPTH - N_A_LAYERS
PLE_DIM = 256
NORM_EPS = 1e-6
GATED_NORM_EPS = 1e-5
SSM_EXPAND = 2
D_INNER = SSM_EXPAND * D_MODEL
SSM_HEADDIM = 64
SSM_HEADS = D_INNER // SSM_HEADDIM
SSM_GROUPS = 8
SSM_HPG = SSM_HEADS // SSM_GROUPS
SSM_STATE = 128
CONV_K = 4
CONV_DIM = D_INNER + 2 * SSM_GROUPS * SSM_STATE
D_IN_PROJ = D_INNER + CONV_DIM + SSM_HEADS
SSD_CHUNK = 128
HEAD_DIM = 64
N_Q_HEADS = D_MODEL // HEAD_DIM
N_KV_HEADS = 2
Q_PER_KV = N_Q_HEADS // N_KV_HEADS
WINDOW = 128
ROT_DIM = HEAD_DIM // 4
ROPE_THETA = 500000.0
KV_DIM = N_KV_HEADS * HEAD_DIM
PEER_HEADS = 8
N_KEYS = 128
N_EXPERTS = N_KEYS * N_KEYS
PEER_DK = 256
PEER_HALF = PEER_DK // 2
PEER_TOPK = 16
PEER_BLOCK = 128

kernel_name = 'hybrid_ssd_swa_sink_peer_yoco'


def rmsnorm(x, g, eps=NORM_EPS):
    xf = x.astype(jnp.float32)
    xf = xf * lax.rsqrt(jnp.mean(xf * xf, axis=-1, keepdims=True) + eps)
    return (xf * g.astype(jnp.float32)).astype(x.dtype)


def rope_tables(positions):
    inv = jnp.power(ROPE_THETA, -jnp.arange(0, ROT_DIM, 2, dtype=jnp.float32) / ROT_DIM)
    ang = positions.astype(jnp.float32)[..., None] * inv
    return jnp.cos(ang)[:, :, None, :], jnp.sin(ang)[:, :, None, :]


def apply_partial_rope(t, cos, sin):
    half = ROT_DIM // 2
    t1 = t[..., :half].astype(jnp.float32)
    t2 = t[..., half:ROT_DIM].astype(jnp.float32)
    rot = jnp.concatenate([t1 * cos - t2 * sin, t2 * cos + t1 * sin], axis=-1).astype(t.dtype)
    return jnp.concatenate([rot, t[..., ROT_DIM:]], axis=-1)


def causal_conv(u, w, b):
    S = u.shape[1]
    up = jnp.pad(u, ((0, 0), (CONV_K - 1, 0), (0, 0)))
    out = b
    for k in range(CONV_K):
        out = out + up[:, k:k + S] * w[k]
    return out


def ssd_chunked(xs, dt, A, Bm, Cm):
    Bsz, S = xs.shape[0], xs.shape[1]
    nc = S // SSD_CHUNK
    f32 = jnp.float32
    X = (xs.astype(f32) * dt[..., None]).reshape(Bsz, nc, SSD_CHUNK, SSM_GROUPS, SSM_HPG, SSM_HEADDIM)
    a = (dt * A).reshape(Bsz, nc, SSD_CHUNK, SSM_GROUPS, SSM_HPG)
    Bc = Bm.astype(f32).reshape(Bsz, nc, SSD_CHUNK, SSM_GROUPS, SSM_STATE)
    Cc = Cm.astype(f32).reshape(Bsz, nc, SSD_CHUNK, SSM_GROUPS, SSM_STATE)
    a_cs = jnp.cumsum(a, axis=2)
    causal = jnp.tril(jnp.ones((SSD_CHUNK, SSD_CHUNK), dtype=bool))[:, :, None, None]
    seg = a_cs[:, :, :, None] - a_cs[:, :, None, :]
    L = jnp.exp(jnp.where(causal, seg, -jnp.inf))
    cb = jnp.einsum('bclgn,bcsgn->bclsg', Cc, Bc)
    y_diag = jnp.einsum('bclsg,bclsgr,bcsgrp->bclgrp', cb, L, X)
    decay_to_end = jnp.exp(a_cs[:, :, -1:] - a_cs)
    states = jnp.einsum('bclgn,bclgr,bclgrp->bcgrpn', Bc, decay_to_end, X)
    chunk_decay = jnp.exp(a_cs[:, :, -1])

    def step(h, inp):
        st, dc = inp
        return h * dc[..., None, None] + st, h

    h0 = jnp.zeros((Bsz, SSM_GROUPS, SSM_HPG, SSM_HEADDIM, SSM_STATE), f32)
    _, prev = lax.scan(step, h0, (jnp.moveaxis(states, 1, 0), jnp.moveaxis(chunk_decay, 1, 0)))
    prev = jnp.moveaxis(prev, 0, 1)
    y_off = jnp.einsum('bclgn,bcgrpn,bclgr->bclgrp', Cc, prev, jnp.exp(a_cs))
    return (y_diag + y_off).reshape(Bsz, S, SSM_HEADS, SSM_HEADDIM)


def mamba2_mixer(x, norm_g, in_w, conv_w, conv_b, dt_bias, A_log, D_skip, gate_norm, out_w):
    Bsz, S, _ = x.shape
    f32 = jnp.float32
    h = rmsnorm(x, norm_g)
    zxbcdt = h @ in_w
    z = zxbcdt[..., :D_INNER]
    xbc = zxbcdt[..., D_INNER:D_INNER + CONV_DIM]
    dt_raw = zxbcdt[..., D_INNER + CONV_DIM:]
    xbc = jax.nn.silu(causal_conv(xbc, conv_w, conv_b))
    gn = SSM_GROUPS * SSM_STATE
    xs = xbc[..., :D_INNER].reshape(Bsz, S, SSM_HEADS, SSM_HEADDIM)
    Bm = xbc[..., D_INNER:D_INNER + gn].reshape(Bsz, S, SSM_GROUPS, SSM_STATE)
    Cm = xbc[..., D_INNER + gn:].reshape(Bsz, S, SSM_GROUPS, SSM_STATE)
    dt = jax.nn.softplus(dt_raw.astype(f32) + dt_bias.astype(f32))
    A = -jnp.exp(A_log.astype(f32))
    y = ssd_chunked(xs, dt, A, Bm, Cm)
    y = y + xs.astype(f32) * D_skip.astype(f32)[:, None]
    y = y.reshape(Bsz, S, D_INNER) * jax.nn.silu(z.astype(f32))
    y = y.reshape(Bsz, S, SSM_GROUPS, D_INNER // SSM_GROUPS)
    y = y * lax.rsqrt(jnp.mean(y * y, axis=-1, keepdims=True) + GATED_NORM_EPS)
    y = (y.reshape(Bsz, S, D_INNER) * gate_norm.astype(f32)).astype(x.dtype)
    return y @ out_w


def shared_kv(x, norm_g, kv_w, kv_b, cos, sin):
    Bsz, S, _ = x.shape
    kv = rmsnorm(x, norm_g) @ kv_w + kv_b
    k = kv[..., :KV_DIM].reshape(Bsz, S, N_KV_HEADS, HEAD_DIM)
    v = kv[..., KV_DIM:].reshape(Bsz, S, N_KV_HEADS, HEAD_DIM)
    return apply_partial_rope(k, cos, sin), v


def swa_sink_attention(x, norm_g, q_w, q_b, sinks, o_w, o_b, k, v, cos, sin):
    Bsz, S, _ = x.shape
    nb = S // WINDOW
    f32 = jnp.float32
    h = rmsnorm(x, norm_g)
    q = (h @ q_w + q_b).reshape(Bsz, S, N_Q_HEADS, HEAD_DIM)
    q = apply_partial_rope(q, cos, sin)
    qb = q.reshape(Bsz, nb, WINDOW, N_KV_HEADS, Q_PER_KV, HEAD_DIM)

    def band(t):
        tp = jnp.concatenate([jnp.zeros_like(t[:, :WINDOW]), t], axis=1)
        tp = tp.reshape(Bsz, nb + 1, WINDOW, N_KV_HEADS, HEAD_DIM)
        return jnp.concatenate([tp[:, :-1], tp[:, 1:]], axis=2)

    kb, vb = band(k), band(v)
    sink_logits = sinks.astype(f32).reshape(N_KV_HEADS, Q_PER_KV)
    qi = jnp.arange(WINDOW)[:, None]
    kj = jnp.arange(2 * WINDOW)[None, :]
    in_window = (kj > qi) & (kj <= qi + WINDOW)
    scale = HEAD_DIM ** -0.5

    def block(inp):
        qblk, kblk, vblk, n = inp
        s = jnp.einsum('bqgrd,bkgd->bgrqk', qblk, kblk).astype(f32) * scale
        valid = in_window & (n * WINDOW - WINDOW + kj >= 0)
        s = jnp.where(valid, s, -jnp.inf)
        sink = jnp.broadcast_to(sink_logits[None, :, :, None, None], s.shape[:-1] + (1,))
        pr = jax.nn.softmax(jnp.concatenate([s, sink], axis=-1), axis=-1)[..., :-1]
        return jnp.einsum('bgrqk,bkgd->bqgrd', pr.astype(vblk.dtype), vblk)

    o = lax.map(block, (jnp.moveaxis(qb, 1, 0), jnp.moveaxis(kb, 1, 0), jnp.moveaxis(vb, 1, 0),
                        jnp.arange(nb)))
    o = jnp.moveaxis(o, 0, 1).reshape(Bsz, S, N_Q_HEADS * HEAD_DIM)
    return o @ o_w + o_b


def peer_ffn(x, norm_g, q_w, sub_keys, u, v):
    Bsz, S, D = x.shape
    T = Bsz * S
    h = rmsnorm(x, norm_g).reshape(T, D)
    q = (h @ q_w).reshape(T, PEER_HEADS, 2, PEER_HALF)
    s = jnp.einsum('thcd,ckd->thck', q, sub_keys).astype(jnp.float32)
    s1, i1 = lax.top_k(s[:, :, 0], PEER_TOPK)
    s2, i2 = lax.top_k(s[:, :, 1], PEER_TOPK)
    cand_s = (s1[..., :, None] + s2[..., None, :]).reshape(T, PEER_HEADS, PEER_TOPK * PEER_TOPK)
    cand_i = (i1[..., :, None] * N_KEYS + i2[..., None, :]).reshape(T, PEER_HEADS, PEER_TOPK * PEER_TOPK)
    best_s, best_pos = lax.top_k(cand_s, PEER_TOPK)
    idx = jnp.take_along_axis(cand_i, best_pos, axis=-1)
    gate = jax.nn.softmax(best_s, axis=-1).astype(x.dtype)
    nblk = T // PEER_BLOCK

    def block(inp):
        hb, ib, gb = inp
        a = jnp.einsum('nd,nhkd->nhk', hb, u[ib])
        w = gb * jax.nn.gelu(a, approximate=False)
        return jnp.einsum('nhk,nhkd->nd', w, v[ib])

    out = lax.map(block, (h.reshape(nblk, PEER_BLOCK, D),
                          idx.reshape(nblk, PEER_BLOCK, PEER_HEADS, PEER_TOPK),
                          gate.reshape(nblk, PEER_BLOCK, PEER_HEADS, PEER_TOPK)))
    return out.reshape(Bsz, S, D)


def ple_add(x, p_i, norm_g, proj, gate_w):
    gate = jax.nn.sigmoid(rmsnorm(x, norm_g) @ gate_w)
    return x + (p_i @ proj) * gate


def setup_inputs(seed: int = 0) -> dict:
    key = jax.random.key(seed)
    ks = jax.random.split(key, 32)
    f32 = jnp.float32

    def nrm(k, shape, scale):
        return jax.random.normal(k, shape, f32) * scale

    def gain(k, shape):
        return 1.0 + 0.05 * jax.random.normal(k, shape, f32)

    dt0 = jnp.exp(jax.random.uniform(ks[6], (N_A_LAYERS, SSM_HEADS), f32,
                                     minval=math.log(1e-3), maxval=math.log(1e-1)))
    inp = {
        'x': nrm(ks[0], (BATCH, SEQ, D_MODEL), 1.0),
        'p': nrm(ks[1], (DEPTH, BATCH, SEQ, PLE_DIM), 1.0),
        'positions': jnp.arange(SEQ, dtype=jnp.int32)[None, :]
                     + jax.random.randint(ks[2], (BATCH, 1), 0, 4096, dtype=jnp.int32),
        'ssm_norm': gain(ks[3], (N_A_LAYERS, D_MODEL)),
        'ssm_in_w': nrm(ks[4], (N_A_LAYERS, D_MODEL, D_IN_PROJ), D_MODEL ** -0.5),
        'ssm_conv_w': nrm(ks[5], (N_A_LAYERS, CONV_K, CONV_DIM), CONV_K ** -0.5),
        'ssm_conv_b': nrm(ks[7], (N_A_LAYERS, CONV_DIM), 0.02),
        'ssm_dt_bias': dt0 + jnp.log(-jnp.expm1(-dt0)),
        'ssm_A_log': jnp.log(jax.random.uniform(ks[8], (N_A_LAYERS, SSM_HEADS), f32, minval=1.0, maxval=16.0)),
        'ssm_D': gain(ks[9], (N_A_LAYERS, SSM_HEADS)),
        'ssm_gate_norm': gain(ks[10], (N_A_LAYERS, D_INNER)),
        'ssm_out_w': nrm(ks[11], (N_A_LAYERS, D_INNER, D_MODEL), D_INNER ** -0.5),
        'kv_norm': gain(ks[12], (D_MODEL,)),
        'kv_w': nrm(ks[13], (D_MODEL, 2 * KV_DIM), D_MODEL ** -0.5),
        'kv_b': nrm(ks[14], (2 * KV_DIM,), 0.02),
        'attn_norm': gain(ks[15], (N_B_LAYERS, D_MODEL)),
        'q_w': nrm(ks[16], (N_B_LAYERS, D_MODEL, N_Q_HEADS * HEAD_DIM), D_MODEL ** -0.5),
        'q_b': nrm(ks[17], (N_B_LAYERS, N_Q_HEADS * HEAD_DIM), 0.02),
        'sinks': nrm(ks[18], (N_B_LAYERS, N_Q_HEADS), 0.5),
        'o_w': nrm(ks[19], (N_B_LAYERS, N_Q_HEADS * HEAD_DIM, D_MODEL), (N_Q_HEADS * HEAD_DIM) ** -0.5),
        'o_b': nrm(ks[20], (N_B_LAYERS, D_MODEL), 0.02),
        'peer_norm': gain(ks[21], (DEPTH, D_MODEL)),
        'peer_q_w': nrm(ks[22], (DEPTH, D_MODEL, PEER_HEADS * PEER_DK), D_MODEL ** -0.5),
        'peer_sub_keys': nrm(ks[23], (DEPTH, 2, N_KEYS, PEER_HALF), PEER_HALF ** -0.5),
        'peer_u': nrm(ks[24], (DEPTH, N_EXPERTS, D_MODEL), D_MODEL ** -0.5),
        'peer_v': nrm(ks[25], (DEPTH, N_EXPERTS, D_MODEL), PEER_HEADS ** -0.5),
        'ple_norm': gain(ks[26], (DEPTH, D_MODEL)),
        'ple_proj': nrm(ks[27], (DEPTH, PLE_DIM, D_MODEL), PLE_DIM ** -0.5),
        'ple_gate_w': nrm(ks[28], (DEPTH, D_MODEL, D_MODEL), D_MODEL ** -0.5),
        'final_norm': gain(ks[29], (D_MODEL,)),
    }
    return inp


def reference(x, p, positions, ssm_norm, ssm_in_w, ssm_conv_w, ssm_conv_b, ssm_dt_bias, ssm_A_log,
              ssm_D, ssm_gate_norm, ssm_out_w, kv_norm, kv_w, kv_b, attn_norm, q_w, q_b, sinks,
              o_w, o_b, peer_norm, peer_q_w, peer_sub_keys, peer_u, peer_v, ple_norm, ple_proj,
              ple_gate_w, final_norm):
    cos, sin = rope_tables(positions)
    k_shared, v_shared = None, None
    for i in range(DEPTH):
        if i < N_A_LAYERS:
            j = i
            x = x + mamba2_mixer(x, ssm_norm[j], ssm_in_w[j], ssm_conv_w[j], ssm_conv_b[j],
                                 ssm_dt_bias[j], ssm_A_log[j], ssm_D[j], ssm_gate_norm[j], ssm_out_w[j])
        else:
            j = i - N_A_LAYERS
            x = x + swa_sink_attention(x, attn_norm[j], q_w[j], q_b[j], sinks[j], o_w[j], o_b[j],
                                       k_shared, v_shared, cos, sin)
        x = x + peer_ffn(x, peer_norm[i], peer_q_w[i], peer_sub_keys[i], peer_u[i], peer_v[i])
        x = ple_add(x, p[i], ple_norm[i], ple_proj[i], ple_gate_w[i])
        if i == N_A_LAYERS - 1:
            k_shared, v_shared = shared_kv(x, kv_norm, kv_w, kv_b, cos, sin)
    return rmsnorm(x, final_norm)
```

```python
import functools
import math

import jax
import jax.numpy as jnp
import numpy as np
from jax import lax
from jax.experimental import pallas as pl
from jax.experimental.pallas import tpu as pltpu

F32 = jnp.float32
BF16 = jnp.bfloat16

NORM_EPS = 1e-6
GATED_NORM_EPS = 1e-5
SSM_HEADDIM = 64
SSM_GROUPS = 8
SSM_STATE = 128
CONV_K = 4
SSD_CHUNK = 128
HEAD_DIM = 64
N_KV_HEADS = 2
WINDOW = 128
ROT_DIM = HEAD_DIM // 4
ROPE_THETA = 500000.0
PEER_HEADS = 8
N_KEYS = 128
PEER_TOPK = 16

LANES = 128
SUBLANES = 8
VMEM_LIMIT = 56 * 1024 * 1024

_CAND_PAIRS = [(r1, r2) for r1 in range(PEER_TOPK) for r2 in range(PEER_TOPK)
               if (r1 + 1) * (r2 + 1) <= PEER_TOPK]


def _params(*sem):
    return pltpu.CompilerParams(dimension_semantics=sem, vmem_limit_bytes=VMEM_LIMIT)


def _const_spec(shape):
    nd = len(shape)
    return pl.BlockSpec(shape, lambda *_: (0,) * nd, pipeline_mode=pl.Buffered(1))


def _rms(x, g, eps):
    return x * lax.rsqrt(jnp.mean(x * x, axis=-1, keepdims=True) + eps) * g


def _dot(a, b, dims=None, precision=None):
    if dims is None:
        dims = (((a.ndim - 1,), (0,)), ((), ()))
    return lax.dot_general(a, b, dims, precision=precision, preferred_element_type=F32)


_NT = (((1,), (1,)), ((), ()))
_TN = (((0,), (0,)), ((), ()))
_HI = lax.Precision.HIGHEST


def _inproj_kernel(x_ref, g_ref, wz_ref, wx_ref, wd_ref, z_ref, xbc_ref, dt_ref):
    h = _rms(x_ref[...], g_ref[...], NORM_EPS).astype(BF16)
    z_ref[...] = _dot(h, wz_ref[...])
    xbc_ref[...] = _dot(h, wx_ref[...])
    dt_ref[...] = _dot(h, wd_ref[...])


def _inproj(x, g, wz, wx, wd, tile):
    T, D = x.shape
    nz, nx, nd = wz.shape[1], wx.shape[1], wd.shape[1]
    return pl.pallas_call(
        _inproj_kernel,
        grid=(T // tile,),
        in_specs=[pl.BlockSpec((tile, D), lambda i: (i, 0)),
                  _const_spec((1, D)), _const_spec(wz.shape), _const_spec(wx.shape),
                  _const_spec(wd.shape)],
        out_specs=[pl.BlockSpec((tile, nz), lambda i: (i, 0)),
                   pl.BlockSpec((tile, nx), lambda i: (i, 0)),
                   pl.BlockSpec((tile, nd), lambda i: (i, 0))],
        out_shape=[jax.ShapeDtypeStruct((T, nz), F32), jax.ShapeDtypeStruct((T, nx), F32),
                   jax.ShapeDtypeStruct((T, nd), F32)],
        compiler_params=_params("arbitrary"),
        name="ssm_inproj",
    )(x, g, wz, wx, wd)


def _ssd_kernel(z_ref, xbc_ref, dtr_ref, cw_ref, cb_ref, dtb_ref, alog_ref, dexp_ref, gn_ref,
                expand_ref, tril_ref, y_ref, xb_ref, st_ref, *, d_inner, n_heads):
    L = SSD_CHUNK
    gw = d_inner // SSM_GROUPS
    hpg = n_heads // SSM_GROUPS
    tail = SUBLANES

    @pl.when(pl.program_id(1) == 0)
    def _():
        xb_ref[0:tail, :] = jnp.zeros((tail, xb_ref.shape[1]), F32)
        st_ref[...] = jnp.zeros(st_ref.shape, F32)

    xb_ref[tail:tail + L, :] = xbc_ref[...]

    def conv_silu(lo, width):
        acc = cb_ref[:, lo:lo + width]
        for k in range(CONV_K):
            off = tail - (CONV_K - 1) + k
            acc = acc + xb_ref[off:off + L, lo:lo + width] * cw_ref[k:k + 1, lo:lo + width]
        return acc * jax.nn.sigmoid(acc)

    dt_in = dtr_ref[...] + dtb_ref[...]
    dt = jnp.maximum(dt_in, 0.0) + jnp.log1p(jnp.exp(-jnp.abs(dt_in)))
    a = dt * (-jnp.exp(alog_ref[...]))
    tril = tril_ref[...]
    expand = expand_ref[...]
    a_cs = _dot(tril, a, precision=_HI)
    a_cs_t = a_cs.T
    a_cs_x = _dot(tril, _dot(a, expand, precision=_HI), precision=_HI)
    dt_x = _dot(dt, expand, precision=_HI)
    a_last_x = a_cs_x[L - 1:L, :]
    causal = tril > 0.5

    for g in range(SSM_GROUPS):
        lo = g * gw
        xs = conv_silu(lo, gw)
        bm = conv_silu(d_inner + g * SSM_STATE, SSM_STATE).astype(BF16)
        cm = conv_silu(d_inner + SSM_GROUPS * SSM_STATE + g * SSM_STATE, SSM_STATE).astype(BF16)
        xdt = xs * dt_x[:, lo:lo + gw]
        cb = _dot(cm, bm, _NT)
        yd = []
        for r in range(hpg):
            hh = g * hpg + r
            seg = a_cs[:, hh:hh + 1] - a_cs_t[hh:hh + 1, :]
            lmat = jnp.exp(jnp.where(causal, seg, -jnp.inf))
            m = (cb * lmat).astype(BF16)
            yd.append(_dot(m, xdt[:, r * SSM_HEADDIM:(r + 1) * SSM_HEADDIM].astype(BF16)))
        y = jnp.concatenate(yd, axis=1)
        acx = a_cs_x[:, lo:lo + gw]
        alx = a_last_x[:, lo:lo + gw]
        prev = st_ref[g]
        y = y + _dot(cm, prev.astype(BF16)) * jnp.exp(acx)
        xd = (xdt * jnp.exp(alx - acx)).astype(BF16)
        st_ref[g] = prev * jnp.exp(alx) + _dot(bm, xd, _TN)
        y = y + xs * dexp_ref[:, lo:lo + gw]
        zg = z_ref[:, lo:lo + gw]
        y = y * (zg * jax.nn.sigmoid(zg))
        y = y * lax.rsqrt(jnp.mean(y * y, axis=-1, keepdims=True) + GATED_NORM_EPS)
        y_ref[:, lo:lo + gw] = (y * gn_ref[:, lo:lo + gw]).astype(y_ref.dtype)

    xb_ref[0:tail, :] = xb_ref[L:L + tail, :]


def _ssd(z, xbc, dtr, cw, cb, dtb, alog, dexp, gn, expand, tril, batch):
    T, d_inner = z.shape
    conv_dim = xbc.shape[1]
    H = dtr.shape[1]
    L = SSD_CHUNK
    nc = T // batch // L
    row = lambda b, c: (b * nc + c, 0)
    kern = functools.partial(_ssd_kernel, d_inner=d_inner, n_heads=H)
    return pl.pallas_call(
        kern,
        grid=(batch, nc),
        in_specs=[pl.BlockSpec((L, d_inner), row), pl.BlockSpec((L, conv_dim), row),
                  pl.BlockSpec((L, H), row),
                  _const_spec(cw.shape), _const_spec(cb.shape), _const_spec(dtb.shape),
                  _const_spec(alog.shape), _const_spec(dexp.shape), _const_spec(gn.shape),
                  _const_spec(expand.shape), _const_spec(tril.shape)],
        out_specs=pl.BlockSpec((L, d_inner), row),
        out_shape=jax.ShapeDtypeStruct((T, d_inner), BF16),
        scratch_shapes=[pltpu.VMEM((L + SUBLANES, conv_dim), F32),
                        pltpu.VMEM((SSM_GROUPS, SSM_STATE, d_inner // SSM_GROUPS), F32)],
        compiler_params=_params("arbitrary", "arbitrary"),
        name="ssd_scan",
    )(z, xbc, dtr, cw, cb, dtb, alog, dexp, gn, expand, tril)


def _outproj_kernel(x_ref, y_ref, w_ref, o_ref):
    o_ref[...] = x_ref[...] + _dot(y_ref[...], w_ref[...])


def _outproj(x, y, w, tile):
    T, D = x.shape
    K = y.shape[1]
    return pl.pallas_call(
        _outproj_kernel,
        grid=(T // tile,),
        in_specs=[pl.BlockSpec((tile, D), lambda i: (i, 0)),
                  pl.BlockSpec((tile, K), lambda i: (i, 0)), _const_spec(w.shape)],
        out_specs=pl.BlockSpec((tile, D), lambda i: (i, 0)),
        out_shape=jax.ShapeDtypeStruct((T, D), F32),
        compiler_params=_params("arbitrary"),
        name="ssm_outproj",
    )(x, y, w)


def _peer_kernel(x_ref, g_ref, wq_ref, keys_ref, u_ref, vt_ref, o_ref,
                 hb_ref, s1_ref, s2_ref, e1_ref, e2_ref, work_ref, top_ref, cand_ref,
                 a_ref, p_ref, acc_ref):
    j = pl.program_id(1)
    nj = pl.num_programs(1)
    tt = x_ref.shape[0]
    ec = u_ref.shape[0]
    rows_per_step = ec // N_KEYS
    half = N_KEYS
    neg_inf = -jnp.inf

    @pl.when(j == 0)
    def _route():
        hb = _rms(x_ref[...], g_ref[...], NORM_EPS).astype(BF16)
        hb_ref[...] = hb
        for h in range(PEER_HEADS):
            q = _dot(wq_ref[h * 2 * half:(h + 1) * 2 * half, :], hb, _NT).astype(BF16)
            s1_ref[h] = _dot(keys_ref[0], q[0:half, :])
            s2_ref[h] = _dot(keys_ref[1], q[half:2 * half, :])

        for c, s_ref in enumerate((s1_ref, s2_ref)):
            for h in range(PEER_HEADS):
                work_ref[...] = s_ref[h]

                def top_body(r, carry, c=c, h=h):
                    w = work_ref[...]
                    m = jnp.max(w, axis=0, keepdims=True)
                    top_ref[c, r, h:h + 1, :] = m
                    work_ref[...] = jnp.where(w == m, neg_inf, w)
                    return carry

                lax.fori_loop(0, PEER_TOPK, top_body, 0)

        for idx, (r1, r2) in enumerate(_CAND_PAIRS):
            cand_ref[idx] = top_ref[0, r1] + top_ref[1, r2]
        a0 = top_ref[0, 0]
        b0 = top_ref[1, 0]
        m0 = a0 + b0

        def cand_body(k, carry):
            z, _ = carry
            m = cand_ref[0]
            for idx in range(1, len(_CAND_PAIRS)):
                m = jnp.maximum(m, cand_ref[idx])
            for idx in range(len(_CAND_PAIRS)):
                cv = cand_ref[idx]
                cand_ref[idx] = jnp.where(cv == m, neg_inf, cv)
            return z + jnp.exp(m - m0), m

        z, tau = lax.fori_loop(0, PEER_TOPK, cand_body,
                               (jnp.zeros(m0.shape, F32), m0))
        zinv = 1.0 / z
        for h in range(PEER_HEADS):
            s1 = s1_ref[h]
            e1_ref[h] = jnp.exp(s1 - a0[h:h + 1, :])
            s1_ref[h] = tau[h:h + 1, :] - s1
            e2_ref[h] = jnp.exp(s2_ref[h] - b0[h:h + 1, :]) * zinv[h:h + 1, :]

    a_ref[...] = _dot(u_ref[...], hb_ref[...], _NT)

    i0 = pl.multiple_of(j * rows_per_step, SUBLANES)

    def col_body(tc, carry):
        cs = pl.ds(pl.multiple_of(tc * LANES, LANES), LANES)
        thr8 = [s1_ref[h, pl.ds(i0, rows_per_step), cs] for h in range(PEER_HEADS)]
        e18 = [e1_ref[h, pl.ds(i0, rows_per_step), cs] for h in range(PEER_HEADS)]
        for ii in range(rows_per_step):
            rows = slice(ii * N_KEYS, (ii + 1) * N_KEYS)
            w = jnp.zeros((N_KEYS, LANES), F32)
            for h in range(PEER_HEADS):
                w = w + jnp.where(s2_ref[h, :, cs] >= thr8[h][ii:ii + 1, :],
                                  e18[h][ii:ii + 1, :] * e2_ref[h, :, cs], 0.0)
            av = a_ref[rows, cs]
            gelu = 0.5 * av * (1.0 + lax.erf(av * np.float32(math.sqrt(0.5))))
            p_ref[rows, cs] = (w * gelu).astype(BF16)
        return carry

    lax.fori_loop(0, tt // LANES, col_body, 0)

    upd = _dot(vt_ref[...], p_ref[...])

    @pl.when(j == 0)
    def _():
        acc_ref[...] = upd

    @pl.when(j > 0)
    def _():
        acc_ref[...] += upd

    @pl.when(j == nj - 1)
    def _():
        o_ref[...] = x_ref[...] + acc_ref[...].T


def _peer(x, g, wq_t, keys, u, vt, tile, ec):
    T, D = x.shape
    E = u.shape[0]
    ncand = len(_CAND_PAIRS)
    assert E == N_KEYS * N_KEYS and ec == SUBLANES * N_KEYS and tile % LANES == 0
    return pl.pallas_call(
        _peer_kernel,
        grid=(T // tile, E // ec),
        in_specs=[pl.BlockSpec((tile, D), lambda i, j: (i, 0)),
                  _const_spec((1, D)), _const_spec(wq_t.shape), _const_spec(keys.shape),
                  pl.BlockSpec((ec, D), lambda i, j: (j, 0)),
                  pl.BlockSpec((D, ec), lambda i, j: (0, j))],
        out_specs=pl.BlockSpec((tile, D), lambda i, j: (i, 0)),
        out_shape=jax.ShapeDtypeStruct((T, D), F32),
        scratch_shapes=[pltpu.VMEM((tile, D), BF16),
                        pltpu.VMEM((PEER_HEADS, N_KEYS, tile), F32),
                        pltpu.VMEM((PEER_HEADS, N_KEYS, tile), F32),
                        pltpu.VMEM((PEER_HEADS, N_KEYS, tile), F32),
                        pltpu.VMEM((PEER_HEADS, N_KEYS, tile), F32),
                        pltpu.VMEM((N_KEYS, tile), F32),
                        pltpu.VMEM((2, PEER_TOPK, PEER_HEADS, tile), F32),
                        pltpu.VMEM((ncand, PEER_HEADS, tile), F32),
                        pltpu.VMEM((ec, tile), F32),
                        pltpu.VMEM((ec, tile), BF16),
                        pltpu.VMEM((D, tile), F32)],
        compiler_params=_params("arbitrary", "arbitrary"),
        name="peer",
    )(x, g, wq_t, keys, u, vt)


def _rope_tables(pos_ref, inv_ref):
    ang = pos_ref[...].astype(F32) * inv_ref[...]
    lane = lax.broadcasted_iota(jnp.int32, ang.shape, 1) % HEAD_DIM
    cos = jnp.cos(ang)
    sin = jnp.sin(ang)
    half = ROT_DIM // 2
    sin_lo = jnp.where(lane < half, -sin, 0.0)
    sin_hi = jnp.where((lane >= half) & (lane < ROT_DIM), sin, 0.0)
    return cos, sin_lo, sin_hi


def _rope_apply(t, cos, sin_lo, sin_hi):
    half = ROT_DIM // 2
    up = pltpu.roll(t, LANES - half, 1)
    dn = pltpu.roll(t, half, 1)
    return t * cos + up * sin_lo + dn * sin_hi


def _ple_core(x_ref, p_ref, g_ref, proj_ref, gw_ref):
    x = x_ref[...]
    hn = _rms(x, g_ref[...], NORM_EPS).astype(BF16)
    gate = jax.nn.sigmoid(_dot(hn, gw_ref[...]))
    return x + _dot(p_ref[...].astype(BF16), proj_ref[...]) * gate


def _ple_kv_kernel(x_ref, p_ref, g_ref, proj_ref, gw_ref, kvg_ref, kvw_ref, kvb_ref, pos_ref,
                   inv_ref, o_ref, k_ref, v_ref):
    x2 = _ple_core(x_ref, p_ref, g_ref, proj_ref, gw_ref)
    o_ref[...] = x2
    kv = _dot(_rms(x2, kvg_ref[...], NORM_EPS).astype(BF16), kvw_ref[...]) + kvb_ref[...]
    kvd = k_ref.shape[1]
    cos, sin_lo, sin_hi = _rope_tables(pos_ref, inv_ref)
    k_ref[...] = _rope_apply(kv[:, :kvd], cos, sin_lo, sin_hi).astype(k_ref.dtype)
    v_ref[...] = kv[:, kvd:].astype(v_ref.dtype)


def _ple_final_kernel(x_ref, p_ref, g_ref, proj_ref, gw_ref, fg_ref, o_ref):
    x2 = _ple_core(x_ref, p_ref, g_ref, proj_ref, gw_ref)
    o_ref[...] = _rms(x2, fg_ref[...], NORM_EPS)


def _ple_kv(x, p, g, proj, gw, kvg, kvw, kvb, pos, inv, tile):
    T, D = x.shape
    P = p.shape[1]
    kvd = kvw.shape[1] // 2
    tok = lambda w: pl.BlockSpec((tile, w), lambda i: (i, 0))
    return pl.pallas_call(
        _ple_kv_kernel,
        grid=(T // tile,),
        in_specs=[tok(D), tok(P), _const_spec((1, D)), _const_spec(proj.shape),
                  _const_spec(gw.shape), _const_spec((1, D)), _const_spec(kvw.shape),
                  _const_spec(kvb.shape), tok(1), _const_spec(inv.shape)],
        out_specs=[tok(D), tok(kvd), tok(kvd)],
        out_shape=[jax.ShapeDtypeStruct((T, D), F32), jax.ShapeDtypeStruct((T, kvd), BF16),
                   jax.ShapeDtypeStruct((T, kvd), BF16)],
        compiler_params=_params("arbitrary"),
        name="ple_kv",
    )(x, p, g, proj, gw, kvg, kvw, kvb, pos, inv)


def _ple_final(x, p, g, proj, gw, fg, tile):
    T, D = x.shape
    P = p.shape[1]
    tok = lambda w: pl.BlockSpec((tile, w), lambda i: (i, 0))
    return pl.pallas_call(
        _ple_final_kernel,
        grid=(T // tile,),
        in_specs=[tok(D), tok(P), _const_spec((1, D)), _const_spec(proj.shape),
                  _const_spec(gw.shape), _const_spec((1, D))],
        out_specs=tok(D),
        out_shape=jax.ShapeDtypeStruct((T, D), F32),
        compiler_params=_params("arbitrary"),
        name="ple_final",
    )(x, p, g, proj, gw, fg)


def _attn_kernel(x_ref, g_ref, qw_ref, qb_ref, sink_ref, ow_ref, ob_ref, kc_ref, kp_ref, vc_ref,
                 vp_ref, pos_ref, inv_ref, o_ref, q_ref, att_ref, *, n_q_heads):
    tile = x_ref.shape[0]
    W = WINDOW
    nblk = tile // W
    q_per_kv = n_q_heads // N_KV_HEADS
    scale = HEAD_DIM ** -0.5
    first = pl.program_id(1) == 0

    h = _rms(x_ref[...], g_ref[...], NORM_EPS).astype(BF16)
    q = _dot(h, qw_ref[...]) + qb_ref[...]
    cos, sin_lo, sin_hi = _rope_tables(pos_ref, inv_ref)
    for lg in range(q.shape[1] // LANES):
        cs = slice(lg * LANES, (lg + 1) * LANES)
        q_ref[:, cs] = _rope_apply(q[:, cs], cos, sin_lo, sin_hi).astype(BF16)

    qi = lax.broadcasted_iota(jnp.int32, (W, 2 * W), 0)
    kj = lax.broadcasted_iota(jnp.int32, (W, 2 * W), 1)
    first_lo = jnp.where(first, W, 0)
    band = jnp.where((kj > qi) & (kj <= qi + W), 0.0, -jnp.inf)
    band_first = jnp.where(kj >= first_lo, band, -jnp.inf)
    band = jnp.concatenate([band] * q_per_kv, axis=0)
    band_first = jnp.concatenate([band_first] * q_per_kv, axis=0)

    for n in range(nblk):
        rows = slice(n * W, (n + 1) * W)
        if n == 0:
            kprev, vprev = kp_ref[...], vp_ref[...]
        else:
            kprev, vprev = kc_ref[(n - 1) * W:n * W, :], vc_ref[(n - 1) * W:n * W, :]
        kblk = jnp.concatenate([kprev, kc_ref[rows, :]], axis=0)
        vblk = jnp.concatenate([vprev, vc_ref[rows, :]], axis=0)
        bias = band_first if n == 0 else band
        for g in range(N_KV_HEADS):
            ks = kblk[:, g * HEAD_DIM:(g + 1) * HEAD_DIM]
            vs = vblk[:, g * HEAD_DIM:(g + 1) * HEAD_DIM]
            qs = jnp.concatenate(
                [q_ref[rows, (g * q_per_kv + r) * HEAD_DIM:(g * q_per_kv + r + 1) * HEAD_DIM]
                 for r in range(q_per_kv)], axis=0)
            s = _dot(qs, ks, _NT) * scale + bias
            sink = jnp.concatenate(
                [jnp.broadcast_to(sink_ref[:, g * q_per_kv + r:g * q_per_kv + r + 1], (W, 1))
                 for r in range(q_per_kv)], axis=0)
            m = jnp.maximum(jnp.max(s, axis=-1, keepdims=True), sink)
            e = jnp.exp(s - m)
            denom = jnp.sum(e, axis=-1, keepdims=True) + jnp.exp(sink - m)
            pr = (e / denom).astype(BF16)
            o = _dot(pr, vs)
            for r in range(q_per_kv):
                hh = g * q_per_kv + r
                att_ref[rows, hh * HEAD_DIM:(hh + 1) * HEAD_DIM] = o[r * W:(r + 1) * W, :].astype(BF16)

    o_ref[...] = x_ref[...] + _dot(att_ref[...], ow_ref[...]) + ob_ref[...]


def _attn(x, g, qw, qb, sinks, ow, ob, k, v, pos, inv, batch, tile):
    T, D = x.shape
    kvd = k.shape[1]
    nq = qw.shape[1] // HEAD_DIM
    nt = T // batch // tile
    bpt = tile // WINDOW
    row = lambda b, i: (b * nt + i, 0)
    prev = lambda b, i: (jnp.maximum((b * nt + i) * bpt - 1, 0), 0)
    kern = functools.partial(_attn_kernel, n_q_heads=nq)
    return pl.pallas_call(
        kern,
        grid=(batch, nt),
        in_specs=[pl.BlockSpec((tile, D), row), _const_spec((1, D)), _const_spec(qw.shape),
                  _const_spec(qb.shape), _const_spec(sinks.shape), _const_spec(ow.shape),
                  _const_spec(ob.shape),
                  pl.BlockSpec((tile, kvd), row), pl.BlockSpec((WINDOW, kvd), prev),
                  pl.BlockSpec((tile, kvd), row), pl.BlockSpec((WINDOW, kvd), prev),
                  pl.BlockSpec((tile, 1), row), _const_spec(inv.shape)],
        out_specs=pl.BlockSpec((tile, D), row),
        out_shape=jax.ShapeDtypeStruct((T, D), F32),
        scratch_shapes=[pltpu.VMEM((tile, qw.shape[1]), BF16),
                        pltpu.VMEM((tile, qw.shape[1]), BF16)],
        compiler_params=_params("arbitrary", "arbitrary"),
        name="swa_attn",
    )(x, g, qw, qb, sinks, ow, ob, k, k, v, v, pos, inv)


def _row(v):
    return v.reshape(1, -1)


def kernel(x, p, positions, ssm_norm, ssm_in_w, ssm_conv_w, ssm_conv_b, ssm_dt_bias, ssm_A_log, ssm_D, ssm_gate_norm, ssm_out_w, kv_norm, kv_w, kv_b, attn_norm, q_w, q_b, sinks, o_w, o_b, peer_norm, peer_q_w, peer_sub_keys, peer_u, peer_v, ple_norm, ple_proj, ple_gate_w, final_norm):
    B, S, D = x.shape
    T = B * S
    depth = p.shape[0]
    n_a = ssm_norm.shape[0]
    H = ssm_D.shape[1]
    d_inner = H * SSM_HEADDIM
    conv_dim = ssm_conv_w.shape[2]

    xt = x.reshape(T, D)
    pos = positions.reshape(T, 1)
    lane = np.arange(LANES) % HEAD_DIM
    inv = np.where(lane < ROT_DIM,
                   ROPE_THETA ** (-(2.0 * (lane % (ROT_DIM // 2))) / ROT_DIM), 0.0)
    inv = jnp.asarray(inv.reshape(1, LANES), F32)
    expand = jnp.asarray(np.repeat(np.eye(H, dtype=np.float32), SSM_HEADDIM, axis=1))
    tril = jnp.asarray(np.tril(np.ones((SSD_CHUNK, SSD_CHUNK), np.float32)))

    k_sh = v_sh = None
    for i in range(depth):
        if i < n_a:
            w = ssm_in_w[i].astype(BF16)
            z, xbc, dtr = _inproj(xt, _row(ssm_norm[i]), w[:, :d_inner],
                                  w[:, d_inner:d_inner + conv_dim], w[:, d_inner + conv_dim:],
                                  tile=256)
            y = _ssd(z, xbc, dtr, ssm_conv_w[i], _row(ssm_conv_b[i]), _row(ssm_dt_bias[i]),
                     _row(ssm_A_log[i]), _row(jnp.repeat(ssm_D[i], SSM_HEADDIM)),
                     _row(ssm_gate_norm[i]), expand, tril, batch=B)
            xt = _outproj(xt, y, ssm_out_w[i].astype(BF16), tile=512)
        else:
            j = i - n_a
            xt = _attn(xt, _row(attn_norm[j]), q_w[j].astype(BF16), _row(q_b[j]), _row(sinks[j]),
                       o_w[j].astype(BF16), _row(o_b[j]), k_sh, v_sh, pos, inv, batch=B, tile=512)
        xt = _peer(xt, _row(peer_norm[i]), peer_q_w[i].T.astype(BF16),
                   peer_sub_keys[i].astype(BF16), peer_u[i].astype(BF16),
                   peer_v[i].T.astype(BF16), tile=512, ec=1024)
        if i == n_a - 1:
            xt, k_sh, v_sh = _ple_kv(xt, p[i].reshape(T, -1), _row(ple_norm[i]),
                                     ple_proj[i].astype(BF16), ple_gate_w[i].astype(BF16),
                                     _row(kv_norm), kv_w.astype(BF16), _row(kv_b), pos, inv,
                                     tile=512)
        elif i == depth - 1:
            xt = _ple_final(xt, p[i].reshape(T, -1), _row(ple_norm[i]), ple_proj[i].astype(BF16),
                            ple_gate_w[i].astype(BF16), _row(final_norm), tile=512)
        else:
            raise NotImplementedError("PLE without K/V or final norm")
    return xt.reshape(B, S, D)
```

```python
import functools
import math

import jax
import jax.numpy as jnp
import numpy as np
from jax import lax
from jax.experimental import pallas as pl
from jax.experimental.pallas import tpu as pltpu

F32 = jnp.float32
BF16 = jnp.bfloat16

NORM_EPS = 1e-6
GATED_NORM_EPS = 1e-5
SSM_HEADDIM = 64
SSM_GROUPS = 8
SSM_STATE = 128
CONV_K = 4
SSD_CHUNK = 128
HEAD_DIM = 64
N_KV_HEADS = 2
WINDOW = 128
ROT_DIM = HEAD_DIM // 4
ROPE_THETA = 500000.0
PEER_HEADS = 8
N_KEYS = 128
PEER_TOPK = 16

LANES = 128
SUBLANES = 8
VMEM_LIMIT = 56 * 1024 * 1024

_CAND_PAIRS = [(r1, r2) for r1 in range(PEER_TOPK) for r2 in range(PEER_TOPK)
               if (r1 + 1) * (r2 + 1) <= PEER_TOPK]


def _params(*sem):
    return pltpu.CompilerParams(dimension_semantics=sem, vmem_limit_bytes=VMEM_LIMIT)


def _const_spec(shape):
    nd = len(shape)
    return pl.BlockSpec(shape, lambda *_: (0,) * nd, pipeline_mode=pl.Buffered(1))


def _rms(x, g, eps):
    return x * lax.rsqrt(jnp.mean(x * x, axis=-1, keepdims=True) + eps) * g


def _dot(a, b, dims=None, precision=None):
    if dims is None:
        dims = (((a.ndim - 1,), (0,)), ((), ()))
    return lax.dot_general(a, b, dims, precision=precision, preferred_element_type=F32)


_NT = (((1,), (1,)), ((), ()))
_TN = (((0,), (0,)), ((), ()))
_HI = lax.Precision.HIGHEST


def _inproj_kernel(x_ref, g_ref, wz_ref, wx_ref, wd_ref, z_ref, xbc_ref, dt_ref):
    h = _rms(x_ref[...], g_ref[...], NORM_EPS).astype(BF16)
    z_ref[...] = _dot(h, wz_ref[...])
    xbc_ref[...] = _dot(h, wx_ref[...])
    dt_ref[...] = _dot(h, wd_ref[...])


def _inproj(x, g, wz, wx, wd, tile):
    T, D = x.shape
    nz, nx, nd = wz.shape[1], wx.shape[1], wd.shape[1]
    return pl.pallas_call(
        _inproj_kernel,
        grid=(T // tile,),
        in_specs=[pl.BlockSpec((tile, D), lambda i: (i, 0)),
                  _const_spec((1, D)), _const_spec(wz.shape), _const_spec(wx.shape),
                  _const_spec(wd.shape)],
        out_specs=[pl.BlockSpec((tile, nz), lambda i: (i, 0)),
                   pl.BlockSpec((tile, nx), lambda i: (i, 0)),
                   pl.BlockSpec((tile, nd), lambda i: (i, 0))],
        out_shape=[jax.ShapeDtypeStruct((T, nz), F32), jax.ShapeDtypeStruct((T, nx), F32),
                   jax.ShapeDtypeStruct((T, nd), F32)],
        compiler_params=_params("arbitrary"),
        name="ssm_inproj",
    )(x, g, wz, wx, wd)


def _ssd_kernel(z_ref, xbc_ref, dtr_ref, cw_ref, cb_ref, dtb_ref, alog_ref, dexp_ref, gn_ref,
                expand_ref, tril_ref, y_ref, xb_ref, st_ref, *, d_inner, n_heads):
    L = SSD_CHUNK
    gw = d_inner // SSM_GROUPS
    hpg = n_heads // SSM_GROUPS
    tail = SUBLANES

    @pl.when(pl.program_id(1) == 0)
    def _():
        xb_ref[0:tail, :] = jnp.zeros((tail, xb_ref.shape[1]), F32)
        st_ref[...] = jnp.zeros(st_ref.shape, F32)

    xb_ref[tail:tail + L, :] = xbc_ref[...]

    def conv_silu(lo, width):
        acc = cb_ref[:, lo:lo + width]
        for k in range(CONV_K):
            off = tail - (CONV_K - 1) + k
            acc = acc + xb_ref[off:off + L, lo:lo + width] * cw_ref[k:k + 1, lo:lo + width]
        return acc * jax.nn.sigmoid(acc)

    dt_in = dtr_ref[...] + dtb_ref[...]
    dt = jnp.maximum(dt_in, 0.0) + jnp.log1p(jnp.exp(-jnp.abs(dt_in)))
    a = dt * (-jnp.exp(alog_ref[...]))
    tril = tril_ref[...]
    expand = expand_ref[...]
    a_cs = _dot(tril, a, precision=_HI)
    a_cs_t = a_cs.T
    a_cs_x = _dot(tril, _dot(a, expand, precision=_HI), precision=_HI)
    dt_x = _dot(dt, expand, precision=_HI)
    a_last_x = a_cs_x[L - 1:L, :]
    causal = tril > 0.5

    for g in range(SSM_GROUPS):
        lo = g * gw
        xs = conv_silu(lo, gw)
        bm = conv_silu(d_inner + g * SSM_STATE, SSM_STATE).astype(BF16)
        cm = conv_silu(d_inner + SSM_GROUPS * SSM_STATE + g * SSM_STATE, SSM_STATE).astype(BF16)
        xdt = xs * dt_x[:, lo:lo + gw]
        cb = _dot(cm, bm, _NT)
        yd = []
        for r in range(hpg):
            hh = g * hpg + r
            seg = a_cs[:, hh:hh + 1] - a_cs_t[hh:hh + 1, :]
            lmat = jnp.exp(jnp.where(causal, seg, -jnp.inf))
            m = (cb * lmat).astype(BF16)
            yd.append(_dot(m, xdt[:, r * SSM_HEADDIM:(r + 1) * SSM_HEADDIM].astype(BF16)))
        y = jnp.concatenate(yd, axis=1)
        acx = a_cs_x[:, lo:lo + gw]
        alx = a_last_x[:, lo:lo + gw]
        prev = st_ref[g]
        y = y + _dot(cm, prev.astype(BF16)) * jnp.exp(acx)
        xd = (xdt * jnp.exp(alx - acx)).astype(BF16)
        st_ref[g] = prev * jnp.exp(alx) + _dot(bm, xd, _TN)
        y = y + xs * dexp_ref[:, lo:lo + gw]
        zg = z_ref[:, lo:lo + gw]
        y = y * (zg * jax.nn.sigmoid(zg))
        y = y * lax.rsqrt(jnp.mean(y * y, axis=-1, keepdims=True) + GATED_NORM_EPS)
        y_ref[:, lo:lo + gw] = (y * gn_ref[:, lo:lo + gw]).astype(y_ref.dtype)

    xb_ref[0:tail, :] = xb_ref[L:L + tail, :]


def _ssd(z, xbc, dtr, cw, cb, dtb, alog, dexp, gn, expand, tril, batch):
    T, d_inner = z.shape
    conv_dim = xbc.shape[1]
    H = dtr.shape[1]
    L = SSD_CHUNK
    nc = T // batch // L
    row = lambda b, c: (b * nc + c, 0)
    kern = functools.partial(_ssd_kernel, d_inner=d_inner, n_heads=H)
    return pl.pallas_call(
        kern,
        grid=(batch, nc),
        in_specs=[pl.BlockSpec((L, d_inner), row), pl.BlockSpec((L, conv_dim), row),
                  pl.BlockSpec((L, H), row),
                  _const_spec(cw.shape), _const_spec(cb.shape), _const_spec(dtb.shape),
                  _const_spec(alog.shape), _const_spec(dexp.shape), _const_spec(gn.shape),
                  _const_spec(expand.shape), _const_spec(tril.shape)],
        out_specs=pl.BlockSpec((L, d_inner), row),
        out_shape=jax.ShapeDtypeStruct((T, d_inner), BF16),
        scratch_shapes=[pltpu.VMEM((L + SUBLANES, conv_dim), F32),
                        pltpu.VMEM((SSM_GROUPS, SSM_STATE, d_inner // SSM_GROUPS), F32)],
        compiler_params=_params("arbitrary", "arbitrary"),
        name="ssd_scan",
    )(z, xbc, dtr, cw, cb, dtb, alog, dexp, gn, expand, tril)


def _outproj_kernel(x_ref, y_ref, w_ref, o_ref):
    o_ref[...] = x_ref[...] + _dot(y_ref[...], w_ref[...])


def _outproj(x, y, w, tile):
    T, D = x.shape
    K = y.shape[1]
    return pl.pallas_call(
        _outproj_kernel,
        grid=(T // tile,),
        in_specs=[pl.BlockSpec((tile, D), lambda i: (i, 0)),
                  pl.BlockSpec((tile, K), lambda i: (i, 0)), _const_spec(w.shape)],
        out_specs=pl.BlockSpec((tile, D), lambda i: (i, 0)),
        out_shape=jax.ShapeDtypeStruct((T, D), F32),
        compiler_params=_params("arbitrary"),
        name="ssm_outproj",
    )(x, y, w)


def _peer_kernel(x_ref, g_ref, wq_ref, keys_ref, u_ref, vt_ref, o_ref,
                 hb_ref, s1_ref, s2_ref, thr_ref, e1_ref, s2b_ref, e2b_ref, work_ref, top_ref,
                 cand_ref, a_ref, p_ref, acc_ref):
    j = pl.program_id(1)
    nj = pl.num_programs(1)
    tt = x_ref.shape[0]
    ec = u_ref.shape[0]
    rows_per_step = ec // N_KEYS
    half = N_KEYS
    neg_inf = -jnp.inf

    @pl.when(j == 0)
    def _route():
        hb = _rms(x_ref[...], g_ref[...], NORM_EPS).astype(BF16)
        hb_ref[...] = hb
        for h in range(PEER_HEADS):
            q = _dot(wq_ref[h * 2 * half:(h + 1) * 2 * half, :], hb, _NT).astype(BF16)
            s1_ref[h] = _dot(keys_ref[0], q[0:half, :])
            s2_ref[h] = _dot(keys_ref[1], q[half:2 * half, :])

        for c, s_ref in enumerate((s1_ref, s2_ref)):
            for h in range(PEER_HEADS):
                work_ref[...] = s_ref[h]

                def top_body(r, carry, c=c, h=h):
                    w = work_ref[...]
                    m = jnp.max(w, axis=0, keepdims=True)
                    top_ref[c, r, h:h + 1, :] = m
                    work_ref[...] = jnp.where(w == m, neg_inf, w)
                    return carry

                lax.fori_loop(0, PEER_TOPK, top_body, 0)

        for idx, (r1, r2) in enumerate(_CAND_PAIRS):
            cand_ref[idx] = top_ref[0, r1] + top_ref[1, r2]
        a0 = top_ref[0, 0]
        b0 = top_ref[1, 0]
        m0 = a0 + b0

        def cand_body(k, carry):
            z, _ = carry
            m = cand_ref[0]
            for idx in range(1, len(_CAND_PAIRS)):
                m = jnp.maximum(m, cand_ref[idx])
            for idx in range(len(_CAND_PAIRS)):
                cv = cand_ref[idx]
                cand_ref[idx] = jnp.where(cv == m, neg_inf, cv)
            return z + jnp.exp(m - m0), m

        z, tau = lax.fori_loop(0, PEER_TOPK, cand_body,
                               (jnp.zeros(m0.shape, F32), m0))
        zinv = 1.0 / z
        for h in range(PEER_HEADS):
            s1 = s1_ref[h]
            e1 = jnp.exp(s1 - a0[h:h + 1, :])
            thr = tau[h:h + 1, :] - s1
            s2 = s2_ref[h]
            e2 = jnp.exp(s2 - b0[h:h + 1, :]) * zinv[h:h + 1, :]
            for tc in range(tt // LANES):
                cs = slice(tc * LANES, (tc + 1) * LANES)
                e1_ref[h, tc] = e1[:, cs]
                thr_ref[h, tc] = thr[:, cs]
                s2b_ref[h, tc] = s2[:, cs].astype(BF16)
                e2b_ref[h, tc] = e2[:, cs].astype(BF16)

    a_ref[...] = _dot(u_ref[...], hb_ref[...], _NT)

    i0 = pl.multiple_of(j * rows_per_step, SUBLANES)

    def col_body(tc, carry):
        cs = pl.ds(pl.multiple_of(tc * LANES, LANES), LANES)
        thr8 = [thr_ref[h, tc, pl.ds(i0, rows_per_step), :].astype(BF16) for h in range(PEER_HEADS)]
        e18 = [e1_ref[h, tc, pl.ds(i0, rows_per_step), :].astype(BF16) for h in range(PEER_HEADS)]
        for ii in range(rows_per_step):
            rows = slice(ii * N_KEYS, (ii + 1) * N_KEYS)
            w = jnp.zeros((N_KEYS, LANES), BF16)
            for h in range(PEER_HEADS):
                w = w + jnp.where(s2b_ref[h, tc] >= thr8[h][ii:ii + 1, :],
                                  e18[h][ii:ii + 1, :] * e2b_ref[h, tc], jnp.zeros((), BF16))
            av = a_ref[rows, cs]
            gelu = 0.5 * av * (1.0 + lax.erf(av * np.float32(math.sqrt(0.5))))
            p_ref[rows, cs] = w * gelu.astype(BF16)
        return carry

    lax.fori_loop(0, tt // LANES, col_body, 0)

    upd = _dot(vt_ref[...], p_ref[...])

    @pl.when(j == 0)
    def _():
        acc_ref[...] = upd

    @pl.when(j > 0)
    def _():
        acc_ref[...] += upd

    @pl.when(j == nj - 1)
    def _():
        o_ref[...] = x_ref[...] + acc_ref[...].T


def _peer(x, g, wq_t, keys, u, vt, tile, ec):
    T, D = x.shape
    E = u.shape[0]
    ncand = len(_CAND_PAIRS)
    assert E == N_KEYS * N_KEYS and ec == SUBLANES * N_KEYS and tile % LANES == 0
    return pl.pallas_call(
        _peer_kernel,
        grid=(T // tile, E // ec),
        in_specs=[pl.BlockSpec((tile, D), lambda i, j: (i, 0)),
                  _const_spec((1, D)), _const_spec(wq_t.shape), _const_spec(keys.shape),
                  pl.BlockSpec((ec, D), lambda i, j: (j, 0)),
                  pl.BlockSpec((D, ec), lambda i, j: (0, j))],
        out_specs=pl.BlockSpec((tile, D), lambda i, j: (i, 0)),
        out_shape=jax.ShapeDtypeStruct((T, D), F32),
        scratch_shapes=[pltpu.VMEM((tile, D), BF16),
                        pltpu.VMEM((PEER_HEADS, N_KEYS, tile), F32),
                        pltpu.VMEM((PEER_HEADS, N_KEYS, tile), F32),
                        pltpu.VMEM((PEER_HEADS, tile // LANES, N_KEYS, LANES), F32),
                        pltpu.VMEM((PEER_HEADS, tile // LANES, N_KEYS, LANES), F32),
                        pltpu.VMEM((PEER_HEADS, tile // LANES, N_KEYS, LANES), BF16),
                        pltpu.VMEM((PEER_HEADS, tile // LANES, N_KEYS, LANES), BF16),
                        pltpu.VMEM((N_KEYS, tile), F32),
                        pltpu.VMEM((2, PEER_TOPK, PEER_HEADS, tile), F32),
                        pltpu.VMEM((ncand, PEER_HEADS, tile), F32),
                        pltpu.VMEM((ec, tile), F32),
                        pltpu.VMEM((ec, tile), BF16),
                        pltpu.VMEM((D, tile), F32)],
        compiler_params=_params("arbitrary", "arbitrary"),
        name="peer",
    )(x, g, wq_t, keys, u, vt)


def _rope_tables(pos_ref, inv_ref):
    ang = pos_ref[...].astype(F32) * inv_ref[...]
    lane = lax.broadcasted_iota(jnp.int32, ang.shape, 1) % HEAD_DIM
    cos = jnp.cos(ang)
    sin = jnp.sin(ang)
    half = ROT_DIM // 2
    sin_lo = jnp.where(lane < half, -sin, 0.0)
    sin_hi = jnp.where((lane >= half) & (lane < ROT_DIM), sin, 0.0)
    return cos, sin_lo, sin_hi


def _rope_apply(t, cos, sin_lo, sin_hi):
    half = ROT_DIM // 2
    up = pltpu.roll(t, LANES - half, 1)
    dn = pltpu.roll(t, half, 1)
    return t * cos + up * sin_lo + dn * sin_hi


def _ple_core(x_ref, p_ref, g_ref, proj_ref, gw_ref):
    x = x_ref[...]
    hn = _rms(x, g_ref[...], NORM_EPS).astype(BF16)
    gate = jax.nn.sigmoid(_dot(hn, gw_ref[...]))
    return x + _dot(p_ref[...].astype(BF16), proj_ref[...]) * gate


def _ple_kv_kernel(x_ref, p_ref, g_ref, proj_ref, gw_ref, kvg_ref, kvw_ref, kvb_ref, pos_ref,
                   inv_ref, o_ref, k_ref, v_ref):
    x2 = _ple_core(x_ref, p_ref, g_ref, proj_ref, gw_ref)
    o_ref[...] = x2
    kv = _dot(_rms(x2, kvg_ref[...], NORM_EPS).astype(BF16), kvw_ref[...]) + kvb_ref[...]
    kvd = k_ref.shape[1]
    cos, sin_lo, sin_hi = _rope_tables(pos_ref, inv_ref)
    k_ref[...] = _rope_apply(kv[:, :kvd], cos, sin_lo, sin_hi).astype(k_ref.dtype)
    v_ref[...] = kv[:, kvd:].astype(v_ref.dtype)


def _ple_final_kernel(x_ref, p_ref, g_ref, proj_ref, gw_ref, fg_ref, o_ref):
    x2 = _ple_core(x_ref, p_ref, g_ref, proj_ref, gw_ref)
    o_ref[...] = _rms(x2, fg_ref[...], NORM_EPS)


def _ple_kv(x, p, g, proj, gw, kvg, kvw, kvb, pos, inv, tile):
    T, D = x.shape
    P = p.shape[1]
    kvd = kvw.shape[1] // 2
    tok = lambda w: pl.BlockSpec((tile, w), lambda i: (i, 0))
    return pl.pallas_call(
        _ple_kv_kernel,
        grid=(T // tile,),
        in_specs=[tok(D), tok(P), _const_spec((1, D)), _const_spec(proj.shape),
                  _const_spec(gw.shape), _const_spec((1, D)), _const_spec(kvw.shape),
                  _const_spec(kvb.shape), tok(1), _const_spec(inv.shape)],
        out_specs=[tok(D), tok(kvd), tok(kvd)],
        out_shape=[jax.ShapeDtypeStruct((T, D), F32), jax.ShapeDtypeStruct((T, kvd), BF16),
                   jax.ShapeDtypeStruct((T, kvd), BF16)],
        compiler_params=_params("arbitrary"),
        name="ple_kv",
    )(x, p, g, proj, gw, kvg, kvw, kvb, pos, inv)


def _ple_final(x, p, g, proj, gw, fg, tile):
    T, D = x.shape
    P = p.shape[1]
    tok = lambda w: pl.BlockSpec((tile, w), lambda i: (i, 0))
    return pl.pallas_call(
        _ple_final_kernel,
        grid=(T // tile,),
        in_specs=[tok(D), tok(P), _const_spec((1, D)), _const_spec(proj.shape),
                  _const_spec(gw.shape), _const_spec((1, D))],
        out_specs=tok(D),
        out_shape=jax.ShapeDtypeStruct((T, D), F32),
        compiler_params=_params("arbitrary"),
        name="ple_final",
    )(x, p, g, proj, gw, fg)


def _attn_kernel(x_ref, g_ref, qw_ref, qb_ref, sink_ref, ow_ref, ob_ref, kc_ref, kp_ref, vc_ref,
                 vp_ref, pos_ref, inv_ref, o_ref, q_ref, att_ref, *, n_q_heads):
    tile = x_ref.shape[0]
    W = WINDOW
    nblk = tile // W
    q_per_kv = n_q_heads // N_KV_HEADS
    scale = HEAD_DIM ** -0.5
    first = pl.program_id(1) == 0

    h = _rms(x_ref[...], g_ref[...], NORM_EPS).astype(BF16)
    q = _dot(h, qw_ref[...]) + qb_ref[...]
    cos, sin_lo, sin_hi = _rope_tables(pos_ref, inv_ref)
    for lg in range(q.shape[1] // LANES):
        cs = slice(lg * LANES, (lg + 1) * LANES)
        q_ref[:, cs] = _rope_apply(q[:, cs], cos, sin_lo, sin_hi).astype(BF16)

    qi = lax.broadcasted_iota(jnp.int32, (W, 2 * W), 0)
    kj = lax.broadcasted_iota(jnp.int32, (W, 2 * W), 1)
    first_lo = jnp.where(first, W, 0)
    band = jnp.where((kj > qi) & (kj <= qi + W), 0.0, -jnp.inf)
    band_first = jnp.where(kj >= first_lo, band, -jnp.inf)
    band = jnp.concatenate([band] * q_per_kv, axis=0)
    band_first = jnp.concatenate([band_first] * q_per_kv, axis=0)

    for n in range(nblk):
        rows = slice(n * W, (n + 1) * W)
        if n == 0:
            kprev, vprev = kp_ref[...], vp_ref[...]
        else:
            kprev, vprev = kc_ref[(n - 1) * W:n * W, :], vc_ref[(n - 1) * W:n * W, :]
        kblk = jnp.concatenate([kprev, kc_ref[rows, :]], axis=0)
        vblk = jnp.concatenate([vprev, vc_ref[rows, :]], axis=0)
        bias = band_first if n == 0 else band
        for g in range(N_KV_HEADS):
            ks = kblk[:, g * HEAD_DIM:(g + 1) * HEAD_DIM]
            vs = vblk[:, g * HEAD_DIM:(g + 1) * HEAD_DIM]
            qs = jnp.concatenate(
                [q_ref[rows, (g * q_per_kv + r) * HEAD_DIM:(g * q_per_kv + r + 1) * HEAD_DIM]
                 for r in range(q_per_kv)], axis=0)
            s = _dot(qs, ks, _NT) * scale + bias
            sink = jnp.concatenate(
                [jnp.broadcast_to(sink_ref[:, g * q_per_kv + r:g * q_per_kv + r + 1], (W, 1))
                 for r in range(q_per_kv)], axis=0)
            m = jnp.maximum(jnp.max(s, axis=-1, keepdims=True), sink)
            e = jnp.exp(s - m)
            denom = jnp.sum(e, axis=-1, keepdims=True) + jnp.exp(sink - m)
            pr = (e / denom).astype(BF16)
            o = _dot(pr, vs)
            for r in range(q_per_kv):
                hh = g * q_per_kv + r
                att_ref[rows, hh * HEAD_DIM:(hh + 1) * HEAD_DIM] = o[r * W:(r + 1) * W, :].astype(BF16)

    o_ref[...] = x_ref[...] + _dot(att_ref[...], ow_ref[...]) + ob_ref[...]


def _attn(x, g, qw, qb, sinks, ow, ob, k, v, pos, inv, batch, tile):
    T, D = x.shape
    kvd = k.shape[1]
    nq = qw.shape[1] // HEAD_DIM
    nt = T // batch // tile
    bpt = tile // WINDOW
    row = lambda b, i: (b * nt + i, 0)
    prev = lambda b, i: (jnp.maximum((b * nt + i) * bpt - 1, 0), 0)
    kern = functools.partial(_attn_kernel, n_q_heads=nq)
    return pl.pallas_call(
        kern,
        grid=(batch, nt),
        in_specs=[pl.BlockSpec((tile, D), row), _const_spec((1, D)), _const_spec(qw.shape),
                  _const_spec(qb.shape), _const_spec(sinks.shape), _const_spec(ow.shape),
                  _const_spec(ob.shape),
                  pl.BlockSpec((tile, kvd), row), pl.BlockSpec((WINDOW, kvd), prev),
                  pl.BlockSpec((tile, kvd), row), pl.BlockSpec((WINDOW, kvd), prev),
                  pl.BlockSpec((tile, 1), row), _const_spec(inv.shape)],
        out_specs=pl.BlockSpec((tile, D), row),
        out_shape=jax.ShapeDtypeStruct((T, D), F32),
        scratch_shapes=[pltpu.VMEM((tile, qw.shape[1]), BF16),
                        pltpu.VMEM((tile, qw.shape[1]), BF16)],
        compiler_params=_params("arbitrary", "arbitrary"),
        name="swa_attn",
    )(x, g, qw, qb, sinks, ow, ob, k, k, v, v, pos, inv)


def _row(v):
    return v.reshape(1, -1)


def kernel(x, p, positions, ssm_norm, ssm_in_w, ssm_conv_w, ssm_conv_b, ssm_dt_bias, ssm_A_log, ssm_D, ssm_gate_norm, ssm_out_w, kv_norm, kv_w, kv_b, attn_norm, q_w, q_b, sinks, o_w, o_b, peer_norm, peer_q_w, peer_sub_keys, peer_u, peer_v, ple_norm, ple_proj, ple_gate_w, final_norm):
    B, S, D = x.shape
    T = B * S
    depth = p.shape[0]
    n_a = ssm_norm.shape[0]
    H = ssm_D.shape[1]
    d_inner = H * SSM_HEADDIM
    conv_dim = ssm_conv_w.shape[2]

    xt = x.reshape(T, D)
    pos = positions.reshape(T, 1)
    lane = np.arange(LANES) % HEAD_DIM
    inv = np.where(lane < ROT_DIM,
                   ROPE_THETA ** (-(2.0 * (lane % (ROT_DIM // 2))) / ROT_DIM), 0.0)
    inv = jnp.asarray(inv.reshape(1, LANES), F32)
    expand = jnp.asarray(np.repeat(np.eye(H, dtype=np.float32), SSM_HEADDIM, axis=1))
    tril = jnp.asarray(np.tril(np.ones((SSD_CHUNK, SSD_CHUNK), np.float32)))

    k_sh = v_sh = None
    for i in range(depth):
        if i < n_a:
            w = ssm_in_w[i].astype(BF16)
            z, xbc, dtr = _inproj(xt, _row(ssm_norm[i]), w[:, :d_inner],
                                  w[:, d_inner:d_inner + conv_dim], w[:, d_inner + conv_dim:],
                                  tile=256)
            y = _ssd(z, xbc, dtr, ssm_conv_w[i], _row(ssm_conv_b[i]), _row(ssm_dt_bias[i]),
                     _row(ssm_A_log[i]), _row(jnp.repeat(ssm_D[i], SSM_HEADDIM)),
                     _row(ssm_gate_norm[i]), expand, tril, batch=B)
            xt = _outproj(xt, y, ssm_out_w[i].astype(BF16), tile=512)
        else:
            j = i - n_a
            xt = _attn(xt, _row(attn_norm[j]), q_w[j].astype(BF16), _row(q_b[j]), _row(sinks[j]),
                       o_w[j].astype(BF16), _row(o_b[j]), k_sh, v_sh, pos, inv, batch=B, tile=512)
        xt = _peer(xt, _row(peer_norm[i]), peer_q_w[i].T.astype(BF16),
                   peer_sub_keys[i].astype(BF16), peer_u[i].astype(BF16),
                   peer_v[i].T.astype(BF16), tile=512, ec=1024)
        if i == n_a - 1:
            xt, k_sh, v_sh = _ple_kv(xt, p[i].reshape(T, -1), _row(ple_norm[i]),
                                     ple_proj[i].astype(BF16), ple_gate_w[i].astype(BF16),
                                     _row(kv_norm), kv_w.astype(BF16), _row(kv_b), pos, inv,
                                     tile=512)
        elif i == depth - 1:
            xt = _ple_final(xt, p[i].reshape(T, -1), _row(ple_norm[i]), ple_proj[i].astype(BF16),
                            ple_gate_w[i].astype(BF16), _row(final_norm), tile=512)
        else:
            raise NotImplementedError("PLE without K/V or final norm")
    return xt.reshape(B, S, D)
```

```python
import functools
import math

import jax
import jax.numpy as jnp
import numpy as np
from jax import lax
from jax.experimental import pallas as pl
from jax.experimental.pallas import tpu as pltpu

F32 = jnp.float32
BF16 = jnp.bfloat16

NORM_EPS = 1e-6
GATED_NORM_EPS = 1e-5
SSM_HEADDIM = 64
SSM_GROUPS = 8
SSM_STATE = 128
CONV_K = 4
SSD_CHUNK = 128
HEAD_DIM = 64
N_KV_HEADS = 2
WINDOW = 128
ROT_DIM = HEAD_DIM // 4
ROPE_THETA = 500000.0
PEER_HEADS = 8
N_KEYS = 128
PEER_TOPK = 16

LANES = 128
SUBLANES = 8
VMEM_LIMIT = 56 * 1024 * 1024

_CAND_PAIRS = [(r1, r2) for r1 in range(PEER_TOPK + 1) for r2 in range(PEER_TOPK + 1)
               if (r1 + 1) * (r2 + 1) <= PEER_TOPK + 1]


def _params(*sem):
    return pltpu.CompilerParams(dimension_semantics=sem, vmem_limit_bytes=VMEM_LIMIT)


def _const_spec(shape):
    nd = len(shape)
    return pl.BlockSpec(shape, lambda *_: (0,) * nd, pipeline_mode=pl.Buffered(1))


def _rms(x, g, eps):
    return x * lax.rsqrt(jnp.mean(x * x, axis=-1, keepdims=True) + eps) * g


def _dot(a, b, dims=None, precision=None):
    if dims is None:
        dims = (((a.ndim - 1,), (0,)), ((), ()))
    return lax.dot_general(a, b, dims, precision=precision, preferred_element_type=F32)


_NT = (((1,), (1,)), ((), ()))
_TN = (((0,), (0,)), ((), ()))
_HI = lax.Precision.HIGHEST


def _inproj_kernel(x_ref, g_ref, wz_ref, wx_ref, wd_ref, z_ref, xbc_ref, dt_ref):
    h = _rms(x_ref[...], g_ref[...], NORM_EPS).astype(BF16)
    z_ref[...] = _dot(h, wz_ref[...])
    xbc_ref[...] = _dot(h, wx_ref[...])
    dt_ref[...] = _dot(h, wd_ref[...])


def _inproj(x, g, wz, wx, wd, tile):
    T, D = x.shape
    nz, nx, nd = wz.shape[1], wx.shape[1], wd.shape[1]
    return pl.pallas_call(
        _inproj_kernel,
        grid=(T // tile,),
        in_specs=[pl.BlockSpec((tile, D), lambda i: (i, 0)),
                  _const_spec((1, D)), _const_spec(wz.shape), _const_spec(wx.shape),
                  _const_spec(wd.shape)],
        out_specs=[pl.BlockSpec((tile, nz), lambda i: (i, 0)),
                   pl.BlockSpec((tile, nx), lambda i: (i, 0)),
                   pl.BlockSpec((tile, nd), lambda i: (i, 0))],
        out_shape=[jax.ShapeDtypeStruct((T, nz), F32), jax.ShapeDtypeStruct((T, nx), F32),
                   jax.ShapeDtypeStruct((T, nd), F32)],
        compiler_params=_params("arbitrary"),
        name="ssm_inproj",
    )(x, g, wz, wx, wd)


def _ssd_kernel(z_ref, xbc_ref, dtr_ref, cw_ref, cb_ref, dtb_ref, alog_ref, dexp_ref, gn_ref,
                expand_ref, tril_ref, y_ref, xb_ref, st_ref, *, d_inner, n_heads):
    L = SSD_CHUNK
    gw = d_inner // SSM_GROUPS
    hpg = n_heads // SSM_GROUPS
    tail = SUBLANES

    @pl.when(pl.program_id(1) == 0)
    def _():
        xb_ref[0:tail, :] = jnp.zeros((tail, xb_ref.shape[1]), F32)
        st_ref[...] = jnp.zeros(st_ref.shape, F32)

    xb_ref[tail:tail + L, :] = xbc_ref[...]

    def conv_silu(lo, width):
        acc = cb_ref[:, lo:lo + width]
        for k in range(CONV_K):
            off = tail - (CONV_K - 1) + k
            acc = acc + xb_ref[off:off + L, lo:lo + width] * cw_ref[k:k + 1, lo:lo + width]
        return acc * jax.nn.sigmoid(acc)

    dt_in = dtr_ref[...] + dtb_ref[...]
    dt = jnp.maximum(dt_in, 0.0) + jnp.log1p(jnp.exp(-jnp.abs(dt_in)))
    a = dt * (-jnp.exp(alog_ref[...]))
    tril = tril_ref[...]
    expand = expand_ref[...]
    a_cs = _dot(tril, a, precision=_HI)
    a_cs_t = a_cs.T
    a_cs_x = _dot(tril, _dot(a, expand, precision=_HI), precision=_HI)
    dt_x = _dot(dt, expand, precision=_HI)
    a_last_x = a_cs_x[L - 1:L, :]
    causal = tril > 0.5

    for g in range(SSM_GROUPS):
        lo = g * gw
        xs = conv_silu(lo, gw)
        bm = conv_silu(d_inner + g * SSM_STATE, SSM_STATE).astype(BF16)
        cm = conv_silu(d_inner + SSM_GROUPS * SSM_STATE + g * SSM_STATE, SSM_STATE).astype(BF16)
        xdt = xs * dt_x[:, lo:lo + gw]
        cb = _dot(cm, bm, _NT)
        yd = []
        for r in range(hpg):
            hh = g * hpg + r
            seg = a_cs[:, hh:hh + 1] - a_cs_t[hh:hh + 1, :]
            lmat = jnp.exp(jnp.where(causal, seg, -jnp.inf))
            m = (cb * lmat).astype(BF16)
            yd.append(_dot(m, xdt[:, r * SSM_HEADDIM:(r + 1) * SSM_HEADDIM].astype(BF16)))
        y = jnp.concatenate(yd, axis=1)
        acx = a_cs_x[:, lo:lo + gw]
        alx = a_last_x[:, lo:lo + gw]
        prev = st_ref[g]
        y = y + _dot(cm, prev.astype(BF16)) * jnp.exp(acx)
        xd = (xdt * jnp.exp(alx - acx)).astype(BF16)
        st_ref[g] = prev * jnp.exp(alx) + _dot(bm, xd, _TN)
        y = y + xs * dexp_ref[:, lo:lo + gw]
        zg = z_ref[:, lo:lo + gw]
        y = y * (zg * jax.nn.sigmoid(zg))
        y = y * lax.rsqrt(jnp.mean(y * y, axis=-1, keepdims=True) + GATED_NORM_EPS)
        y_ref[:, lo:lo + gw] = (y * gn_ref[:, lo:lo + gw]).astype(y_ref.dtype)

    xb_ref[0:tail, :] = xb_ref[L:L + tail, :]


def _ssd(z, xbc, dtr, cw, cb, dtb, alog, dexp, gn, expand, tril, batch):
    T, d_inner = z.shape
    conv_dim = xbc.shape[1]
    H = dtr.shape[1]
    L = SSD_CHUNK
    nc = T // batch // L
    row = lambda b, c: (b * nc + c, 0)
    kern = functools.partial(_ssd_kernel, d_inner=d_inner, n_heads=H)
    return pl.pallas_call(
        kern,
        grid=(batch, nc),
        in_specs=[pl.BlockSpec((L, d_inner), row), pl.BlockSpec((L, conv_dim), row),
                  pl.BlockSpec((L, H), row),
                  _const_spec(cw.shape), _const_spec(cb.shape), _const_spec(dtb.shape),
                  _const_spec(alog.shape), _const_spec(dexp.shape), _const_spec(gn.shape),
                  _const_spec(expand.shape), _const_spec(tril.shape)],
        out_specs=pl.BlockSpec((L, d_inner), row),
        out_shape=jax.ShapeDtypeStruct((T, d_inner), BF16),
        scratch_shapes=[pltpu.VMEM((L + SUBLANES, conv_dim), F32),
                        pltpu.VMEM((SSM_GROUPS, SSM_STATE, d_inner // SSM_GROUPS), F32)],
        compiler_params=_params("arbitrary", "arbitrary"),
        name="ssd_scan",
    )(z, xbc, dtr, cw, cb, dtb, alog, dexp, gn, expand, tril)


def _outproj_kernel(x_ref, y_ref, w_ref, o_ref):
    o_ref[...] = x_ref[...] + _dot(y_ref[...], w_ref[...])


def _outproj(x, y, w, tile):
    T, D = x.shape
    K = y.shape[1]
    return pl.pallas_call(
        _outproj_kernel,
        grid=(T // tile,),
        in_specs=[pl.BlockSpec((tile, D), lambda i: (i, 0)),
                  pl.BlockSpec((tile, K), lambda i: (i, 0)), _const_spec(w.shape)],
        out_specs=pl.BlockSpec((tile, D), lambda i: (i, 0)),
        out_shape=jax.ShapeDtypeStruct((T, D), F32),
        compiler_params=_params("arbitrary"),
        name="ssm_outproj",
    )(x, y, w)


def _peer_kernel(x_ref, g_ref, wq_ref, keys_ref, u0_ref, ub_ref, uc_ref, vta_ref, vtb_ref, o_ref,
                 hb_ref, s1_ref, s2_ref, e1_ref, e2_ref, gmin_ref, work_ref, top_ref, cand_ref,
                 a0_ref, a1_ref, p0_ref, p1_ref, acca_ref, accb_ref):
    g = pl.program_id(1)
    last = pl.num_programs(1) - 1
    tt = x_ref.shape[0]
    ec = ub_ref.shape[0]
    rows_per_chunk = ec // N_KEYS
    half = N_KEYS
    ntop = PEER_TOPK + 1
    neg_inf = -jnp.inf

    @pl.when(g == 0)
    def _route():
        hb = _rms(x_ref[...], g_ref[...], NORM_EPS).astype(BF16)
        hb_ref[...] = hb
        for h in range(PEER_HEADS):
            q = _dot(wq_ref[h * 2 * half:(h + 1) * 2 * half, :], hb, _NT).astype(BF16)
            s1_ref[h] = _dot(keys_ref[0], q[0:half, :])
            s2_ref[h] = _dot(keys_ref[1], q[half:2 * half, :])

        for c, s_ref in enumerate((s1_ref, s2_ref)):
            for h in range(PEER_HEADS):
                work_ref[...] = s_ref[h]

                def top_body(r, carry, c=c, h=h):
                    w = work_ref[...]
                    m = jnp.max(w, axis=0, keepdims=True)
                    top_ref[c, r, h:h + 1, :] = m
                    work_ref[...] = jnp.where(w == m, neg_inf, w)
                    return carry

                lax.fori_loop(0, ntop, top_body, 0)

        for idx, (r1, r2) in enumerate(_CAND_PAIRS):
            cand_ref[idx] = top_ref[0, r1] + top_ref[1, r2]
        a0 = top_ref[0, 0]
        b0 = top_ref[1, 0]
        m0 = a0 + b0

        def pop_max():
            m = cand_ref[0]
            for idx in range(1, len(_CAND_PAIRS)):
                m = jnp.maximum(m, cand_ref[idx])
            for idx in range(len(_CAND_PAIRS)):
                cv = cand_ref[idx]
                cand_ref[idx] = jnp.where(cv == m, neg_inf, cv)
            return m

        def cand_body(k, carry):
            z, _ = carry
            m = pop_max()
            return z + jnp.exp(m - m0), m

        z, v16 = lax.fori_loop(0, PEER_TOPK, cand_body, (jnp.zeros(m0.shape, F32), m0))
        v17 = pop_max()
        zinv = 1.0 / z
        gmin = jnp.exp(0.5 * (v16 + v17) - m0) * zinv
        for h in range(PEER_HEADS):
            e1 = jnp.exp(s1_ref[h] - a0[h:h + 1, :])
            e2 = jnp.exp(s2_ref[h] - b0[h:h + 1, :]) * zinv[h:h + 1, :]
            for tc in range(tt // LANES):
                cs = slice(tc * LANES, (tc + 1) * LANES)
                e1_ref[h, tc] = e1[:, cs]
                e2_ref[h, tc] = e2[:, cs]
                gmin_ref[h, tc] = gmin[h:h + 1, cs]

        p1_ref[...] = jnp.zeros(p1_ref.shape, BF16)
        acca_ref[...] = jnp.zeros(acca_ref.shape, F32)
        accb_ref[...] = jnp.zeros(accb_ref.shape, F32)
        a0_ref[...] = _dot(u0_ref[...], hb_ref[...], _NT)

    mxu_cols = 2 * LANES
    n_piece = tt // mxu_cols

    def expert_scores(u_ref, a_ref, piece):
        cs = slice(piece * mxu_cols, (piece + 1) * mxu_cols)
        a_ref[:, cs] = _dot(u_ref[...], hb_ref[cs, :], _NT)

    def accumulate(vt_ref, p_ref, acc_ref, piece):
        cs = slice(piece * mxu_cols, (piece + 1) * mxu_cols)
        acc_ref[:, cs] += _dot(vt_ref[...], p_ref[:, cs])

    def gate_gelu(a_ref, p_ref, chunk, tc):
        i0 = pl.multiple_of(chunk * rows_per_chunk, SUBLANES)
        cs = slice(tc * LANES, (tc + 1) * LANES)
        e18 = [e1_ref[h, tc, pl.ds(i0, rows_per_chunk), :] for h in range(PEER_HEADS)]
        gm = [gmin_ref[h, tc] for h in range(PEER_HEADS)]
        for ii in range(rows_per_chunk):
            rows = slice(ii * N_KEYS, (ii + 1) * N_KEYS)
            w = jnp.zeros((N_KEYS, LANES), F32)
            for h in range(PEER_HEADS):
                gate = e18[h][ii:ii + 1, :] * e2_ref[h, tc]
                w = w + jnp.where(gate >= gm[h], gate, 0.0)
            av = a_ref[rows, cs]
            gelu = 0.5 * av * (1.0 + lax.erf(av * np.float32(math.sqrt(0.5))))
            p_ref[rows, cs] = (w * gelu).astype(BF16)

    @pl.when(g < last)
    def _steady():
        for piece in range(n_piece):
            gate_gelu(a0_ref, p0_ref, 2 * g, 2 * piece)
            accumulate(vta_ref, p1_ref, acca_ref, piece)
            gate_gelu(a0_ref, p0_ref, 2 * g, 2 * piece + 1)
            expert_scores(ub_ref, a1_ref, piece)
        for piece in range(n_piece):
            gate_gelu(a1_ref, p1_ref, 2 * g + 1, 2 * piece)
            accumulate(vtb_ref, p0_ref, accb_ref, piece)
            gate_gelu(a1_ref, p1_ref, 2 * g + 1, 2 * piece + 1)
            expert_scores(uc_ref, a0_ref, piece)

    @pl.when(g == last)
    def _finish():
        for piece in range(n_piece):
            accumulate(vta_ref, p1_ref, acca_ref, piece)
        o_ref[...] = x_ref[...] + (acca_ref[...] + accb_ref[...]).T


def _peer(x, g, wq_t, keys, u, vt, tile, ec):
    T, D = x.shape
    E = u.shape[0]
    ncand = len(_CAND_PAIRS)
    nchunk = E // ec
    assert E == N_KEYS * N_KEYS and ec == SUBLANES * N_KEYS and tile % LANES == 0
    assert nchunk % 2 == 0
    ntc = tile // LANES
    u_spec = lambda off: pl.BlockSpec(
        (ec, D), lambda i, s: (jnp.minimum(2 * s + off, nchunk - 1), 0))
    vt_spec = lambda off: pl.BlockSpec(
        (D, ec), lambda i, s: (0, jnp.clip(2 * s + off, 0, nchunk - 1)))
    return pl.pallas_call(
        _peer_kernel,
        grid=(T // tile, nchunk // 2 + 1),
        in_specs=[pl.BlockSpec((tile, D), lambda i, s: (i, 0)),
                  _const_spec((1, D)), _const_spec(wq_t.shape), _const_spec(keys.shape),
                  pl.BlockSpec((ec, D), lambda i, s: (0, 0), pipeline_mode=pl.Buffered(1)),
                  u_spec(1), u_spec(2), vt_spec(-1), vt_spec(0)],
        out_specs=pl.BlockSpec((tile, D), lambda i, s: (i, 0)),
        out_shape=jax.ShapeDtypeStruct((T, D), F32),
        scratch_shapes=[pltpu.VMEM((tile, D), BF16),
                        pltpu.VMEM((PEER_HEADS, N_KEYS, tile), F32),
                        pltpu.VMEM((PEER_HEADS, N_KEYS, tile), F32),
                        pltpu.VMEM((PEER_HEADS, ntc, N_KEYS, LANES), F32),
                        pltpu.VMEM((PEER_HEADS, ntc, N_KEYS, LANES), F32),
                        pltpu.VMEM((PEER_HEADS, ntc, 1, LANES), F32),
                        pltpu.VMEM((N_KEYS, tile), F32),
                        pltpu.VMEM((2, PEER_TOPK + 1, PEER_HEADS, tile), F32),
                        pltpu.VMEM((ncand, PEER_HEADS, tile), F32),
                        pltpu.VMEM((ec, tile), F32), pltpu.VMEM((ec, tile), F32),
                        pltpu.VMEM((ec, tile), BF16), pltpu.VMEM((ec, tile), BF16),
                        pltpu.VMEM((D, tile), F32), pltpu.VMEM((D, tile), F32)],
        compiler_params=_params("arbitrary", "arbitrary"),
        name="peer",
    )(x, g, wq_t, keys, u, u, u, vt, vt)


def _rope_tables(pos_ref, inv_ref):
    ang = pos_ref[...].astype(F32) * inv_ref[...]
    lane = lax.broadcasted_iota(jnp.int32, ang.shape, 1) % HEAD_DIM
    cos = jnp.cos(ang)
    sin = jnp.sin(ang)
    half = ROT_DIM // 2
    sin_lo = jnp.where(lane < half, -sin, 0.0)
    sin_hi = jnp.where((lane >= half) & (lane < ROT_DIM), sin, 0.0)
    return cos, sin_lo, sin_hi


def _rope_apply(t, cos, sin_lo, sin_hi):
    half = ROT_DIM // 2
    up = pltpu.roll(t, LANES - half, 1)
    dn = pltpu.roll(t, half, 1)
    return t * cos + up * sin_lo + dn * sin_hi


def _ple_core(x_ref, p_ref, g_ref, proj_ref, gw_ref):
    x = x_ref[...]
    hn = _rms(x, g_ref[...], NORM_EPS).astype(BF16)
    gate = jax.nn.sigmoid(_dot(hn, gw_ref[...]))
    return x + _dot(p_ref[...].astype(BF16), proj_ref[...]) * gate


def _ple_kv_kernel(x_ref, p_ref, g_ref, proj_ref, gw_ref, kvg_ref, kvw_ref, kvb_ref, pos_ref,
                   inv_ref, o_ref, k_ref, v_ref):
    x2 = _ple_core(x_ref, p_ref, g_ref, proj_ref, gw_ref)
    o_ref[...] = x2
    kv = _dot(_rms(x2, kvg_ref[...], NORM_EPS).astype(BF16), kvw_ref[...]) + kvb_ref[...]
    kvd = k_ref.shape[1]
    cos, sin_lo, sin_hi = _rope_tables(pos_ref, inv_ref)
    k_ref[...] = _rope_apply(kv[:, :kvd], cos, sin_lo, sin_hi).astype(k_ref.dtype)
    v_ref[...] = kv[:, kvd:].astype(v_ref.dtype)


def _ple_final_kernel(x_ref, p_ref, g_ref, proj_ref, gw_ref, fg_ref, o_ref):
    x2 = _ple_core(x_ref, p_ref, g_ref, proj_ref, gw_ref)
    o_ref[...] = _rms(x2, fg_ref[...], NORM_EPS)


def _ple_kv(x, p, g, proj, gw, kvg, kvw, kvb, pos, inv, tile):
    T, D = x.shape
    P = p.shape[1]
    kvd = kvw.shape[1] // 2
    tok = lambda w: pl.BlockSpec((tile, w), lambda i: (i, 0))
    return pl.pallas_call(
        _ple_kv_kernel,
        grid=(T // tile,),
        in_specs=[tok(D), tok(P), _const_spec((1, D)), _const_spec(proj.shape),
                  _const_spec(gw.shape), _const_spec((1, D)), _const_spec(kvw.shape),
                  _const_spec(kvb.shape), tok(1), _const_spec(inv.shape)],
        out_specs=[tok(D), tok(kvd), tok(kvd)],
        out_shape=[jax.ShapeDtypeStruct((T, D), F32), jax.ShapeDtypeStruct((T, kvd), BF16),
                   jax.ShapeDtypeStruct((T, kvd), BF16)],
        compiler_params=_params("arbitrary"),
        name="ple_kv",
    )(x, p, g, proj, gw, kvg, kvw, kvb, pos, inv)


def _ple_final(x, p, g, proj, gw, fg, tile):
    T, D = x.shape
    P = p.shape[1]
    tok = lambda w: pl.BlockSpec((tile, w), lambda i: (i, 0))
    return pl.pallas_call(
        _ple_final_kernel,
        grid=(T // tile,),
        in_specs=[tok(D), tok(P), _const_spec((1, D)), _const_spec(proj.shape),
                  _const_spec(gw.shape), _const_spec((1, D))],
        out_specs=tok(D),
        out_shape=jax.ShapeDtypeStruct((T, D), F32),
        compiler_params=_params("arbitrary"),
        name="ple_final",
    )(x, p, g, proj, gw, fg)


def _attn_kernel(x_ref, g_ref, qw_ref, qb_ref, sink_ref, ow_ref, ob_ref, kc_ref, kp_ref, vc_ref,
                 vp_ref, pos_ref, inv_ref, o_ref, q_ref, att_ref, *, n_q_heads):
    tile = x_ref.shape[0]
    W = WINDOW
    nblk = tile // W
    q_per_kv = n_q_heads // N_KV_HEADS
    scale = HEAD_DIM ** -0.5
    first = pl.program_id(1) == 0

    h = _rms(x_ref[...], g_ref[...], NORM_EPS).astype(BF16)
    q = _dot(h, qw_ref[...]) + qb_ref[...]
    cos, sin_lo, sin_hi = _rope_tables(pos_ref, inv_ref)
    for lg in range(q.shape[1] // LANES):
        cs = slice(lg * LANES, (lg + 1) * LANES)
        q_ref[:, cs] = _rope_apply(q[:, cs], cos, sin_lo, sin_hi).astype(BF16)

    qi = lax.broadcasted_iota(jnp.int32, (W, 2 * W), 0)
    kj = lax.broadcasted_iota(jnp.int32, (W, 2 * W), 1)
    first_lo = jnp.where(first, W, 0)
    band = jnp.where((kj > qi) & (kj <= qi + W), 0.0, -jnp.inf)
    band_first = jnp.where(kj >= first_lo, band, -jnp.inf)
    band = jnp.concatenate([band] * q_per_kv, axis=0)
    band_first = jnp.concatenate([band_first] * q_per_kv, axis=0)

    for n in range(nblk):
        rows = slice(n * W, (n + 1) * W)
        if n == 0:
            kprev, vprev = kp_ref[...], vp_ref[...]
        else:
            kprev, vprev = kc_ref[(n - 1) * W:n * W, :], vc_ref[(n - 1) * W:n * W, :]
        kblk = jnp.concatenate([kprev, kc_ref[rows, :]], axis=0)
        vblk = jnp.concatenate([vprev, vc_ref[rows, :]], axis=0)
        bias = band_first if n == 0 else band
        for g in range(N_KV_HEADS):
            ks = kblk[:, g * HEAD_DIM:(g + 1) * HEAD_DIM]
            vs = vblk[:, g * HEAD_DIM:(g + 1) * HEAD_DIM]
            qs = jnp.concatenate(
                [q_ref[rows, (g * q_per_kv + r) * HEAD_DIM:(g * q_per_kv + r + 1) * HEAD_DIM]
                 for r in range(q_per_kv)], axis=0)
            s = _dot(qs, ks, _NT) * scale + bias
            sink = jnp.concatenate(
                [jnp.broadcast_to(sink_ref[:, g * q_per_kv + r:g * q_per_kv + r + 1], (W, 1))
                 for r in range(q_per_kv)], axis=0)
            m = jnp.maximum(jnp.max(s, axis=-1, keepdims=True), sink)
            e = jnp.exp(s - m)
            denom = jnp.sum(e, axis=-1, keepdims=True) + jnp.exp(sink - m)
            pr = (e / denom).astype(BF16)
            o = _dot(pr, vs)
            for r in range(q_per_kv):
                hh = g * q_per_kv + r
                att_ref[rows, hh * HEAD_DIM:(hh + 1) * HEAD_DIM] = o[r * W:(r + 1) * W, :].astype(BF16)

    o_ref[...] = x_ref[...] + _dot(att_ref[...], ow_ref[...]) + ob_ref[...]


def _attn(x, g, qw, qb, sinks, ow, ob, k, v, pos, inv, batch, tile):
    T, D = x.shape
    kvd = k.shape[1]
    nq = qw.shape[1] // HEAD_DIM
    nt = T // batch // tile
    bpt = tile // WINDOW
    row = lambda b, i: (b * nt + i, 0)
    prev = lambda b, i: (jnp.maximum((b * nt + i) * bpt - 1, 0), 0)
    kern = functools.partial(_attn_kernel, n_q_heads=nq)
    return pl.pallas_call(
        kern,
        grid=(batch, nt),
        in_specs=[pl.BlockSpec((tile, D), row), _const_spec((1, D)), _const_spec(qw.shape),
                  _const_spec(qb.shape), _const_spec(sinks.shape), _const_spec(ow.shape),
                  _const_spec(ob.shape),
                  pl.BlockSpec((tile, kvd), row), pl.BlockSpec((WINDOW, kvd), prev),
                  pl.BlockSpec((tile, kvd), row), pl.BlockSpec((WINDOW, kvd), prev),
                  pl.BlockSpec((tile, 1), row), _const_spec(inv.shape)],
        out_specs=pl.BlockSpec((tile, D), row),
        out_shape=jax.ShapeDtypeStruct((T, D), F32),
        scratch_shapes=[pltpu.VMEM((tile, qw.shape[1]), BF16),
                        pltpu.VMEM((tile, qw.shape[1]), BF16)],
        compiler_params=_params("arbitrary", "arbitrary"),
        name="swa_attn",
    )(x, g, qw, qb, sinks, ow, ob, k, k, v, v, pos, inv)


def _row(v):
    return v.reshape(1, -1)


def kernel(x, p, positions, ssm_norm, ssm_in_w, ssm_conv_w, ssm_conv_b, ssm_dt_bias, ssm_A_log, ssm_D, ssm_gate_norm, ssm_out_w, kv_norm, kv_w, kv_b, attn_norm, q_w, q_b, sinks, o_w, o_b, peer_norm, peer_q_w, peer_sub_keys, peer_u, peer_v, ple_norm, ple_proj, ple_gate_w, final_norm):
    B, S, D = x.shape
    T = B * S
    depth = p.shape[0]
    n_a = ssm_norm.shape[0]
    H = ssm_D.shape[1]
    d_inner = H * SSM_HEADDIM
    conv_dim = ssm_conv_w.shape[2]

    xt = x.reshape(T, D)
    pos = positions.reshape(T, 1)
    lane = np.arange(LANES) % HEAD_DIM
    inv = np.where(lane < ROT_DIM,
                   ROPE_THETA ** (-(2.0 * (lane % (ROT_DIM // 2))) / ROT_DIM), 0.0)
    inv = jnp.asarray(inv.reshape(1, LANES), F32)
    expand = jnp.asarray(np.repeat(np.eye(H, dtype=np.float32), SSM_HEADDIM, axis=1))
    tril = jnp.asarray(np.tril(np.ones((SSD_CHUNK, SSD_CHUNK), np.float32)))

    k_sh = v_sh = None
    for i in range(depth):
        if i < n_a:
            w = ssm_in_w[i].astype(BF16)
            z, xbc, dtr = _inproj(xt, _row(ssm_norm[i]), w[:, :d_inner],
                                  w[:, d_inner:d_inner + conv_dim], w[:, d_inner + conv_dim:],
                                  tile=256)
            y = _ssd(z, xbc, dtr, ssm_conv_w[i], _row(ssm_conv_b[i]), _row(ssm_dt_bias[i]),
                     _row(ssm_A_log[i]), _row(jnp.repeat(ssm_D[i], SSM_HEADDIM)),
                     _row(ssm_gate_norm[i]), expand, tril, batch=B)
            xt = _outproj(xt, y, ssm_out_w[i].astype(BF16), tile=512)
        else:
            j = i - n_a
            xt = _attn(xt, _row(attn_norm[j]), q_w[j].astype(BF16), _row(q_b[j]), _row(sinks[j]),
                       o_w[j].astype(BF16), _row(o_b[j]), k_sh, v_sh, pos, inv, batch=B, tile=512)
        xt = _peer(xt, _row(peer_norm[i]), peer_q_w[i].T.astype(BF16),
                   peer_sub_keys[i].astype(BF16), peer_u[i].astype(BF16),
                   peer_v[i].T.astype(BF16), tile=512, ec=1024)
        if i == n_a - 1:
            xt, k_sh, v_sh = _ple_kv(xt, p[i].reshape(T, -1), _row(ple_norm[i]),
                                     ple_proj[i].astype(BF16), ple_gate_w[i].astype(BF16),
                                     _row(kv_norm), kv_w.astype(BF16), _row(kv_b), pos, inv,
                                     tile=512)
        elif i == depth - 1:
            xt = _ple_final(xt, p[i].reshape(T, -1), _row(ple_norm[i]), ple_proj[i].astype(BF16),
                            ple_gate_w[i].astype(BF16), _row(final_norm), tile=512)
        else:
            raise NotImplementedError("PLE without K/V or final norm")
    return xt.reshape(B, S, D)
```

```python
import functools
import math

import jax
import jax.numpy as jnp
import numpy as np
from jax import lax
from jax.experimental import pallas as pl
from jax.experimental.pallas import tpu as pltpu

F32 = jnp.float32
BF16 = jnp.bfloat16

NORM_EPS = 1e-6
GATED_NORM_EPS = 1e-5
SSM_HEADDIM = 64
SSM_GROUPS = 8
SSM_STATE = 128
CONV_K = 4
SSD_CHUNK = 128
HEAD_DIM = 64
N_KV_HEADS = 2
WINDOW = 128
ROT_DIM = HEAD_DIM // 4
ROPE_THETA = 500000.0
PEER_HEADS = 8
N_KEYS = 128
PEER_TOPK = 16

LANES = 128
SUBLANES = 8
VMEM_LIMIT = 56 * 1024 * 1024

_CAND_PAIRS = [(r1, r2) for r1 in range(PEER_TOPK + 1) for r2 in range(PEER_TOPK + 1)
               if (r1 + 1) * (r2 + 1) <= PEER_TOPK + 1]


def _params(*sem):
    return pltpu.CompilerParams(dimension_semantics=sem, vmem_limit_bytes=VMEM_LIMIT)


def _const_spec(shape):
    nd = len(shape)
    return pl.BlockSpec(shape, lambda *_: (0,) * nd, pipeline_mode=pl.Buffered(1))


def _rms(x, g, eps):
    return x * lax.rsqrt(jnp.mean(x * x, axis=-1, keepdims=True) + eps) * g


def _dot(a, b, dims=None, precision=None):
    if dims is None:
        dims = (((a.ndim - 1,), (0,)), ((), ()))
    return lax.dot_general(a, b, dims, precision=precision, preferred_element_type=F32)


_NT = (((1,), (1,)), ((), ()))
_TN = (((0,), (0,)), ((), ()))
_HI = lax.Precision.HIGHEST


def _inproj_kernel(x_ref, g_ref, wz_ref, wx_ref, wd_ref, z_ref, xbc_ref, dt_ref):
    h = _rms(x_ref[...], g_ref[...], NORM_EPS).astype(BF16)
    z_ref[...] = _dot(h, wz_ref[...])
    xbc_ref[...] = _dot(h, wx_ref[...])
    dt_ref[...] = _dot(h, wd_ref[...])


def _inproj(x, g, wz, wx, wd, tile):
    T, D = x.shape
    nz, nx, nd = wz.shape[1], wx.shape[1], wd.shape[1]
    return pl.pallas_call(
        _inproj_kernel,
        grid=(T // tile,),
        in_specs=[pl.BlockSpec((tile, D), lambda i: (i, 0)),
                  _const_spec((1, D)), _const_spec(wz.shape), _const_spec(wx.shape),
                  _const_spec(wd.shape)],
        out_specs=[pl.BlockSpec((tile, nz), lambda i: (i, 0)),
                   pl.BlockSpec((tile, nx), lambda i: (i, 0)),
                   pl.BlockSpec((tile, nd), lambda i: (i, 0))],
        out_shape=[jax.ShapeDtypeStruct((T, nz), F32), jax.ShapeDtypeStruct((T, nx), F32),
                   jax.ShapeDtypeStruct((T, nd), F32)],
        compiler_params=_params("arbitrary"),
        name="ssm_inproj",
    )(x, g, wz, wx, wd)


def _ssd_kernel(z_ref, xbc_ref, dtr_ref, cw_ref, cb_ref, dtb_ref, alog_ref, dexp_ref, gn_ref,
                expand_ref, tril_ref, y_ref, xb_ref, st_ref, *, d_inner, n_heads):
    L = SSD_CHUNK
    gw = d_inner // SSM_GROUPS
    hpg = n_heads // SSM_GROUPS
    tail = SUBLANES

    @pl.when(pl.program_id(1) == 0)
    def _():
        xb_ref[0:tail, :] = jnp.zeros((tail, xb_ref.shape[1]), F32)
        st_ref[...] = jnp.zeros(st_ref.shape, F32)

    xb_ref[tail:tail + L, :] = xbc_ref[...]

    def conv_silu(lo, width):
        acc = cb_ref[:, lo:lo + width]
        for k in range(CONV_K):
            off = tail - (CONV_K - 1) + k
            acc = acc + xb_ref[off:off + L, lo:lo + width] * cw_ref[k:k + 1, lo:lo + width]
        return acc * jax.nn.sigmoid(acc)

    dt_in = dtr_ref[...] + dtb_ref[...]
    dt = jnp.maximum(dt_in, 0.0) + jnp.log1p(jnp.exp(-jnp.abs(dt_in)))
    a = dt * (-jnp.exp(alog_ref[...]))
    tril = tril_ref[...]
    expand = expand_ref[...]
    a_cs = _dot(tril, a, precision=_HI)
    a_cs_t = a_cs.T
    a_cs_x = _dot(tril, _dot(a, expand, precision=_HI), precision=_HI)
    dt_x = _dot(dt, expand, precision=_HI)
    a_last_x = a_cs_x[L - 1:L, :]
    causal = tril > 0.5

    for g in range(SSM_GROUPS):
        lo = g * gw
        xs = conv_silu(lo, gw)
        bm = conv_silu(d_inner + g * SSM_STATE, SSM_STATE).astype(BF16)
        cm = conv_silu(d_inner + SSM_GROUPS * SSM_STATE + g * SSM_STATE, SSM_STATE).astype(BF16)
        xdt = xs * dt_x[:, lo:lo + gw]
        cb = _dot(cm, bm, _NT)
        yd = []
        for r in range(hpg):
            hh = g * hpg + r
            seg = a_cs[:, hh:hh + 1] - a_cs_t[hh:hh + 1, :]
            lmat = jnp.exp(jnp.where(causal, seg, -jnp.inf))
            m = (cb * lmat).astype(BF16)
            yd.append(_dot(m, xdt[:, r * SSM_HEADDIM:(r + 1) * SSM_HEADDIM].astype(BF16)))
        y = jnp.concatenate(yd, axis=1)
        acx = a_cs_x[:, lo:lo + gw]
        alx = a_last_x[:, lo:lo + gw]
        prev = st_ref[g]
        y = y + _dot(cm, prev.astype(BF16)) * jnp.exp(acx)
        xd = (xdt * jnp.exp(alx - acx)).astype(BF16)
        st_ref[g] = prev * jnp.exp(alx) + _dot(bm, xd, _TN)
        y = y + xs * dexp_ref[:, lo:lo + gw]
        zg = z_ref[:, lo:lo + gw]
        y = y * (zg * jax.nn.sigmoid(zg))
        y = y * lax.rsqrt(jnp.mean(y * y, axis=-1, keepdims=True) + GATED_NORM_EPS)
        y_ref[:, lo:lo + gw] = (y * gn_ref[:, lo:lo + gw]).astype(y_ref.dtype)

    xb_ref[0:tail, :] = xb_ref[L:L + tail, :]


def _ssd(z, xbc, dtr, cw, cb, dtb, alog, dexp, gn, expand, tril, batch):
    T, d_inner = z.shape
    conv_dim = xbc.shape[1]
    H = dtr.shape[1]
    L = SSD_CHUNK
    nc = T // batch // L
    row = lambda b, c: (b * nc + c, 0)
    kern = functools.partial(_ssd_kernel, d_inner=d_inner, n_heads=H)
    return pl.pallas_call(
        kern,
        grid=(batch, nc),
        in_specs=[pl.BlockSpec((L, d_inner), row), pl.BlockSpec((L, conv_dim), row),
                  pl.BlockSpec((L, H), row),
                  _const_spec(cw.shape), _const_spec(cb.shape), _const_spec(dtb.shape),
                  _const_spec(alog.shape), _const_spec(dexp.shape), _const_spec(gn.shape),
                  _const_spec(expand.shape), _const_spec(tril.shape)],
        out_specs=pl.BlockSpec((L, d_inner), row),
        out_shape=jax.ShapeDtypeStruct((T, d_inner), BF16),
        scratch_shapes=[pltpu.VMEM((L + SUBLANES, conv_dim), F32),
                        pltpu.VMEM((SSM_GROUPS, SSM_STATE, d_inner // SSM_GROUPS), F32)],
        compiler_params=_params("arbitrary", "arbitrary"),
        name="ssd_scan",
    )(z, xbc, dtr, cw, cb, dtb, alog, dexp, gn, expand, tril)


def _outproj_kernel(x_ref, y_ref, w_ref, o_ref):
    o_ref[...] = x_ref[...] + _dot(y_ref[...], w_ref[...])


def _outproj(x, y, w, tile):
    T, D = x.shape
    K = y.shape[1]
    return pl.pallas_call(
        _outproj_kernel,
        grid=(T // tile,),
        in_specs=[pl.BlockSpec((tile, D), lambda i: (i, 0)),
                  pl.BlockSpec((tile, K), lambda i: (i, 0)), _const_spec(w.shape)],
        out_specs=pl.BlockSpec((tile, D), lambda i: (i, 0)),
        out_shape=jax.ShapeDtypeStruct((T, D), F32),
        compiler_params=_params("arbitrary"),
        name="ssm_outproj",
    )(x, y, w)


def _peer_kernel(x_ref, g_ref, wq_ref, keys_ref, u0_ref, un_ref, vt_ref, o_ref,
                 hb_ref, s1_ref, s2_ref, e1_ref, e2_ref, gmin_ref, work_ref, top_ref, cand_ref,
                 a0_ref, a1_ref, p0_ref, p1_ref, acca_ref, accb_ref):
    g = pl.program_id(1)
    last = pl.num_programs(1) - 1
    tt = x_ref.shape[0]
    ec = un_ref.shape[0] // 2
    rows_per_chunk = ec // N_KEYS
    half = N_KEYS
    ntop = PEER_TOPK + 1
    neg_inf = -jnp.inf

    @pl.when(g == 0)
    def _route():
        hb = _rms(x_ref[...], g_ref[...], NORM_EPS).T.astype(BF16)
        hb_ref[...] = hb
        for h in range(PEER_HEADS):
            q = _dot(wq_ref[h * 2 * half:(h + 1) * 2 * half, :], hb).astype(BF16)
            s1_ref[h] = _dot(keys_ref[0], q[0:half, :])
            s2_ref[h] = _dot(keys_ref[1], q[half:2 * half, :])

        for c, s_ref in enumerate((s1_ref, s2_ref)):
            for h in range(PEER_HEADS):
                work_ref[...] = s_ref[h]

                def top_body(r, carry, c=c, h=h):
                    w = work_ref[...]
                    m = jnp.max(w, axis=0, keepdims=True)
                    top_ref[c, r, h:h + 1, :] = m
                    work_ref[...] = jnp.where(w == m, neg_inf, w)
                    return carry

                lax.fori_loop(0, ntop, top_body, 0)

        for idx, (r1, r2) in enumerate(_CAND_PAIRS):
            cand_ref[idx] = top_ref[0, r1] + top_ref[1, r2]
        a0 = top_ref[0, 0]
        b0 = top_ref[1, 0]
        m0 = a0 + b0

        def pop_max():
            m = cand_ref[0]
            for idx in range(1, len(_CAND_PAIRS)):
                m = jnp.maximum(m, cand_ref[idx])
            for idx in range(len(_CAND_PAIRS)):
                cv = cand_ref[idx]
                cand_ref[idx] = jnp.where(cv == m, neg_inf, cv)
            return m

        def cand_body(k, carry):
            z, _ = carry
            m = pop_max()
            return z + jnp.exp(m - m0), m

        z, v16 = lax.fori_loop(0, PEER_TOPK, cand_body, (jnp.zeros(m0.shape, F32), m0))
        v17 = pop_max()
        zinv = 1.0 / z
        gmin = jnp.exp(0.5 * (v16 + v17) - m0) * zinv
        for h in range(PEER_HEADS):
            e1 = jnp.exp(s1_ref[h] - a0[h:h + 1, :])
            e2 = jnp.exp(s2_ref[h] - b0[h:h + 1, :]) * zinv[h:h + 1, :]
            for tc in range(tt // LANES):
                cs = slice(tc * LANES, (tc + 1) * LANES)
                e1_ref[h, tc] = e1[:, cs]
                e2_ref[h, tc] = e2[:, cs]
                gmin_ref[h, tc] = gmin[h:h + 1, cs]

        acca_ref[...] = jnp.zeros(acca_ref.shape, F32)
        accb_ref[...] = jnp.zeros(accb_ref.shape, F32)
        a0_ref[...] = _dot(u0_ref[0:ec, :], hb)
        a1_ref[...] = _dot(u0_ref[ec:2 * ec, :], hb)

    mxu_cols = 2 * LANES
    n_piece = tt // mxu_cols

    def gate_gelu(a_ref, p_ref, chunk, tc):
        i0 = pl.multiple_of(chunk * rows_per_chunk, SUBLANES)
        cs = slice(tc * LANES, (tc + 1) * LANES)
        e18 = [e1_ref[h, tc, pl.ds(i0, rows_per_chunk), :] for h in range(PEER_HEADS)]
        gm = [gmin_ref[h, tc] for h in range(PEER_HEADS)]
        for ii in range(rows_per_chunk):
            rows = slice(ii * N_KEYS, (ii + 1) * N_KEYS)
            w = jnp.zeros((N_KEYS, LANES), F32)
            for h in range(PEER_HEADS):
                gate = e18[h][ii:ii + 1, :] * e2_ref[h, tc]
                w = w + jnp.where(gate >= gm[h], gate, 0.0)
            av = a_ref[rows, cs]
            gelu = 0.5 * av * (1.0 + lax.erf(av * np.float32(math.sqrt(0.5))))
            p_ref[rows, cs] = (w * gelu).astype(BF16)

    for par, (a_ref, p_ref, acc_ref) in enumerate(((a0_ref, p0_ref, acca_ref),
                                                   (a1_ref, p1_ref, accb_ref))):
        es = slice(par * ec, (par + 1) * ec)
        for piece in range(n_piece):
            cs = slice(piece * mxu_cols, (piece + 1) * mxu_cols)
            for tc in range(piece * (mxu_cols // LANES), (piece + 1) * (mxu_cols // LANES)):
                gate_gelu(a_ref, p_ref, 2 * g + par, tc)
            acc_ref[:, cs] += _dot(vt_ref[:, es], p_ref[:, cs])
            a_ref[:, cs] = _dot(un_ref[es, :], hb_ref[:, cs])

    @pl.when(g == last)
    def _finish():
        o_ref[...] = x_ref[...] + (acca_ref[...] + accb_ref[...]).T


def _peer(x, g, wq_t, keys, u, vt, tile, ec):
    T, D = x.shape
    E = u.shape[0]
    ncand = len(_CAND_PAIRS)
    nchunk = E // ec
    assert E == N_KEYS * N_KEYS and ec == SUBLANES * N_KEYS and tile % LANES == 0
    assert nchunk % 2 == 0
    ntc = tile // LANES
    nstep = nchunk // 2
    return pl.pallas_call(
        _peer_kernel,
        grid=(T // tile, nstep),
        in_specs=[pl.BlockSpec((tile, D), lambda i, s: (i, 0)),
                  _const_spec((1, D)), _const_spec(wq_t.shape), _const_spec(keys.shape),
                  pl.BlockSpec((2 * ec, D), lambda i, s: (0, 0), pipeline_mode=pl.Buffered(1)),
                  pl.BlockSpec((2 * ec, D), lambda i, s: (jnp.minimum(s + 1, nstep - 1), 0)),
                  pl.BlockSpec((D, 2 * ec), lambda i, s: (0, s))],
        out_specs=pl.BlockSpec((tile, D), lambda i, s: (i, 0)),
        out_shape=jax.ShapeDtypeStruct((T, D), F32),
        scratch_shapes=[pltpu.VMEM((D, tile), BF16),
                        pltpu.VMEM((PEER_HEADS, N_KEYS, tile), F32),
                        pltpu.VMEM((PEER_HEADS, N_KEYS, tile), F32),
                        pltpu.VMEM((PEER_HEADS, ntc, N_KEYS, LANES), F32),
                        pltpu.VMEM((PEER_HEADS, ntc, N_KEYS, LANES), F32),
                        pltpu.VMEM((PEER_HEADS, ntc, 1, LANES), F32),
                        pltpu.VMEM((N_KEYS, tile), F32),
                        pltpu.VMEM((2, PEER_TOPK + 1, PEER_HEADS, tile), F32),
                        pltpu.VMEM((ncand, PEER_HEADS, tile), F32),
                        pltpu.VMEM((ec, tile), F32), pltpu.VMEM((ec, tile), F32),
                        pltpu.VMEM((ec, tile), BF16), pltpu.VMEM((ec, tile), BF16),
                        pltpu.VMEM((D, tile), F32), pltpu.VMEM((D, tile), F32)],
        compiler_params=_params("arbitrary", "arbitrary"),
        name="peer",
    )(x, g, wq_t, keys, u, u, vt)


def _rope_tables(pos_ref, inv_ref):
    ang = pos_ref[...].astype(F32) * inv_ref[...]
    lane = lax.broadcasted_iota(jnp.int32, ang.shape, 1) % HEAD_DIM
    cos = jnp.cos(ang)
    sin = jnp.sin(ang)
    half = ROT_DIM // 2
    sin_lo = jnp.where(lane < half, -sin, 0.0)
    sin_hi = jnp.where((lane >= half) & (lane < ROT_DIM), sin, 0.0)
    return cos, sin_lo, sin_hi


def _rope_apply(t, cos, sin_lo, sin_hi):
    half = ROT_DIM // 2
    up = pltpu.roll(t, LANES - half, 1)
    dn = pltpu.roll(t, half, 1)
    return t * cos + up * sin_lo + dn * sin_hi


def _ple_core(x_ref, p_ref, g_ref, proj_ref, gw_ref):
    x = x_ref[...]
    hn = _rms(x, g_ref[...], NORM_EPS).astype(BF16)
    gate = jax.nn.sigmoid(_dot(hn, gw_ref[...]))
    return x + _dot(p_ref[...].astype(BF16), proj_ref[...]) * gate


def _ple_kv_kernel(x_ref, p_ref, g_ref, proj_ref, gw_ref, kvg_ref, kvw_ref, kvb_ref, pos_ref,
                   inv_ref, o_ref, k_ref, v_ref):
    x2 = _ple_core(x_ref, p_ref, g_ref, proj_ref, gw_ref)
    o_ref[...] = x2
    kv = _dot(_rms(x2, kvg_ref[...], NORM_EPS).astype(BF16), kvw_ref[...]) + kvb_ref[...]
    kvd = k_ref.shape[1]
    cos, sin_lo, sin_hi = _rope_tables(pos_ref, inv_ref)
    k_ref[...] = _rope_apply(kv[:, :kvd], cos, sin_lo, sin_hi).astype(k_ref.dtype)
    v_ref[...] = kv[:, kvd:].astype(v_ref.dtype)


def _ple_final_kernel(x_ref, p_ref, g_ref, proj_ref, gw_ref, fg_ref, o_ref):
    x2 = _ple_core(x_ref, p_ref, g_ref, proj_ref, gw_ref)
    o_ref[...] = _rms(x2, fg_ref[...], NORM_EPS)


def _ple_kv(x, p, g, proj, gw, kvg, kvw, kvb, pos, inv, tile):
    T, D = x.shape
    P = p.shape[1]
    kvd = kvw.shape[1] // 2
    tok = lambda w: pl.BlockSpec((tile, w), lambda i: (i, 0))
    return pl.pallas_call(
        _ple_kv_kernel,
        grid=(T // tile,),
        in_specs=[tok(D), tok(P), _const_spec((1, D)), _const_spec(proj.shape),
                  _const_spec(gw.shape), _const_spec((1, D)), _const_spec(kvw.shape),
                  _const_spec(kvb.shape), tok(1), _const_spec(inv.shape)],
        out_specs=[tok(D), tok(kvd), tok(kvd)],
        out_shape=[jax.ShapeDtypeStruct((T, D), F32), jax.ShapeDtypeStruct((T, kvd), BF16),
                   jax.ShapeDtypeStruct((T, kvd), BF16)],
        compiler_params=_params("arbitrary"),
        name="ple_kv",
    )(x, p, g, proj, gw, kvg, kvw, kvb, pos, inv)


def _ple_final(x, p, g, proj, gw, fg, tile):
    T, D = x.shape
    P = p.shape[1]
    tok = lambda w: pl.BlockSpec((tile, w), lambda i: (i, 0))
    return pl.pallas_call(
        _ple_final_kernel,
        grid=(T // tile,),
        in_specs=[tok(D), tok(P), _const_spec((1, D)), _const_spec(proj.shape),
                  _const_spec(gw.shape), _const_spec((1, D))],
        out_specs=tok(D),
        out_shape=jax.ShapeDtypeStruct((T, D), F32),
        compiler_params=_params("arbitrary"),
        name="ple_final",
    )(x, p, g, proj, gw, fg)


def _attn_kernel(x_ref, g_ref, qw_ref, qb_ref, sink_ref, ow_ref, ob_ref, kc_ref, kp_ref, vc_ref,
                 vp_ref, pos_ref, inv_ref, o_ref, q_ref, att_ref, *, n_q_heads):
    tile = x_ref.shape[0]
    W = WINDOW
    nblk = tile // W
    q_per_kv = n_q_heads // N_KV_HEADS
    scale = HEAD_DIM ** -0.5
    first = pl.program_id(1) == 0

    h = _rms(x_ref[...], g_ref[...], NORM_EPS).astype(BF16)
    q = _dot(h, qw_ref[...]) + qb_ref[...]
    cos, sin_lo, sin_hi = _rope_tables(pos_ref, inv_ref)
    for lg in range(q.shape[1] // LANES):
        cs = slice(lg * LANES, (lg + 1) * LANES)
        q_ref[:, cs] = _rope_apply(q[:, cs], cos, sin_lo, sin_hi).astype(BF16)

    qi = lax.broadcasted_iota(jnp.int32, (W, 2 * W), 0)
    kj = lax.broadcasted_iota(jnp.int32, (W, 2 * W), 1)
    first_lo = jnp.where(first, W, 0)
    band = jnp.where((kj > qi) & (kj <= qi + W), 0.0, -jnp.inf)
    band_first = jnp.where(kj >= first_lo, band, -jnp.inf)
    band = jnp.concatenate([band] * q_per_kv, axis=0)
    band_first = jnp.concatenate([band_first] * q_per_kv, axis=0)

    for n in range(nblk):
        rows = slice(n * W, (n + 1) * W)
        if n == 0:
            kprev, vprev = kp_ref[...], vp_ref[...]
        else:
            kprev, vprev = kc_ref[(n - 1) * W:n * W, :], vc_ref[(n - 1) * W:n * W, :]
        kblk = jnp.concatenate([kprev, kc_ref[rows, :]], axis=0)
        vblk = jnp.concatenate([vprev, vc_ref[rows, :]], axis=0)
        bias = band_first if n == 0 else band
        for g in range(N_KV_HEADS):
            ks = kblk[:, g * HEAD_DIM:(g + 1) * HEAD_DIM]
            vs = vblk[:, g * HEAD_DIM:(g + 1) * HEAD_DIM]
            qs = jnp.concatenate(
                [q_ref[rows, (g * q_per_kv + r) * HEAD_DIM:(g * q_per_kv + r + 1) * HEAD_DIM]
                 for r in range(q_per_kv)], axis=0)
            s = _dot(qs, ks, _NT) * scale + bias
            sink = jnp.concatenate(
                [jnp.broadcast_to(sink_ref[:, g * q_per_kv + r:g * q_per_kv + r + 1], (W, 1))
                 for r in range(q_per_kv)], axis=0)
            m = jnp.maximum(jnp.max(s, axis=-1, keepdims=True), sink)
            e = jnp.exp(s - m)
            denom = jnp.sum(e, axis=-1, keepdims=True) + jnp.exp(sink - m)
            pr = (e / denom).astype(BF16)
            o = _dot(pr, vs)
            for r in range(q_per_kv):
                hh = g * q_per_kv + r
                att_ref[rows, hh * HEAD_DIM:(hh + 1) * HEAD_DIM] = o[r * W:(r + 1) * W, :].astype(BF16)

    o_ref[...] = x_ref[...] + _dot(att_ref[...], ow_ref[...]) + ob_ref[...]


def _attn(x, g, qw, qb, sinks, ow, ob, k, v, pos, inv, batch, tile):
    T, D = x.shape
    kvd = k.shape[1]
    nq = qw.shape[1] // HEAD_DIM
    nt = T // batch // tile
    bpt = tile // WINDOW
    row = lambda b, i: (b * nt + i, 0)
    prev = lambda b, i: (jnp.maximum((b * nt + i) * bpt - 1, 0), 0)
    kern = functools.partial(_attn_kernel, n_q_heads=nq)
    return pl.pallas_call(
        kern,
        grid=(batch, nt),
        in_specs=[pl.BlockSpec((tile, D), row), _const_spec((1, D)), _const_spec(qw.shape),
                  _const_spec(qb.shape), _const_spec(sinks.shape), _const_spec(ow.shape),
                  _const_spec(ob.shape),
                  pl.BlockSpec((tile, kvd), row), pl.BlockSpec((WINDOW, kvd), prev),
                  pl.BlockSpec((tile, kvd), row), pl.BlockSpec((WINDOW, kvd), prev),
                  pl.BlockSpec((tile, 1), row), _const_spec(inv.shape)],
        out_specs=pl.BlockSpec((tile, D), row),
        out_shape=jax.ShapeDtypeStruct((T, D), F32),
        scratch_shapes=[pltpu.VMEM((tile, qw.shape[1]), BF16),
                        pltpu.VMEM((tile, qw.shape[1]), BF16)],
        compiler_params=_params("arbitrary", "arbitrary"),
        name="swa_attn",
    )(x, g, qw, qb, sinks, ow, ob, k, k, v, v, pos, inv)


def _row(v):
    return v.reshape(1, -1)


def kernel(x, p, positions, ssm_norm, ssm_in_w, ssm_conv_w, ssm_conv_b, ssm_dt_bias, ssm_A_log, ssm_D, ssm_gate_norm, ssm_out_w, kv_norm, kv_w, kv_b, attn_norm, q_w, q_b, sinks, o_w, o_b, peer_norm, peer_q_w, peer_sub_keys, peer_u, peer_v, ple_norm, ple_proj, ple_gate_w, final_norm):
    B, S, D = x.shape
    T = B * S
    depth = p.shape[0]
    n_a = ssm_norm.shape[0]
    H = ssm_D.shape[1]
    d_inner = H * SSM_HEADDIM
    conv_dim = ssm_conv_w.shape[2]

    xt = x.reshape(T, D)
    pos = positions.reshape(T, 1)
    lane = np.arange(LANES) % HEAD_DIM
    inv = np.where(lane < ROT_DIM,
                   ROPE_THETA ** (-(2.0 * (lane % (ROT_DIM // 2))) / ROT_DIM), 0.0)
    inv = jnp.asarray(inv.reshape(1, LANES), F32)
    expand = jnp.asarray(np.repeat(np.eye(H, dtype=np.float32), SSM_HEADDIM, axis=1))
    tril = jnp.asarray(np.tril(np.ones((SSD_CHUNK, SSD_CHUNK), np.float32)))

    k_sh = v_sh = None
    for i in range(depth):
        if i < n_a:
            w = ssm_in_w[i].astype(BF16)
            z, xbc, dtr = _inproj(xt, _row(ssm_norm[i]), w[:, :d_inner],
                                  w[:, d_inner:d_inner + conv_dim], w[:, d_inner + conv_dim:],
                                  tile=256)
            y = _ssd(z, xbc, dtr, ssm_conv_w[i], _row(ssm_conv_b[i]), _row(ssm_dt_bias[i]),
                     _row(ssm_A_log[i]), _row(jnp.repeat(ssm_D[i], SSM_HEADDIM)),
                     _row(ssm_gate_norm[i]), expand, tril, batch=B)
            xt = _outproj(xt, y, ssm_out_w[i].astype(BF16), tile=512)
        else:
            j = i - n_a
            xt = _attn(xt, _row(attn_norm[j]), q_w[j].astype(BF16), _row(q_b[j]), _row(sinks[j]),
                       o_w[j].astype(BF16), _row(o_b[j]), k_sh, v_sh, pos, inv, batch=B, tile=512)
        xt = _peer(xt, _row(peer_norm[i]), peer_q_w[i].T.astype(BF16),
                   peer_sub_keys[i].astype(BF16), peer_u[i].astype(BF16),
                   peer_v[i].T.astype(BF16), tile=512, ec=1024)
        if i == n_a - 1:
            xt, k_sh, v_sh = _ple_kv(xt, p[i].reshape(T, -1), _row(ple_norm[i]),
                                     ple_proj[i].astype(BF16), ple_gate_w[i].astype(BF16),
                                     _row(kv_norm), kv_w.astype(BF16), _row(kv_b), pos, inv,
                                     tile=512)
        elif i == depth - 1:
            xt = _ple_final(xt, p[i].reshape(T, -1), _row(ple_norm[i]), ple_proj[i].astype(BF16),
                            ple_gate_w[i].astype(BF16), _row(final_norm), tile=512)
        else:
            raise NotImplementedError("PLE without K/V or final norm")
    return xt.reshape(B, S, D)
```

```python
import functools
import math

import jax
import jax.numpy as jnp
import numpy as np
from jax import lax
from jax.experimental import pallas as pl
from jax.experimental.pallas import tpu as pltpu

F32 = jnp.float32
BF16 = jnp.bfloat16

NORM_EPS = 1e-6
GATED_NORM_EPS = 1e-5
SSM_HEADDIM = 64
SSM_GROUPS = 8
SSM_STATE = 128
CONV_K = 4
SSD_CHUNK = 128
HEAD_DIM = 64
N_KV_HEADS = 2
WINDOW = 128
ROT_DIM = HEAD_DIM // 4
ROPE_THETA = 500000.0
PEER_HEADS = 8
N_KEYS = 128
PEER_TOPK = 16

LANES = 128
SUBLANES = 8
VMEM_LIMIT = 56 * 1024 * 1024

_CAND_PAIRS = [(r1, r2) for r1 in range(PEER_TOPK + 1) for r2 in range(PEER_TOPK + 1)
               if (r1 + 1) * (r2 + 1) <= PEER_TOPK + 1]


def _sort_network(n):
    pairs = []
    p = 1
    while p < n:
        k = p
        while k >= 1:
            for j in range(k % p, n - k, 2 * k):
                for i in range(min(k, n - j - k)):
                    if (i + j) // (2 * p) == (i + j + k) // (2 * p):
                        pairs.append((i + j, i + j + k))
            k //= 2
        p *= 2
    return pairs


_SORT_PAIRS = _sort_network(N_KEYS // SUBLANES)


def _params(*sem):
    return pltpu.CompilerParams(dimension_semantics=sem, vmem_limit_bytes=VMEM_LIMIT)


def _const_spec(shape):
    nd = len(shape)
    return pl.BlockSpec(shape, lambda *_: (0,) * nd, pipeline_mode=pl.Buffered(1))


def _rms(x, g, eps):
    return x * lax.rsqrt(jnp.mean(x * x, axis=-1, keepdims=True) + eps) * g


def _dot(a, b, dims=None, precision=None):
    if dims is None:
        dims = (((a.ndim - 1,), (0,)), ((), ()))
    return lax.dot_general(a, b, dims, precision=precision, preferred_element_type=F32)


_NT = (((1,), (1,)), ((), ()))
_TN = (((0,), (0,)), ((), ()))
_HI = lax.Precision.HIGHEST


def _inproj_kernel(x_ref, g_ref, wz_ref, wx_ref, wd_ref, z_ref, xbc_ref, dt_ref):
    h = _rms(x_ref[...], g_ref[...], NORM_EPS).astype(BF16)
    z_ref[...] = _dot(h, wz_ref[...])
    xbc_ref[...] = _dot(h, wx_ref[...])
    dt_ref[...] = _dot(h, wd_ref[...])


def _inproj(x, g, wz, wx, wd, tile):
    T, D = x.shape
    nz, nx, nd = wz.shape[1], wx.shape[1], wd.shape[1]
    return pl.pallas_call(
        _inproj_kernel,
        grid=(T // tile,),
        in_specs=[pl.BlockSpec((tile, D), lambda i: (i, 0)),
                  _const_spec((1, D)), _const_spec(wz.shape), _const_spec(wx.shape),
                  _const_spec(wd.shape)],
        out_specs=[pl.BlockSpec((tile, nz), lambda i: (i, 0)),
                   pl.BlockSpec((tile, nx), lambda i: (i, 0)),
                   pl.BlockSpec((tile, nd), lambda i: (i, 0))],
        out_shape=[jax.ShapeDtypeStruct((T, nz), F32), jax.ShapeDtypeStruct((T, nx), F32),
                   jax.ShapeDtypeStruct((T, nd), F32)],
        compiler_params=_params("arbitrary"),
        name="ssm_inproj",
    )(x, g, wz, wx, wd)


def _ssd_kernel(z_ref, xbc_ref, dtr_ref, cw_ref, cb_ref, dtb_ref, alog_ref, dexp_ref, gn_ref,
                expand_ref, tril_ref, y_ref, xb_ref, st_ref, *, d_inner, n_heads):
    L = SSD_CHUNK
    gw = d_inner // SSM_GROUPS
    hpg = n_heads // SSM_GROUPS
    tail = SUBLANES

    @pl.when(pl.program_id(1) == 0)
    def _():
        xb_ref[0:tail, :] = jnp.zeros((tail, xb_ref.shape[1]), F32)
        st_ref[...] = jnp.zeros(st_ref.shape, F32)

    xb_ref[tail:tail + L, :] = xbc_ref[...]

    def conv_silu(lo, width):
        acc = cb_ref[:, lo:lo + width]
        for k in range(CONV_K):
            off = tail - (CONV_K - 1) + k
            acc = acc + xb_ref[off:off + L, lo:lo + width] * cw_ref[k:k + 1, lo:lo + width]
        return acc * jax.nn.sigmoid(acc)

    dt_in = dtr_ref[...] + dtb_ref[...]
    dt = jnp.maximum(dt_in, 0.0) + jnp.log1p(jnp.exp(-jnp.abs(dt_in)))
    a = dt * (-jnp.exp(alog_ref[...]))
    tril = tril_ref[...]
    expand = expand_ref[...]
    a_cs = _dot(tril, a, precision=_HI)
    a_cs_t = a_cs.T
    a_cs_x = _dot(tril, _dot(a, expand, precision=_HI), precision=_HI)
    dt_x = _dot(dt, expand, precision=_HI)
    a_last_x = a_cs_x[L - 1:L, :]
    causal = tril > 0.5

    for g in range(SSM_GROUPS):
        lo = g * gw
        xs = conv_silu(lo, gw)
        bm = conv_silu(d_inner + g * SSM_STATE, SSM_STATE).astype(BF16)
        cm = conv_silu(d_inner + SSM_GROUPS * SSM_STATE + g * SSM_STATE, SSM_STATE).astype(BF16)
        xdt = xs * dt_x[:, lo:lo + gw]
        cb = _dot(cm, bm, _NT)
        yd = []
        for r in range(hpg):
            hh = g * hpg + r
            seg = a_cs[:, hh:hh + 1] - a_cs_t[hh:hh + 1, :]
            lmat = jnp.exp(jnp.where(causal, seg, -jnp.inf))
            m = (cb * lmat).astype(BF16)
            yd.append(_dot(m, xdt[:, r * SSM_HEADDIM:(r + 1) * SSM_HEADDIM].astype(BF16)))
        y = jnp.concatenate(yd, axis=1)
        acx = a_cs_x[:, lo:lo + gw]
        alx = a_last_x[:, lo:lo + gw]
        prev = st_ref[g]
        y = y + _dot(cm, prev.astype(BF16)) * jnp.exp(acx)
        xd = (xdt * jnp.exp(alx - acx)).astype(BF16)
        st_ref[g] = prev * jnp.exp(alx) + _dot(bm, xd, _TN)
        y = y + xs * dexp_ref[:, lo:lo + gw]
        zg = z_ref[:, lo:lo + gw]
        y = y * (zg * jax.nn.sigmoid(zg))
        y = y * lax.rsqrt(jnp.mean(y * y, axis=-1, keepdims=True) + GATED_NORM_EPS)
        y_ref[:, lo:lo + gw] = (y * gn_ref[:, lo:lo + gw]).astype(y_ref.dtype)

    xb_ref[0:tail, :] = xb_ref[L:L + tail, :]


def _ssd(z, xbc, dtr, cw, cb, dtb, alog, dexp, gn, expand, tril, batch):
    T, d_inner = z.shape
    conv_dim = xbc.shape[1]
    H = dtr.shape[1]
    L = SSD_CHUNK
    nc = T // batch // L
    row = lambda b, c: (b * nc + c, 0)
    kern = functools.partial(_ssd_kernel, d_inner=d_inner, n_heads=H)
    return pl.pallas_call(
        kern,
        grid=(batch, nc),
        in_specs=[pl.BlockSpec((L, d_inner), row), pl.BlockSpec((L, conv_dim), row),
                  pl.BlockSpec((L, H), row),
                  _const_spec(cw.shape), _const_spec(cb.shape), _const_spec(dtb.shape),
                  _const_spec(alog.shape), _const_spec(dexp.shape), _const_spec(gn.shape),
                  _const_spec(expand.shape), _const_spec(tril.shape)],
        out_specs=pl.BlockSpec((L, d_inner), row),
        out_shape=jax.ShapeDtypeStruct((T, d_inner), BF16),
        scratch_shapes=[pltpu.VMEM((L + SUBLANES, conv_dim), F32),
                        pltpu.VMEM((SSM_GROUPS, SSM_STATE, d_inner // SSM_GROUPS), F32)],
        compiler_params=_params("arbitrary", "arbitrary"),
        name="ssd_scan",
    )(z, xbc, dtr, cw, cb, dtb, alog, dexp, gn, expand, tril)


def _outproj_kernel(x_ref, y_ref, w_ref, o_ref):
    o_ref[...] = x_ref[...] + _dot(y_ref[...], w_ref[...])


def _outproj(x, y, w, tile):
    T, D = x.shape
    K = y.shape[1]
    return pl.pallas_call(
        _outproj_kernel,
        grid=(T // tile,),
        in_specs=[pl.BlockSpec((tile, D), lambda i: (i, 0)),
                  pl.BlockSpec((tile, K), lambda i: (i, 0)), _const_spec(w.shape)],
        out_specs=pl.BlockSpec((tile, D), lambda i: (i, 0)),
        out_shape=jax.ShapeDtypeStruct((T, D), F32),
        compiler_params=_params("arbitrary"),
        name="ssm_outproj",
    )(x, y, w)


def _peer_kernel(x_ref, g_ref, wq_ref, keys_ref, u0_ref, un_ref, vt_ref, o_ref,
                 hb_ref, e1_ref, e2_ref, gmin_ref, top_ref,
                 a0_ref, a1_ref, p0_ref, p1_ref, acca_ref, accb_ref):
    g = pl.program_id(1)
    last = pl.num_programs(1) - 1
    tt = x_ref.shape[0]
    ec = un_ref.shape[0] // 2
    rows_per_chunk = ec // N_KEYS
    mxu_cols = 2 * LANES
    n_piece = tt // mxu_cols
    half = N_KEYS
    ntop = PEER_TOPK + 1
    neg_inf = -jnp.inf

    @pl.when(g == 0)
    def _route():
        hb = _rms(x_ref[...], g_ref[...], NORM_EPS).T.astype(BF16)
        hb_ref[...] = hb
        for h in range(PEER_HEADS):
            q = _dot(wq_ref[h * 2 * half:(h + 1) * 2 * half, :], hb).astype(BF16)
            a0_ref[h * N_KEYS:(h + 1) * N_KEYS, :] = _dot(keys_ref[0], q[0:half, :])
            a1_ref[h * N_KEYS:(h + 1) * N_KEYS, :] = _dot(keys_ref[1], q[half:2 * half, :])

        def top_values(s_ref, h, cs):
            v = [s_ref[h * N_KEYS + r * SUBLANES:h * N_KEYS + (r + 1) * SUBLANES, cs]
                 for r in range(N_KEYS // SUBLANES)]
            for i, j in _SORT_PAIRS:
                v[i], v[j] = jnp.maximum(v[i], v[j]), jnp.minimum(v[i], v[j])
            depth = len(v)
            out = []
            for k in range(ntop):
                m = v[0]
                for shift in (4, 2, 1):
                    m = jnp.maximum(m, pltpu.roll(m, shift, 0))
                out.append(m)
                hit = v[0] == m
                for r in range(min(depth, ntop - 1 - k)):
                    v[r] = jnp.where(hit, v[r + 1] if r + 1 < depth else neg_inf, v[r])
            return out

        for tc in range(tt // LANES):
            cs = slice(tc * LANES, (tc + 1) * LANES)
            for c, s_ref in enumerate((a0_ref, a1_ref)):
                for h in range(PEER_HEADS):
                    for r, m in enumerate(top_values(s_ref, h, cs)):
                        top_ref[c, r, h:h + 1, cs] = m[0:1, :]

            a = [top_ref[0, r, :, cs] for r in range(ntop)]
            b = [top_ref[1, r, :, cs] for r in range(ntop)]
            cand = [a[r1] + b[r2] for r1, r2 in _CAND_PAIRS]
            a0, b0 = a[0], b[0]
            m0 = a0 + b0
            z = jnp.zeros(m0.shape, F32)
            for k in range(ntop):
                m = functools.reduce(jnp.maximum, cand)
                if k < PEER_TOPK:
                    z = z + jnp.exp(m - m0)
                if k == PEER_TOPK - 1:
                    v16 = m
                if k < ntop - 1:
                    cand = [jnp.where(cv == m, neg_inf, cv) for cv in cand]
            v17 = m
            zinv = 1.0 / z
            gmin = jnp.exp(0.5 * (v16 + v17) - m0) * zinv
            for h in range(PEER_HEADS):
                hk = slice(h * N_KEYS, (h + 1) * N_KEYS)
                e1_ref[h, tc] = jnp.exp(a0_ref[hk, cs] - a0[h:h + 1, :])
                e2_ref[h, tc] = jnp.exp(a1_ref[hk, cs] - b0[h:h + 1, :]) * zinv[h:h + 1, :]
                gmin_ref[h, tc] = gmin[h:h + 1, :]

            if (tc + 1) % (mxu_cols // LANES) == 0:
                ps = slice((tc + 1) * LANES - mxu_cols, (tc + 1) * LANES)
                a0_ref[:, ps] = _dot(u0_ref[0:ec, :], hb_ref[:, ps])
                a1_ref[:, ps] = _dot(u0_ref[ec:2 * ec, :], hb_ref[:, ps])

        acca_ref[...] = jnp.zeros(acca_ref.shape, F32)
        accb_ref[...] = jnp.zeros(accb_ref.shape, F32)

    def gate_gelu(a_ref, p_ref, chunk, tc):
        i0 = pl.multiple_of(chunk * rows_per_chunk, SUBLANES)
        cs = slice(tc * LANES, (tc + 1) * LANES)
        e18 = [e1_ref[h, tc, pl.ds(i0, rows_per_chunk), :] for h in range(PEER_HEADS)]
        gm = [gmin_ref[h, tc] for h in range(PEER_HEADS)]
        for ii in range(rows_per_chunk):
            rows = slice(ii * N_KEYS, (ii + 1) * N_KEYS)
            w = jnp.zeros((N_KEYS, LANES), F32)
            for h in range(PEER_HEADS):
                gate = e18[h][ii:ii + 1, :] * e2_ref[h, tc]
                w = w + jnp.where(gate >= gm[h], gate, 0.0)
            av = a_ref[rows, cs]
            gelu = 0.5 * av * (1.0 + lax.erf(av * np.float32(math.sqrt(0.5))))
            p_ref[rows, cs] = (w * gelu).astype(BF16)

    for par, (a_ref, p_ref, acc_ref) in enumerate(((a0_ref, p0_ref, acca_ref),
                                                   (a1_ref, p1_ref, accb_ref))):
        es = slice(par * ec, (par + 1) * ec)
        for piece in range(n_piece):
            cs = slice(piece * mxu_cols, (piece + 1) * mxu_cols)
            for tc in range(piece * (mxu_cols // LANES), (piece + 1) * (mxu_cols // LANES)):
                gate_gelu(a_ref, p_ref, 2 * g + par, tc)
            acc_ref[:, cs] += _dot(vt_ref[:, es], p_ref[:, cs])
            a_ref[:, cs] = _dot(un_ref[es, :], hb_ref[:, cs])

    @pl.when(g == last)
    def _finish():
        o_ref[...] = x_ref[...] + (acca_ref[...] + accb_ref[...]).T


def _peer(x, g, wq_t, keys, u, vt, tile, ec):
    T, D = x.shape
    E = u.shape[0]
    nchunk = E // ec
    assert E == N_KEYS * N_KEYS and ec == SUBLANES * N_KEYS and tile % LANES == 0
    assert ec == PEER_HEADS * N_KEYS
    assert nchunk % 2 == 0
    ntc = tile // LANES
    nstep = nchunk // 2
    return pl.pallas_call(
        _peer_kernel,
        grid=(T // tile, nstep),
        in_specs=[pl.BlockSpec((tile, D), lambda i, s: (i, 0)),
                  _const_spec((1, D)), _const_spec(wq_t.shape), _const_spec(keys.shape),
                  pl.BlockSpec((2 * ec, D), lambda i, s: (0, 0), pipeline_mode=pl.Buffered(1)),
                  pl.BlockSpec((2 * ec, D), lambda i, s: (jnp.minimum(s + 1, nstep - 1), 0)),
                  pl.BlockSpec((D, 2 * ec), lambda i, s: (0, s))],
        out_specs=pl.BlockSpec((tile, D), lambda i, s: (i, 0)),
        out_shape=jax.ShapeDtypeStruct((T, D), F32),
        scratch_shapes=[pltpu.VMEM((D, tile), BF16),
                        pltpu.VMEM((PEER_HEADS, ntc, N_KEYS, LANES), F32),
                        pltpu.VMEM((PEER_HEADS, ntc, N_KEYS, LANES), F32),
                        pltpu.VMEM((PEER_HEADS, ntc, 1, LANES), F32),
                        pltpu.VMEM((2, PEER_TOPK + 1, PEER_HEADS, tile), F32),
                        pltpu.VMEM((ec, tile), F32), pltpu.VMEM((ec, tile), F32),
                        pltpu.VMEM((ec, tile), BF16), pltpu.VMEM((ec, tile), BF16),
                        pltpu.VMEM((D, tile), F32), pltpu.VMEM((D, tile), F32)],
        compiler_params=_params("arbitrary", "arbitrary"),
        name="peer",
    )(x, g, wq_t, keys, u, u, vt)


def _rope_tables(pos_ref, inv_ref):
    ang = pos_ref[...].astype(F32) * inv_ref[...]
    lane = lax.broadcasted_iota(jnp.int32, ang.shape, 1) % HEAD_DIM
    cos = jnp.cos(ang)
    sin = jnp.sin(ang)
    half = ROT_DIM // 2
    sin_lo = jnp.where(lane < half, -sin, 0.0)
    sin_hi = jnp.where((lane >= half) & (lane < ROT_DIM), sin, 0.0)
    return cos, sin_lo, sin_hi


def _rope_apply(t, cos, sin_lo, sin_hi):
    half = ROT_DIM // 2
    up = pltpu.roll(t, LANES - half, 1)
    dn = pltpu.roll(t, half, 1)
    return t * cos + up * sin_lo + dn * sin_hi


def _ple_core(x_ref, p_ref, g_ref, proj_ref, gw_ref):
    x = x_ref[...]
    hn = _rms(x, g_ref[...], NORM_EPS).astype(BF16)
    gate = jax.nn.sigmoid(_dot(hn, gw_ref[...]))
    return x + _dot(p_ref[...].astype(BF16), proj_ref[...]) * gate


def _ple_kv_kernel(x_ref, p_ref, g_ref, proj_ref, gw_ref, kvg_ref, kvw_ref, kvb_ref, pos_ref,
                   inv_ref, o_ref, k_ref, v_ref):
    x2 = _ple_core(x_ref, p_ref, g_ref, proj_ref, gw_ref)
    o_ref[...] = x2
    kv = _dot(_rms(x2, kvg_ref[...], NORM_EPS).astype(BF16), kvw_ref[...]) + kvb_ref[...]
    kvd = k_ref.shape[1]
    cos, sin_lo, sin_hi = _rope_tables(pos_ref, inv_ref)
    k_ref[...] = _rope_apply(kv[:, :kvd], cos, sin_lo, sin_hi).astype(k_ref.dtype)
    v_ref[...] = kv[:, kvd:].astype(v_ref.dtype)


def _ple_final_kernel(x_ref, p_ref, g_ref, proj_ref, gw_ref, fg_ref, o_ref):
    x2 = _ple_core(x_ref, p_ref, g_ref, proj_ref, gw_ref)
    o_ref[...] = _rms(x2, fg_ref[...], NORM_EPS)


def _ple_kv(x, p, g, proj, gw, kvg, kvw, kvb, pos, inv, tile):
    T, D = x.shape
    P = p.shape[1]
    kvd = kvw.shape[1] // 2
    tok = lambda w: pl.BlockSpec((tile, w), lambda i: (i, 0))
    return pl.pallas_call(
        _ple_kv_kernel,
        grid=(T // tile,),
        in_specs=[tok(D), tok(P), _const_spec((1, D)), _const_spec(proj.shape),
                  _const_spec(gw.shape), _const_spec((1, D)), _const_spec(kvw.shape),
                  _const_spec(kvb.shape), tok(1), _const_spec(inv.shape)],
        out_specs=[tok(D), tok(kvd), tok(kvd)],
        out_shape=[jax.ShapeDtypeStruct((T, D), F32), jax.ShapeDtypeStruct((T, kvd), BF16),
                   jax.ShapeDtypeStruct((T, kvd), BF16)],
        compiler_params=_params("arbitrary"),
        name="ple_kv",
    )(x, p, g, proj, gw, kvg, kvw, kvb, pos, inv)


def _ple_final(x, p, g, proj, gw, fg, tile):
    T, D = x.shape
    P = p.shape[1]
    tok = lambda w: pl.BlockSpec((tile, w), lambda i: (i, 0))
    return pl.pallas_call(
        _ple_final_kernel,
        grid=(T // tile,),
        in_specs=[tok(D), tok(P), _const_spec((1, D)), _const_spec(proj.shape),
                  _const_spec(gw.shape), _const_spec((1, D))],
        out_specs=tok(D),
        out_shape=jax.ShapeDtypeStruct((T, D), F32),
        compiler_params=_params("arbitrary"),
        name="ple_final",
    )(x, p, g, proj, gw, fg)


def _attn_kernel(x_ref, g_ref, qw_ref, qb_ref, sink_ref, ow_ref, ob_ref, kc_ref, kp_ref, vc_ref,
                 vp_ref, pos_ref, inv_ref, o_ref, q_ref, att_ref, *, n_q_heads):
    tile = x_ref.shape[0]
    W = WINDOW
    nblk = tile // W
    q_per_kv = n_q_heads // N_KV_HEADS
    scale = HEAD_DIM ** -0.5
    first = pl.program_id(1) == 0

    h = _rms(x_ref[...], g_ref[...], NORM_EPS).astype(BF16)
    q = _dot(h, qw_ref[...]) + qb_ref[...]
    cos, sin_lo, sin_hi = _rope_tables(pos_ref, inv_ref)
    for lg in range(q.shape[1] // LANES):
        cs = slice(lg * LANES, (lg + 1) * LANES)
        q_ref[:, cs] = _rope_apply(q[:, cs], cos, sin_lo, sin_hi).astype(BF16)

    qi = lax.broadcasted_iota(jnp.int32, (W, 2 * W), 0)
    kj = lax.broadcasted_iota(jnp.int32, (W, 2 * W), 1)
    first_lo = jnp.where(first, W, 0)
    band = jnp.where((kj > qi) & (kj <= qi + W), 0.0, -jnp.inf)
    band_first = jnp.where(kj >= first_lo, band, -jnp.inf)
    band = jnp.concatenate([band] * q_per_kv, axis=0)
    band_first = jnp.concatenate([band_first] * q_per_kv, axis=0)

    for n in range(nblk):
        rows = slice(n * W, (n + 1) * W)
        if n == 0:
            kprev, vprev = kp_ref[...], vp_ref[...]
        else:
            kprev, vprev = kc_ref[(n - 1) * W:n * W, :], vc_ref[(n - 1) * W:n * W, :]
        kblk = jnp.concatenate([kprev, kc_ref[rows, :]], axis=0)
        vblk = jnp.concatenate([vprev, vc_ref[rows, :]], axis=0)
        bias = band_first if n == 0 else band
        for g in range(N_KV_HEADS):
            ks = kblk[:, g * HEAD_DIM:(g + 1) * HEAD_DIM]
            vs = vblk[:, g * HEAD_DIM:(g + 1) * HEAD_DIM]
            qs = jnp.concatenate(
                [q_ref[rows, (g * q_per_kv + r) * HEAD_DIM:(g * q_per_kv + r + 1) * HEAD_DIM]
                 for r in range(q_per_kv)], axis=0)
            s = _dot(qs, ks, _NT) * scale + bias
            sink = jnp.concatenate(
                [jnp.broadcast_to(sink_ref[:, g * q_per_kv + r:g * q_per_kv + r + 1], (W, 1))
                 for r in range(q_per_kv)], axis=0)
            m = jnp.maximum(jnp.max(s, axis=-1, keepdims=True), sink)
            e = jnp.exp(s - m)
            denom = jnp.sum(e, axis=-1, keepdims=True) + jnp.exp(sink - m)
            pr = (e / denom).astype(BF16)
            o = _dot(pr, vs)
            for r in range(q_per_kv):
                hh = g * q_per_kv + r
                att_ref[rows, hh * HEAD_DIM:(hh + 1) * HEAD_DIM] = o[r * W:(r + 1) * W, :].astype(BF16)

    o_ref[...] = x_ref[...] + _dot(att_ref[...], ow_ref[...]) + ob_ref[...]


def _attn(x, g, qw, qb, sinks, ow, ob, k, v, pos, inv, batch, tile):
    T, D = x.shape
    kvd = k.shape[1]
    nq = qw.shape[1] // HEAD_DIM
    nt = T // batch // tile
    bpt = tile // WINDOW
    row = lambda b, i: (b * nt + i, 0)
    prev = lambda b, i: (jnp.maximum((b * nt + i) * bpt - 1, 0), 0)
    kern = functools.partial(_attn_kernel, n_q_heads=nq)
    return pl.pallas_call(
        kern,
        grid=(batch, nt),
        in_specs=[pl.BlockSpec((tile, D), row), _const_spec((1, D)), _const_spec(qw.shape),
                  _const_spec(qb.shape), _const_spec(sinks.shape), _const_spec(ow.shape),
                  _const_spec(ob.shape),
                  pl.BlockSpec((tile, kvd), row), pl.BlockSpec((WINDOW, kvd), prev),
                  pl.BlockSpec((tile, kvd), row), pl.BlockSpec((WINDOW, kvd), prev),
                  pl.BlockSpec((tile, 1), row), _const_spec(inv.shape)],
        out_specs=pl.BlockSpec((tile, D), row),
        out_shape=jax.ShapeDtypeStruct((T, D), F32),
        scratch_shapes=[pltpu.VMEM((tile, qw.shape[1]), BF16),
                        pltpu.VMEM((tile, qw.shape[1]), BF16)],
        compiler_params=_params("arbitrary", "arbitrary"),
        name="swa_attn",
    )(x, g, qw, qb, sinks, ow, ob, k, k, v, v, pos, inv)


def _row(v):
    return v.reshape(1, -1)


def kernel(x, p, positions, ssm_norm, ssm_in_w, ssm_conv_w, ssm_conv_b, ssm_dt_bias, ssm_A_log, ssm_D, ssm_gate_norm, ssm_out_w, kv_norm, kv_w, kv_b, attn_norm, q_w, q_b, sinks, o_w, o_b, peer_norm, peer_q_w, peer_sub_keys, peer_u, peer_v, ple_norm, ple_proj, ple_gate_w, final_norm):
    B, S, D = x.shape
    T = B * S
    depth = p.shape[0]
    n_a = ssm_norm.shape[0]
    H = ssm_D.shape[1]
    d_inner = H * SSM_HEADDIM
    conv_dim = ssm_conv_w.shape[2]

    xt = x.reshape(T, D)
    pos = positions.reshape(T, 1)
    lane = np.arange(LANES) % HEAD_DIM
    inv = np.where(lane < ROT_DIM,
                   ROPE_THETA ** (-(2.0 * (lane % (ROT_DIM // 2))) / ROT_DIM), 0.0)
    inv = jnp.asarray(inv.reshape(1, LANES), F32)
    expand = jnp.asarray(np.repeat(np.eye(H, dtype=np.float32), SSM_HEADDIM, axis=1))
    tril = jnp.asarray(np.tril(np.ones((SSD_CHUNK, SSD_CHUNK), np.float32)))

    k_sh = v_sh = None
    for i in range(depth):
        if i < n_a:
            w = ssm_in_w[i].astype(BF16)
            z, xbc, dtr = _inproj(xt, _row(ssm_norm[i]), w[:, :d_inner],
                                  w[:, d_inner:d_inner + conv_dim], w[:, d_inner + conv_dim:],
                                  tile=256)
            y = _ssd(z, xbc, dtr, ssm_conv_w[i], _row(ssm_conv_b[i]), _row(ssm_dt_bias[i]),
                     _row(ssm_A_log[i]), _row(jnp.repeat(ssm_D[i], SSM_HEADDIM)),
                     _row(ssm_gate_norm[i]), expand, tril, batch=B)
            xt = _outproj(xt, y, ssm_out_w[i].astype(BF16), tile=512)
        else:
            j = i - n_a
            xt = _attn(xt, _row(attn_norm[j]), q_w[j].astype(BF16), _row(q_b[j]), _row(sinks[j]),
                       o_w[j].astype(BF16), _row(o_b[j]), k_sh, v_sh, pos, inv, batch=B, tile=512)
        xt = _peer(xt, _row(peer_norm[i]), peer_q_w[i].T.astype(BF16),
                   peer_sub_keys[i].astype(BF16), peer_u[i].astype(BF16),
                   peer_v[i].T.astype(BF16), tile=512, ec=1024)
        if i == n_a - 1:
            xt, k_sh, v_sh = _ple_kv(xt, p[i].reshape(T, -1), _row(ple_norm[i]),
                                     ple_proj[i].astype(BF16), ple_gate_w[i].astype(BF16),
                                     _row(kv_norm), kv_w.astype(BF16), _row(kv_b), pos, inv,
                                     tile=512)
        elif i == depth - 1:
            xt = _ple_final(xt, p[i].reshape(T, -1), _row(ple_norm[i]), ple_proj[i].astype(BF16),
                            ple_gate_w[i].astype(BF16), _row(final_norm), tile=512)
        else:
            raise NotImplementedError("PLE without K/V or final norm")
    return xt.reshape(B, S, D)
```

```python
import functools
import math

import jax
import jax.numpy as jnp
import numpy as np
from jax import lax
from jax.experimental import pallas as pl
from jax.experimental.pallas import tpu as pltpu

F32 = jnp.float32
BF16 = jnp.bfloat16

NORM_EPS = 1e-6
GATED_NORM_EPS = 1e-5
SSM_HEADDIM = 64
SSM_GROUPS = 8
SSM_STATE = 128
CONV_K = 4
SSD_CHUNK = 128
HEAD_DIM = 64
N_KV_HEADS = 2
WINDOW = 128
ROT_DIM = HEAD_DIM // 4
ROPE_THETA = 500000.0
PEER_HEADS = 8
N_KEYS = 128
PEER_TOPK = 16

LANES = 128
SUBLANES = 8
VMEM_LIMIT = 56 * 1024 * 1024

_CAND_PAIRS = [(r1, r2) for r1 in range(PEER_TOPK + 1) for r2 in range(PEER_TOPK + 1)
               if (r1 + 1) * (r2 + 1) <= PEER_TOPK + 1]


def _sort_network(n):
    pairs = []
    p = 1
    while p < n:
        k = p
        while k >= 1:
            for j in range(k % p, n - k, 2 * k):
                for i in range(min(k, n - j - k)):
                    if (i + j) // (2 * p) == (i + j + k) // (2 * p):
                        pairs.append((i + j, i + j + k))
            k //= 2
        p *= 2
    return pairs


_SORT_PAIRS = _sort_network(N_KEYS // SUBLANES)


def _params(*sem):
    return pltpu.CompilerParams(dimension_semantics=sem, vmem_limit_bytes=VMEM_LIMIT)


def _const_spec(shape):
    nd = len(shape)
    return pl.BlockSpec(shape, lambda *_: (0,) * nd, pipeline_mode=pl.Buffered(1))


def _rms(x, g, eps):
    return x * lax.rsqrt(jnp.mean(x * x, axis=-1, keepdims=True) + eps) * g


def _dot(a, b, dims=None, precision=None):
    if dims is None:
        dims = (((a.ndim - 1,), (0,)), ((), ()))
    return lax.dot_general(a, b, dims, precision=precision, preferred_element_type=F32)


_NT = (((1,), (1,)), ((), ()))
_TN = (((0,), (0,)), ((), ()))
_HI = lax.Precision.HIGHEST


def _inproj_kernel(x_ref, g_ref, wz_ref, wx_ref, wd_ref, z_ref, xbc_ref, dt_ref):
    h = _rms(x_ref[...], g_ref[...], NORM_EPS).astype(BF16)
    z_ref[...] = _dot(h, wz_ref[...])
    xbc_ref[...] = _dot(h, wx_ref[...])
    dt_ref[...] = _dot(h, wd_ref[...])


def _inproj(x, g, wz, wx, wd, tile):
    T, D = x.shape
    nz, nx, nd = wz.shape[1], wx.shape[1], wd.shape[1]
    return pl.pallas_call(
        _inproj_kernel,
        grid=(T // tile,),
        in_specs=[pl.BlockSpec((tile, D), lambda i: (i, 0)),
                  _const_spec((1, D)), _const_spec(wz.shape), _const_spec(wx.shape),
                  _const_spec(wd.shape)],
        out_specs=[pl.BlockSpec((tile, nz), lambda i: (i, 0)),
                   pl.BlockSpec((tile, nx), lambda i: (i, 0)),
                   pl.BlockSpec((tile, nd), lambda i: (i, 0))],
        out_shape=[jax.ShapeDtypeStruct((T, nz), F32), jax.ShapeDtypeStruct((T, nx), F32),
                   jax.ShapeDtypeStruct((T, nd), F32)],
        compiler_params=_params("arbitrary"),
        name="ssm_inproj",
    )(x, g, wz, wx, wd)


def _ssd_kernel(z_ref, xbc_ref, dtr_ref, cw_ref, cb_ref, dtb_ref, alog_ref, dexp_ref, gn_ref,
                expand_ref, tril_ref, y_ref, xb_ref, st_ref, *, d_inner, n_heads):
    L = SSD_CHUNK
    gw = d_inner // SSM_GROUPS
    hpg = n_heads // SSM_GROUPS
    tail = SUBLANES

    @pl.when(pl.program_id(1) == 0)
    def _():
        xb_ref[0:tail, :] = jnp.zeros((tail, xb_ref.shape[1]), F32)
        st_ref[...] = jnp.zeros(st_ref.shape, F32)

    xb_ref[tail:tail + L, :] = xbc_ref[...]

    def conv_silu(lo, width):
        acc = cb_ref[:, lo:lo + width]
        for k in range(CONV_K):
            off = tail - (CONV_K - 1) + k
            acc = acc + xb_ref[off:off + L, lo:lo + width] * cw_ref[k:k + 1, lo:lo + width]
        return acc * jax.nn.sigmoid(acc)

    dt_in = dtr_ref[...] + dtb_ref[...]
    dt = jnp.maximum(dt_in, 0.0) + jnp.log1p(jnp.exp(-jnp.abs(dt_in)))
    a = dt * (-jnp.exp(alog_ref[...]))
    tril = tril_ref[...]
    expand = expand_ref[...]
    a_cs = _dot(tril, a, precision=_HI)
    a_cs_t = a_cs.T
    a_cs_x = _dot(tril, _dot(a, expand, precision=_HI), precision=_HI)
    dt_x = _dot(dt, expand, precision=_HI)
    a_last_x = a_cs_x[L - 1:L, :]
    causal = tril > 0.5

    for g in range(SSM_GROUPS):
        lo = g * gw
        xs = conv_silu(lo, gw)
        bm = conv_silu(d_inner + g * SSM_STATE, SSM_STATE).astype(BF16)
        cm = conv_silu(d_inner + SSM_GROUPS * SSM_STATE + g * SSM_STATE, SSM_STATE).astype(BF16)
        xdt = xs * dt_x[:, lo:lo + gw]
        cb = _dot(cm, bm, _NT)
        yd = []
        for r in range(hpg):
            hh = g * hpg + r
            seg = a_cs[:, hh:hh + 1] - a_cs_t[hh:hh + 1, :]
            lmat = jnp.exp(jnp.where(causal, seg, -jnp.inf))
            m = (cb * lmat).astype(BF16)
            yd.append(_dot(m, xdt[:, r * SSM_HEADDIM:(r + 1) * SSM_HEADDIM].astype(BF16)))
        y = jnp.concatenate(yd, axis=1)
        acx = a_cs_x[:, lo:lo + gw]
        alx = a_last_x[:, lo:lo + gw]
        prev = st_ref[g]
        y = y + _dot(cm, prev.astype(BF16)) * jnp.exp(acx)
        xd = (xdt * jnp.exp(alx - acx)).astype(BF16)
        st_ref[g] = prev * jnp.exp(alx) + _dot(bm, xd, _TN)
        y = y + xs * dexp_ref[:, lo:lo + gw]
        zg = z_ref[:, lo:lo + gw]
        y = y * (zg * jax.nn.sigmoid(zg))
        y = y * lax.rsqrt(jnp.mean(y * y, axis=-1, keepdims=True) + GATED_NORM_EPS)
        y_ref[:, lo:lo + gw] = (y * gn_ref[:, lo:lo + gw]).astype(y_ref.dtype)

    xb_ref[0:tail, :] = xb_ref[L:L + tail, :]


def _ssd(z, xbc, dtr, cw, cb, dtb, alog, dexp, gn, expand, tril, batch):
    T, d_inner = z.shape
    conv_dim = xbc.shape[1]
    H = dtr.shape[1]
    L = SSD_CHUNK
    nc = T // batch // L
    row = lambda b, c: (b * nc + c, 0)
    kern = functools.partial(_ssd_kernel, d_inner=d_inner, n_heads=H)
    return pl.pallas_call(
        kern,
        grid=(batch, nc),
        in_specs=[pl.BlockSpec((L, d_inner), row), pl.BlockSpec((L, conv_dim), row),
                  pl.BlockSpec((L, H), row),
                  _const_spec(cw.shape), _const_spec(cb.shape), _const_spec(dtb.shape),
                  _const_spec(alog.shape), _const_spec(dexp.shape), _const_spec(gn.shape),
                  _const_spec(expand.shape), _const_spec(tril.shape)],
        out_specs=pl.BlockSpec((L, d_inner), row),
        out_shape=jax.ShapeDtypeStruct((T, d_inner), BF16),
        scratch_shapes=[pltpu.VMEM((L + SUBLANES, conv_dim), F32),
                        pltpu.VMEM((SSM_GROUPS, SSM_STATE, d_inner // SSM_GROUPS), F32)],
        compiler_params=_params("arbitrary", "arbitrary"),
        name="ssd_scan",
    )(z, xbc, dtr, cw, cb, dtb, alog, dexp, gn, expand, tril)


def _outproj_kernel(x_ref, y_ref, w_ref, o_ref):
    o_ref[...] = x_ref[...] + _dot(y_ref[...], w_ref[...])


def _outproj(x, y, w, tile):
    T, D = x.shape
    K = y.shape[1]
    return pl.pallas_call(
        _outproj_kernel,
        grid=(T // tile,),
        in_specs=[pl.BlockSpec((tile, D), lambda i: (i, 0)),
                  pl.BlockSpec((tile, K), lambda i: (i, 0)), _const_spec(w.shape)],
        out_specs=pl.BlockSpec((tile, D), lambda i: (i, 0)),
        out_shape=jax.ShapeDtypeStruct((T, D), F32),
        compiler_params=_params("arbitrary"),
        name="ssm_outproj",
    )(x, y, w)


def _peer_kernel(x_ref, g_ref, wq_ref, keys_ref, u0_ref, un_ref, vt_ref, o_ref,
                 hb_ref, e1_ref, e2_ref, gmin_ref, top_ref,
                 a0_ref, a1_ref, p0_ref, p1_ref, acca_ref, accb_ref):
    g = pl.program_id(1)
    last = pl.num_programs(1) - 1
    tt = x_ref.shape[0]
    ec = un_ref.shape[0] // 2
    rows_per_chunk = ec // N_KEYS
    mxu_cols = 2 * LANES
    n_piece = tt // mxu_cols
    half = N_KEYS
    ntop = PEER_TOPK + 1
    neg_inf = -jnp.inf

    @pl.when(g == 0)
    def _route():
        hb = _rms(x_ref[...], g_ref[...], NORM_EPS).T.astype(BF16)
        hb_ref[...] = hb
        for h in range(PEER_HEADS):
            q = _dot(wq_ref[h * 2 * half:(h + 1) * 2 * half, :], hb).astype(BF16)
            a0_ref[h * N_KEYS:(h + 1) * N_KEYS, :] = _dot(keys_ref[0], q[0:half, :])
            a1_ref[h * N_KEYS:(h + 1) * N_KEYS, :] = _dot(keys_ref[1], q[half:2 * half, :])

        def top_values(s_ref, h, cs):
            v = [s_ref[h * N_KEYS + r * SUBLANES:h * N_KEYS + (r + 1) * SUBLANES, cs]
                 for r in range(N_KEYS // SUBLANES)]
            for i, j in _SORT_PAIRS:
                v[i], v[j] = jnp.maximum(v[i], v[j]), jnp.minimum(v[i], v[j])
            depth = len(v)
            out = []
            for k in range(ntop):
                m = v[0]
                for shift in (4, 2, 1):
                    m = jnp.maximum(m, pltpu.roll(m, shift, 0))
                out.append(m)
                hit = v[0] == m
                for r in range(min(depth, ntop - 1 - k)):
                    v[r] = jnp.where(hit, v[r + 1] if r + 1 < depth else neg_inf, v[r])
            return out

        for tc in range(tt // LANES):
            cs = slice(tc * LANES, (tc + 1) * LANES)
            for c, s_ref in enumerate((a0_ref, a1_ref)):
                for h in range(PEER_HEADS):
                    for r, m in enumerate(top_values(s_ref, h, cs)):
                        top_ref[c, r, h:h + 1, cs] = m[0:1, :]

            a = [top_ref[0, r, :, cs] for r in range(ntop)]
            b = [top_ref[1, r, :, cs] for r in range(ntop)]
            cand = [a[r1] + b[r2] for r1, r2 in _CAND_PAIRS]
            a0, b0 = a[0], b[0]
            m0 = a0 + b0
            z = jnp.zeros(m0.shape, F32)
            for k in range(ntop):
                m = functools.reduce(jnp.maximum, cand)
                if k < PEER_TOPK:
                    z = z + jnp.exp(m - m0)
                if k == PEER_TOPK - 1:
                    v16 = m
                if k < ntop - 1:
                    cand = [jnp.where(cv == m, neg_inf, cv) for cv in cand]
            v17 = m
            zinv = 1.0 / z
            gmin = jnp.exp(0.5 * (v16 + v17) - m0) * zinv
            for h in range(PEER_HEADS):
                hk = slice(h * N_KEYS, (h + 1) * N_KEYS)
                e1_ref[h, tc] = jnp.exp(a0_ref[hk, cs] - a0[h:h + 1, :])
                e2_ref[h, tc] = jnp.exp(a1_ref[hk, cs] - b0[h:h + 1, :]) * zinv[h:h + 1, :]
                gmin_ref[h, tc] = gmin[h:h + 1, :]

            if (tc + 1) % (mxu_cols // LANES) == 0:
                ps = slice((tc + 1) * LANES - mxu_cols, (tc + 1) * LANES)
                a0_ref[:, ps] = _dot(u0_ref[0:ec, :], hb_ref[:, ps])
                a1_ref[:, ps] = _dot(u0_ref[ec:2 * ec, :], hb_ref[:, ps])

        acca_ref[...] = jnp.zeros(acca_ref.shape, F32)
        accb_ref[...] = jnp.zeros(accb_ref.shape, F32)

    def gate_gelu(a_ref, p_ref, chunk, tc):
        i0 = pl.multiple_of(chunk * rows_per_chunk, SUBLANES)
        cs = slice(tc * LANES, (tc + 1) * LANES)
        e18 = [e1_ref[h, tc, pl.ds(i0, rows_per_chunk), :] for h in range(PEER_HEADS)]
        gm = [gmin_ref[h, tc] for h in range(PEER_HEADS)]
        for ii in range(rows_per_chunk):
            rows = slice(ii * N_KEYS, (ii + 1) * N_KEYS)
            w = jnp.zeros((N_KEYS, LANES), F32)
            for h in range(PEER_HEADS):
                gate = e18[h][ii:ii + 1, :] * e2_ref[h, tc]
                w = w + jnp.where(gate >= gm[h], gate, 0.0)
            av = a_ref[rows, cs]
            gelu = 0.5 * av * (1.0 + lax.erf(av * np.float32(math.sqrt(0.5))))
            p_ref[rows, cs] = (w * gelu).astype(BF16)

    for par, (a_ref, p_ref, acc_ref) in enumerate(((a0_ref, p0_ref, acca_ref),
                                                   (a1_ref, p1_ref, accb_ref))):
        es = slice(par * ec, (par + 1) * ec)
        for piece in range(n_piece):
            cs = slice(piece * mxu_cols, (piece + 1) * mxu_cols)
            for tc in range(piece * (mxu_cols // LANES), (piece + 1) * (mxu_cols // LANES)):
                gate_gelu(a_ref, p_ref, 2 * g + par, tc)
            acc_ref[:, cs] += _dot(vt_ref[:, es], p_ref[:, cs])
            a_ref[:, cs] = _dot(un_ref[es, :], hb_ref[:, cs])

    @pl.when(g == last)
    def _finish():
        o_ref[...] = x_ref[...] + (acca_ref[...] + accb_ref[...]).T


def _peer(x, g, wq_t, keys, u, vt, tile, ec):
    T, D = x.shape
    E = u.shape[0]
    nchunk = E // ec
    assert E == N_KEYS * N_KEYS and ec == SUBLANES * N_KEYS and tile % LANES == 0
    assert ec == PEER_HEADS * N_KEYS
    assert nchunk % 2 == 0
    ntc = tile // LANES
    nstep = nchunk // 2
    vt = vt.reshape(D, nstep, 2 * ec).transpose(1, 0, 2)
    return pl.pallas_call(
        _peer_kernel,
        grid=(T // tile, nstep),
        in_specs=[pl.BlockSpec((tile, D), lambda i, s: (i, 0)),
                  _const_spec((1, D)), _const_spec(wq_t.shape), _const_spec(keys.shape),
                  pl.BlockSpec((2 * ec, D), lambda i, s: (0, 0), pipeline_mode=pl.Buffered(1)),
                  pl.BlockSpec((2 * ec, D), lambda i, s: (jnp.minimum(s + 1, nstep - 1), 0)),
                  pl.BlockSpec((None, D, 2 * ec), lambda i, s: (s, 0, 0))],
        out_specs=pl.BlockSpec((tile, D), lambda i, s: (i, 0)),
        out_shape=jax.ShapeDtypeStruct((T, D), F32),
        scratch_shapes=[pltpu.VMEM((D, tile), BF16),
                        pltpu.VMEM((PEER_HEADS, ntc, N_KEYS, LANES), F32),
                        pltpu.VMEM((PEER_HEADS, ntc, N_KEYS, LANES), F32),
                        pltpu.VMEM((PEER_HEADS, ntc, 1, LANES), F32),
                        pltpu.VMEM((2, PEER_TOPK + 1, PEER_HEADS, tile), F32),
                        pltpu.VMEM((ec, tile), F32), pltpu.VMEM((ec, tile), F32),
                        pltpu.VMEM((ec, tile), BF16), pltpu.VMEM((ec, tile), BF16),
                        pltpu.VMEM((D, tile), F32), pltpu.VMEM((D, tile), F32)],
        compiler_params=_params("arbitrary", "arbitrary"),
        name="peer",
    )(x, g, wq_t, keys, u, u, vt)


def _rope_tables(pos_ref, inv_ref):
    ang = pos_ref[...].astype(F32) * inv_ref[...]
    lane = lax.broadcasted_iota(jnp.int32, ang.shape, 1) % HEAD_DIM
    cos = jnp.cos(ang)
    sin = jnp.sin(ang)
    half = ROT_DIM // 2
    sin_lo = jnp.where(lane < half, -sin, 0.0)
    sin_hi = jnp.where((lane >= half) & (lane < ROT_DIM), sin, 0.0)
    return cos, sin_lo, sin_hi


def _rope_apply(t, cos, sin_lo, sin_hi):
    half = ROT_DIM // 2
    up = pltpu.roll(t, LANES - half, 1)
    dn = pltpu.roll(t, half, 1)
    return t * cos + up * sin_lo + dn * sin_hi


def _ple_core(x_ref, p_ref, g_ref, proj_ref, gw_ref):
    x = x_ref[...]
    hn = _rms(x, g_ref[...], NORM_EPS).astype(BF16)
    gate = jax.nn.sigmoid(_dot(hn, gw_ref[...]))
    return x + _dot(p_ref[...].astype(BF16), proj_ref[...]) * gate


def _ple_kv_kernel(x_ref, p_ref, g_ref, proj_ref, gw_ref, kvg_ref, kvw_ref, kvb_ref, pos_ref,
                   inv_ref, o_ref, k_ref, v_ref):
    x2 = _ple_core(x_ref, p_ref, g_ref, proj_ref, gw_ref)
    o_ref[...] = x2
    kv = _dot(_rms(x2, kvg_ref[...], NORM_EPS).astype(BF16), kvw_ref[...]) + kvb_ref[...]
    kvd = k_ref.shape[1]
    cos, sin_lo, sin_hi = _rope_tables(pos_ref, inv_ref)
    k_ref[...] = _rope_apply(kv[:, :kvd], cos, sin_lo, sin_hi).astype(k_ref.dtype)
    v_ref[...] = kv[:, kvd:].astype(v_ref.dtype)


def _ple_final_kernel(x_ref, p_ref, g_ref, proj_ref, gw_ref, fg_ref, o_ref):
    x2 = _ple_core(x_ref, p_ref, g_ref, proj_ref, gw_ref)
    o_ref[...] = _rms(x2, fg_ref[...], NORM_EPS)


def _ple_kv(x, p, g, proj, gw, kvg, kvw, kvb, pos, inv, tile):
    T, D = x.shape
    P = p.shape[1]
    kvd = kvw.shape[1] // 2
    tok = lambda w: pl.BlockSpec((tile, w), lambda i: (i, 0))
    return pl.pallas_call(
        _ple_kv_kernel,
        grid=(T // tile,),
        in_specs=[tok(D), tok(P), _const_spec((1, D)), _const_spec(proj.shape),
                  _const_spec(gw.shape), _const_spec((1, D)), _const_spec(kvw.shape),
                  _const_spec(kvb.shape), tok(1), _const_spec(inv.shape)],
        out_specs=[tok(D), tok(kvd), tok(kvd)],
        out_shape=[jax.ShapeDtypeStruct((T, D), F32), jax.ShapeDtypeStruct((T, kvd), BF16),
                   jax.ShapeDtypeStruct((T, kvd), BF16)],
        compiler_params=_params("arbitrary"),
        name="ple_kv",
    )(x, p, g, proj, gw, kvg, kvw, kvb, pos, inv)


def _ple_final(x, p, g, proj, gw, fg, tile):
    T, D = x.shape
    P = p.shape[1]
    tok = lambda w: pl.BlockSpec((tile, w), lambda i: (i, 0))
    return pl.pallas_call(
        _ple_final_kernel,
        grid=(T // tile,),
        in_specs=[tok(D), tok(P), _const_spec((1, D)), _const_spec(proj.shape),
                  _const_spec(gw.shape), _const_spec((1, D))],
        out_specs=tok(D),
        out_shape=jax.ShapeDtypeStruct((T, D), F32),
        compiler_params=_params("arbitrary"),
        name="ple_final",
    )(x, p, g, proj, gw, fg)


def _attn_kernel(x_ref, g_ref, qw_ref, qb_ref, sink_ref, ow_ref, ob_ref, kc_ref, kp_ref, vc_ref,
                 vp_ref, pos_ref, inv_ref, o_ref, q_ref, att_ref, *, n_q_heads):
    tile = x_ref.shape[0]
    W = WINDOW
    nblk = tile // W
    q_per_kv = n_q_heads // N_KV_HEADS
    scale = HEAD_DIM ** -0.5
    first = pl.program_id(1) == 0

    h = _rms(x_ref[...], g_ref[...], NORM_EPS).astype(BF16)
    q = _dot(h, qw_ref[...]) + qb_ref[...]
    cos, sin_lo, sin_hi = _rope_tables(pos_ref, inv_ref)
    for lg in range(q.shape[1] // LANES):
        cs = slice(lg * LANES, (lg + 1) * LANES)
        q_ref[:, cs] = _rope_apply(q[:, cs], cos, sin_lo, sin_hi).astype(BF16)

    qi = lax.broadcasted_iota(jnp.int32, (W, 2 * W), 0)
    kj = lax.broadcasted_iota(jnp.int32, (W, 2 * W), 1)
    first_lo = jnp.where(first, W, 0)
    band = jnp.where((kj > qi) & (kj <= qi + W), 0.0, -jnp.inf)
    band_first = jnp.where(kj >= first_lo, band, -jnp.inf)
    band = jnp.concatenate([band] * q_per_kv, axis=0)
    band_first = jnp.concatenate([band_first] * q_per_kv, axis=0)

    for n in range(nblk):
        rows = slice(n * W, (n + 1) * W)
        if n == 0:
            kprev, vprev = kp_ref[...], vp_ref[...]
        else:
            kprev, vprev = kc_ref[(n - 1) * W:n * W, :], vc_ref[(n - 1) * W:n * W, :]
        kblk = jnp.concatenate([kprev, kc_ref[rows, :]], axis=0)
        vblk = jnp.concatenate([vprev, vc_ref[rows, :]], axis=0)
        bias = band_first if n == 0 else band
        for g in range(N_KV_HEADS):
            ks = kblk[:, g * HEAD_DIM:(g + 1) * HEAD_DIM]
            vs = vblk[:, g * HEAD_DIM:(g + 1) * HEAD_DIM]
            qs = jnp.concatenate(
                [q_ref[rows, (g * q_per_kv + r) * HEAD_DIM:(g * q_per_kv + r + 1) * HEAD_DIM]
                 for r in range(q_per_kv)], axis=0)
            s = _dot(qs, ks, _NT) * scale + bias
            sink = jnp.concatenate(
                [jnp.broadcast_to(sink_ref[:, g * q_per_kv + r:g * q_per_kv + r + 1], (W, 1))
                 for r in range(q_per_kv)], axis=0)
            m = jnp.maximum(jnp.max(s, axis=-1, keepdims=True), sink)
            e = jnp.exp(s - m)
            denom = jnp.sum(e, axis=-1, keepdims=True) + jnp.exp(sink - m)
            pr = (e / denom).astype(BF16)
            o = _dot(pr, vs)
            for r in range(q_per_kv):
                hh = g * q_per_kv + r
                att_ref[rows, hh * HEAD_DIM:(hh + 1) * HEAD_DIM] = o[r * W:(r + 1) * W, :].astype(BF16)

    o_ref[...] = x_ref[...] + _dot(att_ref[...], ow_ref[...]) + ob_ref[...]


def _attn(x, g, qw, qb, sinks, ow, ob, k, v, pos, inv, batch, tile):
    T, D = x.shape
    kvd = k.shape[1]
    nq = qw.shape[1] // HEAD_DIM
    nt = T // batch // tile
    bpt = tile // WINDOW
    row = lambda b, i: (b * nt + i, 0)
    prev = lambda b, i: (jnp.maximum((b * nt + i) * bpt - 1, 0), 0)
    kern = functools.partial(_attn_kernel, n_q_heads=nq)
    return pl.pallas_call(
        kern,
        grid=(batch, nt),
        in_specs=[pl.BlockSpec((tile, D), row), _const_spec((1, D)), _const_spec(qw.shape),
                  _const_spec(qb.shape), _const_spec(sinks.shape), _const_spec(ow.shape),
                  _const_spec(ob.shape),
                  pl.BlockSpec((tile, kvd), row), pl.BlockSpec((WINDOW, kvd), prev),
                  pl.BlockSpec((tile, kvd), row), pl.BlockSpec((WINDOW, kvd), prev),
                  pl.BlockSpec((tile, 1), row), _const_spec(inv.shape)],
        out_specs=pl.BlockSpec((tile, D), row),
        out_shape=jax.ShapeDtypeStruct((T, D), F32),
        scratch_shapes=[pltpu.VMEM((tile, qw.shape[1]), BF16),
                        pltpu.VMEM((tile, qw.shape[1]), BF16)],
        compiler_params=_params("arbitrary", "arbitrary"),
        name="swa_attn",
    )(x, g, qw, qb, sinks, ow, ob, k, k, v, v, pos, inv)


def _row(v):
    return v.reshape(1, -1)


def kernel(x, p, positions, ssm_norm, ssm_in_w, ssm_conv_w, ssm_conv_b, ssm_dt_bias, ssm_A_log, ssm_D, ssm_gate_norm, ssm_out_w, kv_norm, kv_w, kv_b, attn_norm, q_w, q_b, sinks, o_w, o_b, peer_norm, peer_q_w, peer_sub_keys, peer_u, peer_v, ple_norm, ple_proj, ple_gate_w, final_norm):
    B, S, D = x.shape
    T = B * S
    depth = p.shape[0]
    n_a = ssm_norm.shape[0]
    H = ssm_D.shape[1]
    d_inner = H * SSM_HEADDIM
    conv_dim = ssm_conv_w.shape[2]

    xt = x.reshape(T, D)
    pos = positions.reshape(T, 1)
    lane = np.arange(LANES) % HEAD_DIM
    inv = np.where(lane < ROT_DIM,
                   ROPE_THETA ** (-(2.0 * (lane % (ROT_DIM // 2))) / ROT_DIM), 0.0)
    inv = jnp.asarray(inv.reshape(1, LANES), F32)
    expand = jnp.asarray(np.repeat(np.eye(H, dtype=np.float32), SSM_HEADDIM, axis=1))
    tril = jnp.asarray(np.tril(np.ones((SSD_CHUNK, SSD_CHUNK), np.float32)))

    k_sh = v_sh = None
    for i in range(depth):
        if i < n_a:
            w = ssm_in_w[i].astype(BF16)
            z, xbc, dtr = _inproj(xt, _row(ssm_norm[i]), w[:, :d_inner],
                                  w[:, d_inner:d_inner + conv_dim], w[:, d_inner + conv_dim:],
                                  tile=256)
            y = _ssd(z, xbc, dtr, ssm_conv_w[i], _row(ssm_conv_b[i]), _row(ssm_dt_bias[i]),
                     _row(ssm_A_log[i]), _row(jnp.repeat(ssm_D[i], SSM_HEADDIM)),
                     _row(ssm_gate_norm[i]), expand, tril, batch=B)
            xt = _outproj(xt, y, ssm_out_w[i].astype(BF16), tile=512)
        else:
            j = i - n_a
            xt = _attn(xt, _row(attn_norm[j]), q_w[j].astype(BF16), _row(q_b[j]), _row(sinks[j]),
                       o_w[j].astype(BF16), _row(o_b[j]), k_sh, v_sh, pos, inv, batch=B, tile=512)
        xt = _peer(xt, _row(peer_norm[i]), peer_q_w[i].T.astype(BF16),
                   peer_sub_keys[i].astype(BF16), peer_u[i].astype(BF16),
                   peer_v[i].T.astype(BF16), tile=512, ec=1024)
        if i == n_a - 1:
            xt, k_sh, v_sh = _ple_kv(xt, p[i].reshape(T, -1), _row(ple_norm[i]),
                                     ple_proj[i].astype(BF16), ple_gate_w[i].astype(BF16),
                                     _row(kv_norm), kv_w.astype(BF16), _row(kv_b), pos, inv,
                                     tile=512)
        elif i == depth - 1:
            xt = _ple_final(xt, p[i].reshape(T, -1), _row(ple_norm[i]), ple_proj[i].astype(BF16),
                            ple_gate_w[i].astype(BF16), _row(final_norm), tile=512)
        else:
            raise NotImplementedError("PLE without K/V or final norm")
    return xt.reshape(B, S, D)
```

```python
import functools
import math

import jax
import jax.numpy as jnp
import numpy as np
from jax import lax
from jax.experimental import pallas as pl
from jax.experimental.pallas import tpu as pltpu

F32 = jnp.float32
BF16 = jnp.bfloat16

NORM_EPS = 1e-6
GATED_NORM_EPS = 1e-5
SSM_HEADDIM = 64
SSM_GROUPS = 8
SSM_STATE = 128
CONV_K = 4
SSD_CHUNK = 128
HEAD_DIM = 64
N_KV_HEADS = 2
WINDOW = 128
ROT_DIM = HEAD_DIM // 4
ROPE_THETA = 500000.0
PEER_HEADS = 8
N_KEYS = 128
PEER_TOPK = 16

LANES = 128
SUBLANES = 8
VMEM_LIMIT = 56 * 1024 * 1024

_CAND_PAIRS = [(r1, r2) for r1 in range(PEER_TOPK + 1) for r2 in range(PEER_TOPK + 1)
               if (r1 + 1) * (r2 + 1) <= PEER_TOPK + 1]


def _sort_network(n):
    pairs = []
    p = 1
    while p < n:
        k = p
        while k >= 1:
            for j in range(k % p, n - k, 2 * k):
                for i in range(min(k, n - j - k)):
                    if (i + j) // (2 * p) == (i + j + k) // (2 * p):
                        pairs.append((i + j, i + j + k))
            k //= 2
        p *= 2
    return pairs


_SORT_PAIRS = _sort_network(N_KEYS // SUBLANES)


def _params(*sem):
    return pltpu.CompilerParams(dimension_semantics=sem, vmem_limit_bytes=VMEM_LIMIT)


def _const_spec(shape):
    nd = len(shape)
    return pl.BlockSpec(shape, lambda *_: (0,) * nd, pipeline_mode=pl.Buffered(1))


def _rms(x, g, eps):
    return x * lax.rsqrt(jnp.mean(x * x, axis=-1, keepdims=True) + eps) * g


def _dot(a, b, dims=None, precision=None):
    if dims is None:
        dims = (((a.ndim - 1,), (0,)), ((), ()))
    return lax.dot_general(a, b, dims, precision=precision, preferred_element_type=F32)


_NT = (((1,), (1,)), ((), ()))
_TN = (((0,), (0,)), ((), ()))
_HI = lax.Precision.HIGHEST


def _inproj_kernel(x_ref, g_ref, wz_ref, wx_ref, wd_ref, z_ref, xbc_ref, dt_ref):
    h = _rms(x_ref[...], g_ref[...], NORM_EPS).astype(BF16)
    z_ref[...] = _dot(h, wz_ref[...])
    xbc_ref[...] = _dot(h, wx_ref[...])
    dt_ref[...] = _dot(h, wd_ref[...])


def _inproj(x, g, wz, wx, wd, tile):
    T, D = x.shape
    nz, nx, nd = wz.shape[1], wx.shape[1], wd.shape[1]
    return pl.pallas_call(
        _inproj_kernel,
        grid=(T // tile,),
        in_specs=[pl.BlockSpec((tile, D), lambda i: (i, 0)),
                  _const_spec((1, D)), _const_spec(wz.shape), _const_spec(wx.shape),
                  _const_spec(wd.shape)],
        out_specs=[pl.BlockSpec((tile, nz), lambda i: (i, 0)),
                   pl.BlockSpec((tile, nx), lambda i: (i, 0)),
                   pl.BlockSpec((tile, nd), lambda i: (i, 0))],
        out_shape=[jax.ShapeDtypeStruct((T, nz), F32), jax.ShapeDtypeStruct((T, nx), F32),
                   jax.ShapeDtypeStruct((T, nd), F32)],
        compiler_params=_params("arbitrary"),
        name="ssm_inproj",
    )(x, g, wz, wx, wd)


def _ssd_kernel(z_ref, xbc_ref, dtr_ref, cw_ref, cb_ref, dtb_ref, alog_ref, dexp_ref, gn_ref,
                expand_ref, tril_ref, y_ref, xb_ref, st_ref, *, d_inner, n_heads):
    L = SSD_CHUNK
    gw = d_inner // SSM_GROUPS
    hpg = n_heads // SSM_GROUPS
    tail = SUBLANES

    @pl.when(pl.program_id(1) == 0)
    def _():
        xb_ref[0:tail, :] = jnp.zeros((tail, xb_ref.shape[1]), F32)
        st_ref[...] = jnp.zeros(st_ref.shape, F32)

    xb_ref[tail:tail + L, :] = xbc_ref[...]

    def conv_silu(lo, width):
        acc = cb_ref[:, lo:lo + width]
        for k in range(CONV_K):
            off = tail - (CONV_K - 1) + k
            acc = acc + xb_ref[off:off + L, lo:lo + width] * cw_ref[k:k + 1, lo:lo + width]
        return acc * jax.nn.sigmoid(acc)

    dt_in = dtr_ref[...] + dtb_ref[...]
    dt = jnp.maximum(dt_in, 0.0) + jnp.log1p(jnp.exp(-jnp.abs(dt_in)))
    a = dt * (-jnp.exp(alog_ref[...]))
    tril = tril_ref[...]
    expand = expand_ref[...]
    a_cs = _dot(tril, a, precision=_HI)
    a_cs_t = a_cs.T
    a_cs_x = _dot(tril, _dot(a, expand, precision=_HI), precision=_HI)
    dt_x = _dot(dt, expand, precision=_HI)
    a_last_x = a_cs_x[L - 1:L, :]
    causal = tril > 0.5

    for g in range(SSM_GROUPS):
        lo = g * gw
        xs = conv_silu(lo, gw)
        bm = conv_silu(d_inner + g * SSM_STATE, SSM_STATE).astype(BF16)
        cm = conv_silu(d_inner + SSM_GROUPS * SSM_STATE + g * SSM_STATE, SSM_STATE).astype(BF16)
        xdt = xs * dt_x[:, lo:lo + gw]
        cb = _dot(cm, bm, _NT)
        yd = []
        for r in range(hpg):
            hh = g * hpg + r
            seg = a_cs[:, hh:hh + 1] - a_cs_t[hh:hh + 1, :]
            lmat = jnp.exp(jnp.where(causal, seg, -jnp.inf))
            m = (cb * lmat).astype(BF16)
            yd.append(_dot(m, xdt[:, r * SSM_HEADDIM:(r + 1) * SSM_HEADDIM].astype(BF16)))
        y = jnp.concatenate(yd, axis=1)
        acx = a_cs_x[:, lo:lo + gw]
        alx = a_last_x[:, lo:lo + gw]
        prev = st_ref[g]
        y = y + _dot(cm, prev.astype(BF16)) * jnp.exp(acx)
        xd = (xdt * jnp.exp(alx - acx)).astype(BF16)
        st_ref[g] = prev * jnp.exp(alx) + _dot(bm, xd, _TN)
        y = y + xs * dexp_ref[:, lo:lo + gw]
        zg = z_ref[:, lo:lo + gw]
        y = y * (zg * jax.nn.sigmoid(zg))
        y = y * lax.rsqrt(jnp.mean(y * y, axis=-1, keepdims=True) + GATED_NORM_EPS)
        y_ref[:, lo:lo + gw] = (y * gn_ref[:, lo:lo + gw]).astype(y_ref.dtype)

    xb_ref[0:tail, :] = xb_ref[L:L + tail, :]


def _ssd(z, xbc, dtr, cw, cb, dtb, alog, dexp, gn, expand, tril, batch):
    T, d_inner = z.shape
    conv_dim = xbc.shape[1]
    H = dtr.shape[1]
    L = SSD_CHUNK
    nc = T // batch // L
    row = lambda b, c: (b * nc + c, 0)
    kern = functools.partial(_ssd_kernel, d_inner=d_inner, n_heads=H)
    return pl.pallas_call(
        kern,
        grid=(batch, nc),
        in_specs=[pl.BlockSpec((L, d_inner), row), pl.BlockSpec((L, conv_dim), row),
                  pl.BlockSpec((L, H), row),
                  _const_spec(cw.shape), _const_spec(cb.shape), _const_spec(dtb.shape),
                  _const_spec(alog.shape), _const_spec(dexp.shape), _const_spec(gn.shape),
                  _const_spec(expand.shape), _const_spec(tril.shape)],
        out_specs=pl.BlockSpec((L, d_inner), row),
        out_shape=jax.ShapeDtypeStruct((T, d_inner), BF16),
        scratch_shapes=[pltpu.VMEM((L + SUBLANES, conv_dim), F32),
                        pltpu.VMEM((SSM_GROUPS, SSM_STATE, d_inner // SSM_GROUPS), F32)],
        compiler_params=_params("arbitrary", "arbitrary"),
        name="ssd_scan",
    )(z, xbc, dtr, cw, cb, dtb, alog, dexp, gn, expand, tril)


def _outproj_kernel(x_ref, y_ref, w_ref, o_ref):
    o_ref[...] = x_ref[...] + _dot(y_ref[...], w_ref[...])


def _outproj(x, y, w, tile):
    T, D = x.shape
    K = y.shape[1]
    return pl.pallas_call(
        _outproj_kernel,
        grid=(T // tile,),
        in_specs=[pl.BlockSpec((tile, D), lambda i: (i, 0)),
                  pl.BlockSpec((tile, K), lambda i: (i, 0)), _const_spec(w.shape)],
        out_specs=pl.BlockSpec((tile, D), lambda i: (i, 0)),
        out_shape=jax.ShapeDtypeStruct((T, D), F32),
        compiler_params=_params("arbitrary"),
        name="ssm_outproj",
    )(x, y, w)


def _peer_kernel(x_ref, g_ref, wq_ref, keys_ref, u0_ref, un_ref, vt_ref, o_ref,
                 hb_ref, e1_ref, e2_ref, gmin_ref, top_ref,
                 a0_ref, a1_ref, p0_ref, p1_ref, acca_ref, accb_ref):
    g = pl.program_id(1)
    last = pl.num_programs(1) - 1
    tt = x_ref.shape[0]
    ec = un_ref.shape[0] // 2
    rows_per_chunk = ec // N_KEYS
    mxu_cols = 2 * LANES
    n_piece = tt // mxu_cols
    half = N_KEYS
    ntop = PEER_TOPK + 1
    neg_inf = -jnp.inf

    @pl.when(g == 0)
    def _route():
        hb = _rms(x_ref[...], g_ref[...], NORM_EPS).T.astype(BF16)
        hb_ref[...] = hb
        for h in range(PEER_HEADS):
            q = _dot(wq_ref[h * 2 * half:(h + 1) * 2 * half, :], hb).astype(BF16)
            a0_ref[h * N_KEYS:(h + 1) * N_KEYS, :] = _dot(keys_ref[0], q[0:half, :])
            a1_ref[h * N_KEYS:(h + 1) * N_KEYS, :] = _dot(keys_ref[1], q[half:2 * half, :])

        def top_values(s_ref, h, cs):
            v = [s_ref[h * N_KEYS + r * SUBLANES:h * N_KEYS + (r + 1) * SUBLANES, cs]
                 for r in range(N_KEYS // SUBLANES)]
            for i, j in _SORT_PAIRS:
                v[i], v[j] = jnp.maximum(v[i], v[j]), jnp.minimum(v[i], v[j])
            depth = len(v)
            out = []
            for k in range(ntop):
                m = v[0]
                for shift in (4, 2, 1):
                    m = jnp.maximum(m, pltpu.roll(m, shift, 0))
                out.append(m)
                hit = v[0] == m
                for r in range(min(depth, ntop - 1 - k)):
                    v[r] = jnp.where(hit, v[r + 1] if r + 1 < depth else neg_inf, v[r])
            return out

        for tc in range(tt // LANES):
            cs = slice(tc * LANES, (tc + 1) * LANES)
            for c, s_ref in enumerate((a0_ref, a1_ref)):
                for h in range(PEER_HEADS):
                    for r, m in enumerate(top_values(s_ref, h, cs)):
                        top_ref[c, r, h:h + 1, cs] = m[0:1, :]

            a = [top_ref[0, r, :, cs] for r in range(ntop)]
            b = [top_ref[1, r, :, cs] for r in range(ntop)]
            cand = [a[r1] + b[r2] for r1, r2 in _CAND_PAIRS]
            a0, b0 = a[0], b[0]
            m0 = a0 + b0
            z = jnp.zeros(m0.shape, F32)
            for k in range(ntop):
                m = functools.reduce(jnp.maximum, cand)
                if k < PEER_TOPK:
                    z = z + jnp.exp(m - m0)
                if k == PEER_TOPK - 1:
                    v16 = m
                if k < ntop - 1:
                    cand = [jnp.where(cv == m, neg_inf, cv) for cv in cand]
            v17 = m
            zinv = 1.0 / z
            gmin = jnp.exp(0.5 * (v16 + v17) - m0) * zinv
            for h in range(PEER_HEADS):
                hk = slice(h * N_KEYS, (h + 1) * N_KEYS)
                e1_ref[h, tc] = jnp.exp(a0_ref[hk, cs] - a0[h:h + 1, :])
                e2_ref[h, tc] = jnp.exp(a1_ref[hk, cs] - b0[h:h + 1, :]) * zinv[h:h + 1, :]
                gmin_ref[h, tc] = gmin[h:h + 1, :]

            if (tc + 1) % (mxu_cols // LANES) == 0:
                ps = slice((tc + 1) * LANES - mxu_cols, (tc + 1) * LANES)
                a0_ref[:, ps] = _dot(u0_ref[0:ec, :], hb_ref[:, ps])
                a1_ref[:, ps] = _dot(u0_ref[ec:2 * ec, :], hb_ref[:, ps])

        acca_ref[...] = jnp.zeros(acca_ref.shape, F32)
        accb_ref[...] = jnp.zeros(accb_ref.shape, F32)

    def gate_gelu(a_ref, p_ref, chunk, tc):
        i0 = pl.multiple_of(chunk * rows_per_chunk, SUBLANES)
        cs = slice(tc * LANES, (tc + 1) * LANES)
        e18 = [e1_ref[h, tc, pl.ds(i0, rows_per_chunk), :] for h in range(PEER_HEADS)]
        gm = [gmin_ref[h, tc] for h in range(PEER_HEADS)]
        for ii in range(rows_per_chunk):
            rows = slice(ii * N_KEYS, (ii + 1) * N_KEYS)
            w = jnp.zeros((N_KEYS, LANES), F32)
            for h in range(PEER_HEADS):
                gate = e18[h][ii:ii + 1, :] * e2_ref[h, tc]
                w = w + jnp.where(gate >= gm[h], gate, 0.0)
            av = a_ref[rows, cs]
            gelu = 0.5 * av * (1.0 + lax.erf(av * np.float32(math.sqrt(0.5))))
            p_ref[rows, cs] = (w * gelu).astype(BF16)

    for par, (a_ref, p_ref, acc_ref) in enumerate(((a0_ref, p0_ref, acca_ref),
                                                   (a1_ref, p1_ref, accb_ref))):
        es = slice(par * ec, (par + 1) * ec)
        for piece in range(n_piece):
            cs = slice(piece * mxu_cols, (piece + 1) * mxu_cols)
            for tc in range(piece * (mxu_cols // LANES), (piece + 1) * (mxu_cols // LANES)):
                gate_gelu(a_ref, p_ref, 2 * g + par, tc)
            acc_ref[:, cs] += _dot(vt_ref[:, es], p_ref[:, cs])
            a_ref[:, cs] = _dot(un_ref[es, :], hb_ref[:, cs])

    @pl.when(g == last)
    def _finish():
        o_ref[...] = x_ref[...] + (acca_ref[...] + accb_ref[...]).T


def _peer(x, g, wq_t, keys, u, vt, tile, ec):
    T, D = x.shape
    E = u.shape[0]
    nchunk = E // ec
    assert E == N_KEYS * N_KEYS and ec == SUBLANES * N_KEYS and tile % LANES == 0
    assert ec == PEER_HEADS * N_KEYS
    assert nchunk % 2 == 0
    ntc = tile // LANES
    nstep = nchunk // 2
    return pl.pallas_call(
        _peer_kernel,
        grid=(T // tile, nstep),
        in_specs=[pl.BlockSpec((tile, D), lambda i, s: (i, 0)),
                  _const_spec((1, D)), _const_spec(wq_t.shape), _const_spec(keys.shape),
                  pl.BlockSpec((2 * ec, D), lambda i, s: (0, 0), pipeline_mode=pl.Buffered(1)),
                  pl.BlockSpec((2 * ec, D), lambda i, s: (jnp.minimum(s + 1, nstep - 1), 0)),
                  pl.BlockSpec((D, 2 * ec), lambda i, s: (0, s))],
        out_specs=pl.BlockSpec((tile, D), lambda i, s: (i, 0)),
        out_shape=jax.ShapeDtypeStruct((T, D), F32),
        scratch_shapes=[pltpu.VMEM((D, tile), BF16),
                        pltpu.VMEM((PEER_HEADS, ntc, N_KEYS, LANES), F32),
                        pltpu.VMEM((PEER_HEADS, ntc, N_KEYS, LANES), F32),
                        pltpu.VMEM((PEER_HEADS, ntc, 1, LANES), F32),
                        pltpu.VMEM((2, PEER_TOPK + 1, PEER_HEADS, tile), F32),
                        pltpu.VMEM((ec, tile), F32), pltpu.VMEM((ec, tile), F32),
                        pltpu.VMEM((ec, tile), BF16), pltpu.VMEM((ec, tile), BF16),
                        pltpu.VMEM((D, tile), F32), pltpu.VMEM((D, tile), F32)],
        compiler_params=_params("arbitrary", "arbitrary"),
        name="peer",
    )(x, g, wq_t, keys, u, u, vt)


def _rope_tables(pos_ref, inv_ref):
    ang = pos_ref[...].astype(F32) * inv_ref[...]
    lane = lax.broadcasted_iota(jnp.int32, ang.shape, 1) % HEAD_DIM
    cos = jnp.cos(ang)
    sin = jnp.sin(ang)
    half = ROT_DIM // 2
    sin_lo = jnp.where(lane < half, -sin, 0.0)
    sin_hi = jnp.where((lane >= half) & (lane < ROT_DIM), sin, 0.0)
    return cos, sin_lo, sin_hi


def _rope_apply(t, cos, sin_lo, sin_hi):
    half = ROT_DIM // 2
    up = pltpu.roll(t, LANES - half, 1)
    dn = pltpu.roll(t, half, 1)
    return t * cos + up * sin_lo + dn * sin_hi


def _ple_core(x_ref, p_ref, g_ref, proj_ref, gw_ref):
    x = x_ref[...]
    hn = _rms(x, g_ref[...], NORM_EPS).astype(BF16)
    gate = jax.nn.sigmoid(_dot(hn, gw_ref[...]))
    return x + _dot(p_ref[...].astype(BF16), proj_ref[...]) * gate


def _ple_kv_kernel(x_ref, p_ref, g_ref, proj_ref, gw_ref, kvg_ref, kvw_ref, kvb_ref, pos_ref,
                   inv_ref, o_ref, k_ref, v_ref):
    x2 = _ple_core(x_ref, p_ref, g_ref, proj_ref, gw_ref)
    o_ref[...] = x2
    kv = _dot(_rms(x2, kvg_ref[...], NORM_EPS).astype(BF16), kvw_ref[...]) + kvb_ref[...]
    kvd = k_ref.shape[1]
    cos, sin_lo, sin_hi = _rope_tables(pos_ref, inv_ref)
    k_ref[...] = _rope_apply(kv[:, :kvd], cos, sin_lo, sin_hi).astype(k_ref.dtype)
    v_ref[...] = kv[:, kvd:].astype(v_ref.dtype)


def _ple_final_kernel(x_ref, p_ref, g_ref, proj_ref, gw_ref, fg_ref, o_ref):
    x2 = _ple_core(x_ref, p_ref, g_ref, proj_ref, gw_ref)
    o_ref[...] = _rms(x2, fg_ref[...], NORM_EPS)


def _ple_kv(x, p, g, proj, gw, kvg, kvw, kvb, pos, inv, tile):
    T, D = x.shape
    P = p.shape[1]
    kvd = kvw.shape[1] // 2
    tok = lambda w: pl.BlockSpec((tile, w), lambda i: (i, 0))
    return pl.pallas_call(
        _ple_kv_kernel,
        grid=(T // tile,),
        in_specs=[tok(D), tok(P), _const_spec((1, D)), _const_spec(proj.shape),
                  _const_spec(gw.shape), _const_spec((1, D)), _const_spec(kvw.shape),
                  _const_spec(kvb.shape), tok(1), _const_spec(inv.shape)],
        out_specs=[tok(D), tok(kvd), tok(kvd)],
        out_shape=[jax.ShapeDtypeStruct((T, D), F32), jax.ShapeDtypeStruct((T, kvd), BF16),
                   jax.ShapeDtypeStruct((T, kvd), BF16)],
        compiler_params=_params("arbitrary"),
        name="ple_kv",
    )(x, p, g, proj, gw, kvg, kvw, kvb, pos, inv)


def _ple_final(x, p, g, proj, gw, fg, tile):
    T, D = x.shape
    P = p.shape[1]
    tok = lambda w: pl.BlockSpec((tile, w), lambda i: (i, 0))
    return pl.pallas_call(
        _ple_final_kernel,
        grid=(T // tile,),
        in_specs=[tok(D), tok(P), _const_spec((1, D)), _const_spec(proj.shape),
                  _const_spec(gw.shape), _const_spec((1, D))],
        out_specs=tok(D),
        out_shape=jax.ShapeDtypeStruct((T, D), F32),
        compiler_params=_params("arbitrary"),
        name="ple_final",
    )(x, p, g, proj, gw, fg)


def _attn_kernel(x_ref, g_ref, qw_ref, qb_ref, sink_ref, owt_ref, ob_ref, kc_ref, kp_ref, vc_ref,
                 vp_ref, pos_ref, invc_ref, o_ref, qt_ref, att_ref, *, n_q_heads):
    tile = x_ref.shape[0]
    W = WINDOW
    nblk = tile // W
    q_per_kv = n_q_heads // N_KV_HEADS
    half = ROT_DIM // 2
    scale = HEAD_DIM ** -0.5
    first = pl.program_id(1) == 0

    h = _rms(x_ref[...], g_ref[...], NORM_EPS).astype(BF16)
    q = (_dot(h, qw_ref[...]) + qb_ref[...]) * scale
    qt = q.T
    ang = invc_ref[...] * pos_ref[...].astype(F32)
    cos, sin = jnp.cos(ang), jnp.sin(ang)
    for hh in range(n_q_heads):
        base = hh * HEAD_DIM
        t1 = qt[base:base + half, :]
        t2 = qt[base + half:base + ROT_DIM, :]
        qt_ref[base:base + half, :] = (t1 * cos - t2 * sin).astype(BF16)
        qt_ref[base + half:base + ROT_DIM, :] = (t2 * cos + t1 * sin).astype(BF16)
        qt_ref[base + ROT_DIM:base + HEAD_DIM, :] = qt[base + ROT_DIM:base + HEAD_DIM, :].astype(BF16)

    kj = lax.broadcasted_iota(jnp.int32, (2 * W, W), 0)
    qi = lax.broadcasted_iota(jnp.int32, (2 * W, W), 1)
    first_lo = jnp.where(first, W, 0)
    band = jnp.where((kj > qi) & (kj <= qi + W), 0.0, -jnp.inf)
    band_first = jnp.where(kj >= first_lo, band, -jnp.inf)
    band = jnp.concatenate([band] * q_per_kv, axis=1)
    band_first = jnp.concatenate([band_first] * q_per_kv, axis=1)

    for n in range(nblk):
        cols = slice(n * W, (n + 1) * W)
        if n == 0:
            kprev, vprev = kp_ref[...], vp_ref[...]
        else:
            kprev, vprev = kc_ref[(n - 1) * W:n * W, :], vc_ref[(n - 1) * W:n * W, :]
        kblk = jnp.concatenate([kprev, kc_ref[cols, :]], axis=0)
        vblk = jnp.concatenate([vprev, vc_ref[cols, :]], axis=0)
        bias = band_first if n == 0 else band
        for g in range(N_KV_HEADS):
            ks = kblk[:, g * HEAD_DIM:(g + 1) * HEAD_DIM]
            vs = vblk[:, g * HEAD_DIM:(g + 1) * HEAD_DIM]
            heads = range(g * q_per_kv, (g + 1) * q_per_kv)
            qs = jnp.concatenate([qt_ref[hh * HEAD_DIM:(hh + 1) * HEAD_DIM, cols] for hh in heads],
                                 axis=1)
            sink = jnp.concatenate([jnp.broadcast_to(sink_ref[:, hh:hh + 1], (1, W)) for hh in heads],
                                   axis=1)
            s = _dot(ks, qs) + bias
            m = jnp.maximum(jnp.max(s, axis=0, keepdims=True), sink)
            e = jnp.exp(s - m)
            denom = jnp.sum(e, axis=0, keepdims=True) + jnp.exp(sink - m)
            o = _dot(vs, e.astype(BF16), _TN) / denom
            for r, hh in enumerate(heads):
                att_ref[hh * HEAD_DIM:(hh + 1) * HEAD_DIM, cols] = o[:, r * W:(r + 1) * W].astype(BF16)

    o_ref[...] = x_ref[...] + _dot(owt_ref[...], att_ref[...]).T + ob_ref[...]


def _attn(x, g, qw, qb, sinks, ow_t, ob, k, v, pos, inv_col, batch, tile):
    T, D = x.shape
    kvd = k.shape[1]
    nq = qw.shape[1] // HEAD_DIM
    nt = T // batch // tile
    bpt = tile // WINDOW
    row = lambda b, i: (b * nt + i, 0)
    prev = lambda b, i: (jnp.maximum((b * nt + i) * bpt - 1, 0), 0)
    kern = functools.partial(_attn_kernel, n_q_heads=nq)
    return pl.pallas_call(
        kern,
        grid=(batch, nt),
        in_specs=[pl.BlockSpec((tile, D), row), _const_spec((1, D)), _const_spec(qw.shape),
                  _const_spec(qb.shape), _const_spec(sinks.shape), _const_spec(ow_t.shape),
                  _const_spec(ob.shape),
                  pl.BlockSpec((tile, kvd), row), pl.BlockSpec((WINDOW, kvd), prev),
                  pl.BlockSpec((tile, kvd), row), pl.BlockSpec((WINDOW, kvd), prev),
                  pl.BlockSpec((None, 1, tile), lambda b, i: (b * nt + i, 0, 0)),
                  _const_spec(inv_col.shape)],
        out_specs=pl.BlockSpec((tile, D), row),
        out_shape=jax.ShapeDtypeStruct((T, D), F32),
        scratch_shapes=[pltpu.VMEM((qw.shape[1], tile), BF16),
                        pltpu.VMEM((qw.shape[1], tile), BF16)],
        compiler_params=_params("arbitrary", "arbitrary"),
        name="swa_attn",
    )(x, g, qw, qb, sinks, ow_t, ob, k, k, v, v, pos.reshape(T // tile, 1, tile), inv_col)


def _row(v):
    return v.reshape(1, -1)


def kernel(x, p, positions, ssm_norm, ssm_in_w, ssm_conv_w, ssm_conv_b, ssm_dt_bias, ssm_A_log, ssm_D, ssm_gate_norm, ssm_out_w, kv_norm, kv_w, kv_b, attn_norm, q_w, q_b, sinks, o_w, o_b, peer_norm, peer_q_w, peer_sub_keys, peer_u, peer_v, ple_norm, ple_proj, ple_gate_w, final_norm):
    B, S, D = x.shape
    T = B * S
    depth = p.shape[0]
    n_a = ssm_norm.shape[0]
    H = ssm_D.shape[1]
    d_inner = H * SSM_HEADDIM
    conv_dim = ssm_conv_w.shape[2]

    xt = x.reshape(T, D)
    pos = positions.reshape(T, 1)
    lane = np.arange(LANES) % HEAD_DIM
    inv = np.where(lane < ROT_DIM,
                   ROPE_THETA ** (-(2.0 * (lane % (ROT_DIM // 2))) / ROT_DIM), 0.0)
    inv = jnp.asarray(inv.reshape(1, LANES), F32)
    inv_col = inv[0, :ROT_DIM // 2].reshape(-1, 1)
    expand = jnp.asarray(np.repeat(np.eye(H, dtype=np.float32), SSM_HEADDIM, axis=1))
    tril = jnp.asarray(np.tril(np.ones((SSD_CHUNK, SSD_CHUNK), np.float32)))

    k_sh = v_sh = None
    for i in range(depth):
        if i < n_a:
            w = ssm_in_w[i].astype(BF16)
            z, xbc, dtr = _inproj(xt, _row(ssm_norm[i]), w[:, :d_inner],
                                  w[:, d_inner:d_inner + conv_dim], w[:, d_inner + conv_dim:],
                                  tile=256)
            y = _ssd(z, xbc, dtr, ssm_conv_w[i], _row(ssm_conv_b[i]), _row(ssm_dt_bias[i]),
                     _row(ssm_A_log[i]), _row(jnp.repeat(ssm_D[i], SSM_HEADDIM)),
                     _row(ssm_gate_norm[i]), expand, tril, batch=B)
            xt = _outproj(xt, y, ssm_out_w[i].astype(BF16), tile=512)
        else:
            j = i - n_a
            xt = _attn(xt, _row(attn_norm[j]), q_w[j].astype(BF16), _row(q_b[j]), _row(sinks[j]),
                       o_w[j].T.astype(BF16), _row(o_b[j]), k_sh, v_sh, pos, inv_col, batch=B,
                       tile=512)
        xt = _peer(xt, _row(peer_norm[i]), peer_q_w[i].T.astype(BF16),
                   peer_sub_keys[i].astype(BF16), peer_u[i].astype(BF16),
                   peer_v[i].T.astype(BF16), tile=512, ec=1024)
        if i == n_a - 1:
            xt, k_sh, v_sh = _ple_kv(xt, p[i].reshape(T, -1), _row(ple_norm[i]),
                                     ple_proj[i].astype(BF16), ple_gate_w[i].astype(BF16),
                                     _row(kv_norm), kv_w.astype(BF16), _row(kv_b), pos, inv,
                                     tile=512)
        elif i == depth - 1:
            xt = _ple_final(xt, p[i].reshape(T, -1), _row(ple_norm[i]), ple_proj[i].astype(BF16),
                            ple_gate_w[i].astype(BF16), _row(final_norm), tile=512)
        else:
            raise NotImplementedError("PLE without K/V or final norm")
    return xt.reshape(B, S, D)
```

```python
import functools
import math

import jax
import jax.numpy as jnp
import numpy as np
from jax import lax
from jax.experimental import pallas as pl
from jax.experimental.pallas import tpu as pltpu

F32 = jnp.float32
BF16 = jnp.bfloat16

NORM_EPS = 1e-6
GATED_NORM_EPS = 1e-5
SSM_HEADDIM = 64
SSM_GROUPS = 8
SSM_STATE = 128
CONV_K = 4
SSD_CHUNK = 128
CONV_COLS = 512
HEAD_DIM = 64
N_KV_HEADS = 2
WINDOW = 128
ROT_DIM = HEAD_DIM // 4
ROPE_THETA = 500000.0
PEER_HEADS = 8
N_KEYS = 128
PEER_TOPK = 16

LANES = 128
SUBLANES = 8
VMEM_LIMIT = 56 * 1024 * 1024

_CAND_PAIRS = [(r1, r2) for r1 in range(PEER_TOPK + 1) for r2 in range(PEER_TOPK + 1)
               if (r1 + 1) * (r2 + 1) <= PEER_TOPK + 1]


def _sort_network(n):
    pairs = []
    p = 1
    while p < n:
        k = p
        while k >= 1:
            for j in range(k % p, n - k, 2 * k):
                for i in range(min(k, n - j - k)):
                    if (i + j) // (2 * p) == (i + j + k) // (2 * p):
                        pairs.append((i + j, i + j + k))
            k //= 2
        p *= 2
    return pairs


_SORT_PAIRS = _sort_network(N_KEYS // SUBLANES)


def _params(*sem):
    return pltpu.CompilerParams(dimension_semantics=sem, vmem_limit_bytes=VMEM_LIMIT)


def _const_spec(shape):
    nd = len(shape)
    return pl.BlockSpec(shape, lambda *_: (0,) * nd, pipeline_mode=pl.Buffered(1))


def _rms(x, g, eps):
    return x * lax.rsqrt(jnp.mean(x * x, axis=-1, keepdims=True) + eps) * g


def _dot(a, b, dims=None, precision=None):
    if dims is None:
        dims = (((a.ndim - 1,), (0,)), ((), ()))
    return lax.dot_general(a, b, dims, precision=precision, preferred_element_type=F32)


_NT = (((1,), (1,)), ((), ()))
_TN = (((0,), (0,)), ((), ()))
_HI = lax.Precision.HIGHEST


def _inproj_kernel(x_ref, g_ref, wz_ref, wx_ref, wd_ref, cw_ref, cb_ref, z_ref, xs_ref, bc_ref,
                   dt_ref, xb_ref, tail_ref, *, tiles_per_seq):
    tile = x_ref.shape[0]
    d_inner = xs_ref.shape[1]
    conv_dim = tail_ref.shape[1]
    tail = SUBLANES

    @pl.when(pl.program_id(0) % tiles_per_seq == 0)
    def _():
        tail_ref[...] = jnp.zeros(tail_ref.shape, F32)

    h = _rms(x_ref[...], g_ref[...], NORM_EPS).astype(BF16)
    dt_ref[...] = _dot(h, wd_ref[...])

    for gi, lo in enumerate(range(0, conv_dim, CONV_COLS)):
        cs = slice(lo, lo + CONV_COLS)
        stage = xb_ref.at[gi % 2]
        stage[0:tail, :] = tail_ref[:, cs]
        stage[tail:tail + tile, :] = _dot(h, wx_ref[:, cs])
        if lo < d_inner:
            z_ref[:, cs] = _dot(h, wz_ref[:, cs])
        u = stage[...]
        tail_ref[:, cs] = u[tile:tile + tail, :]
        acc = cb_ref[:, cs] + u[tail:, :] * cw_ref[CONV_K - 1:CONV_K, cs]
        for back in range(1, CONV_K):
            acc = acc + pltpu.roll(u, back, 0)[tail:, :] * cw_ref[CONV_K - 1 - back:CONV_K - back, cs]
        act = acc * jax.nn.sigmoid(acc)
        if lo < d_inner:
            xs_ref[:, cs] = act
        else:
            bc_ref[:, lo - d_inner:lo - d_inner + CONV_COLS] = act.astype(BF16)


def _inproj(x, g, wz, wx, wd, cw, cb, tile, seq):
    T, D = x.shape
    nz, nx, nd = wz.shape[1], wx.shape[1], wd.shape[1]
    assert nz % CONV_COLS == 0 and nx % CONV_COLS == 0 and seq % tile == 0
    tok = lambda w: pl.BlockSpec((tile, w), lambda i: (i, 0))
    kern = functools.partial(_inproj_kernel, tiles_per_seq=seq // tile)
    return pl.pallas_call(
        kern,
        grid=(T // tile,),
        in_specs=[tok(D), _const_spec((1, D)), _const_spec(wz.shape), _const_spec(wx.shape),
                  _const_spec(wd.shape), _const_spec(cw.shape), _const_spec(cb.shape)],
        out_specs=[tok(nz), tok(nz), tok(nx - nz), tok(nd)],
        out_shape=[jax.ShapeDtypeStruct((T, nz), F32), jax.ShapeDtypeStruct((T, nz), F32),
                   jax.ShapeDtypeStruct((T, nx - nz), BF16), jax.ShapeDtypeStruct((T, nd), F32)],
        scratch_shapes=[pltpu.VMEM((2, tile + SUBLANES, CONV_COLS), F32),
                        pltpu.VMEM((SUBLANES, nx), F32)],
        compiler_params=_params("arbitrary"),
        name="ssm_inproj",
    )(x, g, wz, wx, wd, cw, cb)


def _ssd_kernel(z_ref, xs_ref, bc_ref, dtr_ref, dtb_ref, alog_ref, dexp_ref, gn_ref,
                expand_ref, tril_ref, y_ref, st_ref, *, d_inner, n_heads):
    L = SSD_CHUNK
    gw = d_inner // SSM_GROUPS
    hpg = n_heads // SSM_GROUPS
    gn_state = SSM_GROUPS * SSM_STATE

    @pl.when(pl.program_id(1) == 0)
    def _():
        st_ref[...] = jnp.zeros(st_ref.shape, F32)

    dt_in = dtr_ref[...] + dtb_ref[...]
    dt = jnp.maximum(dt_in, 0.0) + jnp.log1p(jnp.exp(-jnp.abs(dt_in)))
    a = dt * (-jnp.exp(alog_ref[...]))
    tril = tril_ref[...]
    expand = expand_ref[...]
    a_cs = _dot(tril, a, precision=_HI)
    a_cs_t = a_cs.T
    a_cs_x = _dot(a_cs, expand, precision=_HI)
    dt_x = _dot(dt, expand, precision=_HI)
    a_last_x = a_cs_x[L - 1:L, :]
    causal = tril > 0.5

    for g in range(SSM_GROUPS):
        lo = g * gw
        xs = xs_ref[:, lo:lo + gw]
        bm = bc_ref[:, g * SSM_STATE:(g + 1) * SSM_STATE]
        cm = bc_ref[:, gn_state + g * SSM_STATE:gn_state + (g + 1) * SSM_STATE]
        xdt = xs * dt_x[:, lo:lo + gw]
        cb = _dot(cm, bm, _NT)
        yd = []
        for r in range(hpg):
            hh = g * hpg + r
            seg = a_cs[:, hh:hh + 1] - a_cs_t[hh:hh + 1, :]
            lmat = jnp.exp(jnp.where(causal, seg, -jnp.inf))
            m = (cb * lmat).astype(BF16)
            yd.append(_dot(m, xdt[:, r * SSM_HEADDIM:(r + 1) * SSM_HEADDIM].astype(BF16)))
        y = jnp.concatenate(yd, axis=1)
        acx = a_cs_x[:, lo:lo + gw]
        alx = a_last_x[:, lo:lo + gw]
        prev = st_ref[g]
        y = y + _dot(cm, prev.astype(BF16)) * jnp.exp(acx)
        xd = (xdt * jnp.exp(alx - acx)).astype(BF16)
        st_ref[g] = prev * jnp.exp(alx) + _dot(bm, xd, _TN)
        y = y + xs * dexp_ref[:, lo:lo + gw]
        zg = z_ref[:, lo:lo + gw]
        y = y * (zg * jax.nn.sigmoid(zg))
        y = y * lax.rsqrt(jnp.mean(y * y, axis=-1, keepdims=True) + GATED_NORM_EPS)
        y_ref[:, lo:lo + gw] = (y * gn_ref[:, lo:lo + gw]).astype(y_ref.dtype)


def _ssd(z, xs, bc, dtr, dtb, alog, dexp, gn, expand, tril, batch):
    T, d_inner = z.shape
    H = dtr.shape[1]
    L = SSD_CHUNK
    nc = T // batch // L
    row = lambda b, c: (b * nc + c, 0)
    kern = functools.partial(_ssd_kernel, d_inner=d_inner, n_heads=H)
    return pl.pallas_call(
        kern,
        grid=(batch, nc),
        in_specs=[pl.BlockSpec((L, d_inner), row), pl.BlockSpec((L, d_inner), row),
                  pl.BlockSpec((L, bc.shape[1]), row), pl.BlockSpec((L, H), row),
                  _const_spec(dtb.shape), _const_spec(alog.shape), _const_spec(dexp.shape),
                  _const_spec(gn.shape), _const_spec(expand.shape), _const_spec(tril.shape)],
        out_specs=pl.BlockSpec((L, d_inner), row),
        out_shape=jax.ShapeDtypeStruct((T, d_inner), BF16),
        scratch_shapes=[pltpu.VMEM((SSM_GROUPS, SSM_STATE, d_inner // SSM_GROUPS), F32)],
        compiler_params=_params("arbitrary", "arbitrary"),
        name="ssd_scan",
    )(z, xs, bc, dtr, dtb, alog, dexp, gn, expand, tril)


def _outproj_kernel(x_ref, y_ref, w_ref, o_ref):
    o_ref[...] = x_ref[...] + _dot(y_ref[...], w_ref[...])


def _outproj(x, y, w, tile):
    T, D = x.shape
    K = y.shape[1]
    return pl.pallas_call(
        _outproj_kernel,
        grid=(T // tile,),
        in_specs=[pl.BlockSpec((tile, D), lambda i: (i, 0)),
                  pl.BlockSpec((tile, K), lambda i: (i, 0)), _const_spec(w.shape)],
        out_specs=pl.BlockSpec((tile, D), lambda i: (i, 0)),
        out_shape=jax.ShapeDtypeStruct((T, D), F32),
        compiler_params=_params("arbitrary"),
        name="ssm_outproj",
    )(x, y, w)


def _peer_kernel(x_ref, g_ref, wq_ref, keys_ref, u0_ref, un_ref, vt_ref, o_ref,
                 hb_ref, e1_ref, e2_ref, gmin_ref, top_ref,
                 a0_ref, a1_ref, p0_ref, p1_ref, acca_ref, accb_ref):
    g = pl.program_id(1)
    last = pl.num_programs(1) - 1
    tt = x_ref.shape[0]
    ec = un_ref.shape[0] // 2
    rows_per_chunk = ec // N_KEYS
    mxu_cols = 2 * LANES
    n_piece = tt // mxu_cols
    half = N_KEYS
    ntop = PEER_TOPK + 1
    neg_inf = -jnp.inf

    @pl.when(g == 0)
    def _route():
        hb = _rms(x_ref[...], g_ref[...], NORM_EPS).T.astype(BF16)
        hb_ref[...] = hb
        for h in range(PEER_HEADS):
            q = _dot(wq_ref[h * 2 * half:(h + 1) * 2 * half, :], hb).astype(BF16)
            a0_ref[h * N_KEYS:(h + 1) * N_KEYS, :] = _dot(keys_ref[0], q[0:half, :])
            a1_ref[h * N_KEYS:(h + 1) * N_KEYS, :] = _dot(keys_ref[1], q[half:2 * half, :])

        def top_values(s_ref, h, cs):
            v = [s_ref[h * N_KEYS + r * SUBLANES:h * N_KEYS + (r + 1) * SUBLANES, cs]
                 for r in range(N_KEYS // SUBLANES)]
            for i, j in _SORT_PAIRS:
                v[i], v[j] = jnp.maximum(v[i], v[j]), jnp.minimum(v[i], v[j])
            depth = len(v)
            out = []
            for k in range(ntop):
                m = v[0]
                for shift in (4, 2, 1):
                    m = jnp.maximum(m, pltpu.roll(m, shift, 0))
                out.append(m)
                hit = v[0] == m
                for r in range(min(depth, ntop - 1 - k)):
                    v[r] = jnp.where(hit, v[r + 1] if r + 1 < depth else neg_inf, v[r])
            return out

        for tc in range(tt // LANES):
            cs = slice(tc * LANES, (tc + 1) * LANES)
            for c, s_ref in enumerate((a0_ref, a1_ref)):
                for h in range(PEER_HEADS):
                    for r, m in enumerate(top_values(s_ref, h, cs)):
                        top_ref[c, r, h:h + 1, cs] = m[0:1, :]

            a = [top_ref[0, r, :, cs] for r in range(ntop)]
            b = [top_ref[1, r, :, cs] for r in range(ntop)]
            cand = [a[r1] + b[r2] for r1, r2 in _CAND_PAIRS]
            a0, b0 = a[0], b[0]
            m0 = a0 + b0
            z = jnp.zeros(m0.shape, F32)
            for k in range(ntop):
                m = functools.reduce(jnp.maximum, cand)
                if k < PEER_TOPK:
                    z = z + jnp.exp(m - m0)
                if k == PEER_TOPK - 1:
                    v16 = m
                if k < ntop - 1:
                    cand = [jnp.where(cv == m, neg_inf, cv) for cv in cand]
            v17 = m
            zinv = 1.0 / z
            gmin = jnp.exp(0.5 * (v16 + v17) - m0) * zinv
            for h in range(PEER_HEADS):
                hk = slice(h * N_KEYS, (h + 1) * N_KEYS)
                e1_ref[h, tc] = jnp.exp(a0_ref[hk, cs] - a0[h:h + 1, :])
                e2_ref[h, tc] = jnp.exp(a1_ref[hk, cs] - b0[h:h + 1, :]) * zinv[h:h + 1, :]
                gmin_ref[h, tc] = gmin[h:h + 1, :]

            if (tc + 1) % (mxu_cols // LANES) == 0:
                ps = slice((tc + 1) * LANES - mxu_cols, (tc + 1) * LANES)
                a0_ref[:, ps] = _dot(u0_ref[0:ec, :], hb_ref[:, ps])
                a1_ref[:, ps] = _dot(u0_ref[ec:2 * ec, :], hb_ref[:, ps])

        acca_ref[...] = jnp.zeros(acca_ref.shape, F32)
        accb_ref[...] = jnp.zeros(accb_ref.shape, F32)

    def gate_gelu(a_ref, p_ref, chunk, tc):
        i0 = pl.multiple_of(chunk * rows_per_chunk, SUBLANES)
        cs = slice(tc * LANES, (tc + 1) * LANES)
        e18 = [e1_ref[h, tc, pl.ds(i0, rows_per_chunk), :] for h in range(PEER_HEADS)]
        gm = [gmin_ref[h, tc] for h in range(PEER_HEADS)]
        for ii in range(rows_per_chunk):
            rows = slice(ii * N_KEYS, (ii + 1) * N_KEYS)
            w = jnp.zeros((N_KEYS, LANES), F32)
            for h in range(PEER_HEADS):
                gate = e18[h][ii:ii + 1, :] * e2_ref[h, tc]
                w = w + jnp.where(gate >= gm[h], gate, 0.0)
            av = a_ref[rows, cs]
            gelu = 0.5 * av * (1.0 + lax.erf(av * np.float32(math.sqrt(0.5))))
            p_ref[rows, cs] = (w * gelu).astype(BF16)

    for par, (a_ref, p_ref, acc_ref) in enumerate(((a0_ref, p0_ref, acca_ref),
                                                   (a1_ref, p1_ref, accb_ref))):
        es = slice(par * ec, (par + 1) * ec)
        for piece in range(n_piece):
            cs = slice(piece * mxu_cols, (piece + 1) * mxu_cols)
            for tc in range(piece * (mxu_cols // LANES), (piece + 1) * (mxu_cols // LANES)):
                gate_gelu(a_ref, p_ref, 2 * g + par, tc)
            acc_ref[:, cs] += _dot(vt_ref[:, es], p_ref[:, cs])
            a_ref[:, cs] = _dot(un_ref[es, :], hb_ref[:, cs])

    @pl.when(g == last)
    def _finish():
        o_ref[...] = x_ref[...] + (acca_ref[...] + accb_ref[...]).T


def _peer(x, g, wq_t, keys, u, vt, tile, ec):
    T, D = x.shape
    E = u.shape[0]
    nchunk = E // ec
    assert E == N_KEYS * N_KEYS and ec == SUBLANES * N_KEYS and tile % LANES == 0
    assert ec == PEER_HEADS * N_KEYS
    assert nchunk % 2 == 0
    ntc = tile // LANES
    nstep = nchunk // 2
    return pl.pallas_call(
        _peer_kernel,
        grid=(T // tile, nstep),
        in_specs=[pl.BlockSpec((tile, D), lambda i, s: (i, 0)),
                  _const_spec((1, D)), _const_spec(wq_t.shape), _const_spec(keys.shape),
                  pl.BlockSpec((2 * ec, D), lambda i, s: (0, 0), pipeline_mode=pl.Buffered(1)),
                  pl.BlockSpec((2 * ec, D), lambda i, s: (jnp.minimum(s + 1, nstep - 1), 0)),
                  pl.BlockSpec((D, 2 * ec), lambda i, s: (0, s))],
        out_specs=pl.BlockSpec((tile, D), lambda i, s: (i, 0)),
        out_shape=jax.ShapeDtypeStruct((T, D), F32),
        scratch_shapes=[pltpu.VMEM((D, tile), BF16),
                        pltpu.VMEM((PEER_HEADS, ntc, N_KEYS, LANES), F32),
                        pltpu.VMEM((PEER_HEADS, ntc, N_KEYS, LANES), F32),
                        pltpu.VMEM((PEER_HEADS, ntc, 1, LANES), F32),
                        pltpu.VMEM((2, PEER_TOPK + 1, PEER_HEADS, tile), F32),
                        pltpu.VMEM((ec, tile), F32), pltpu.VMEM((ec, tile), F32),
                        pltpu.VMEM((ec, tile), BF16), pltpu.VMEM((ec, tile), BF16),
                        pltpu.VMEM((D, tile), F32), pltpu.VMEM((D, tile), F32)],
        compiler_params=_params("arbitrary", "arbitrary"),
        name="peer",
    )(x, g, wq_t, keys, u, u, vt)


def _rope_tables(pos_ref, inv_ref):
    ang = pos_ref[...].astype(F32) * inv_ref[...]
    lane = lax.broadcasted_iota(jnp.int32, ang.shape, 1) % HEAD_DIM
    cos = jnp.cos(ang)
    sin = jnp.sin(ang)
    half = ROT_DIM // 2
    sin_lo = jnp.where(lane < half, -sin, 0.0)
    sin_hi = jnp.where((lane >= half) & (lane < ROT_DIM), sin, 0.0)
    return cos, sin_lo, sin_hi


def _rope_apply(t, cos, sin_lo, sin_hi):
    half = ROT_DIM // 2
    up = pltpu.roll(t, LANES - half, 1)
    dn = pltpu.roll(t, half, 1)
    return t * cos + up * sin_lo + dn * sin_hi


def _ple_core(x_ref, p_ref, g_ref, proj_ref, gw_ref):
    x = x_ref[...]
    hn = _rms(x, g_ref[...], NORM_EPS).astype(BF16)
    gate = jax.nn.sigmoid(_dot(hn, gw_ref[...]))
    return x + _dot(p_ref[...].astype(BF16), proj_ref[...]) * gate


def _ple_kv_kernel(x_ref, p_ref, g_ref, proj_ref, gw_ref, kvg_ref, kvw_ref, kvb_ref, pos_ref,
                   inv_ref, o_ref, k_ref, v_ref):
    x2 = _ple_core(x_ref, p_ref, g_ref, proj_ref, gw_ref)
    o_ref[...] = x2
    kv = _dot(_rms(x2, kvg_ref[...], NORM_EPS).astype(BF16), kvw_ref[...]) + kvb_ref[...]
    kvd = k_ref.shape[1]
    cos, sin_lo, sin_hi = _rope_tables(pos_ref, inv_ref)
    k_ref[...] = _rope_apply(kv[:, :kvd], cos, sin_lo, sin_hi).astype(k_ref.dtype)
    v_ref[...] = kv[:, kvd:].astype(v_ref.dtype)


def _ple_final_kernel(x_ref, p_ref, g_ref, proj_ref, gw_ref, fg_ref, o_ref):
    x2 = _ple_core(x_ref, p_ref, g_ref, proj_ref, gw_ref)
    o_ref[...] = _rms(x2, fg_ref[...], NORM_EPS)


def _ple_kv(x, p, g, proj, gw, kvg, kvw, kvb, pos, inv, tile):
    T, D = x.shape
    P = p.shape[1]
    kvd = kvw.shape[1] // 2
    tok = lambda w: pl.BlockSpec((tile, w), lambda i: (i, 0))
    return pl.pallas_call(
        _ple_kv_kernel,
        grid=(T // tile,),
        in_specs=[tok(D), tok(P), _const_spec((1, D)), _const_spec(proj.shape),
                  _const_spec(gw.shape), _const_spec((1, D)), _const_spec(kvw.shape),
                  _const_spec(kvb.shape), tok(1), _const_spec(inv.shape)],
        out_specs=[tok(D), tok(kvd), tok(kvd)],
        out_shape=[jax.ShapeDtypeStruct((T, D), F32), jax.ShapeDtypeStruct((T, kvd), BF16),
                   jax.ShapeDtypeStruct((T, kvd), BF16)],
        compiler_params=_params("arbitrary"),
        name="ple_kv",
    )(x, p, g, proj, gw, kvg, kvw, kvb, pos, inv)


def _ple_final(x, p, g, proj, gw, fg, tile):
    T, D = x.shape
    P = p.shape[1]
    tok = lambda w: pl.BlockSpec((tile, w), lambda i: (i, 0))
    return pl.pallas_call(
        _ple_final_kernel,
        grid=(T // tile,),
        in_specs=[tok(D), tok(P), _const_spec((1, D)), _const_spec(proj.shape),
                  _const_spec(gw.shape), _const_spec((1, D))],
        out_specs=tok(D),
        out_shape=jax.ShapeDtypeStruct((T, D), F32),
        compiler_params=_params("arbitrary"),
        name="ple_final",
    )(x, p, g, proj, gw, fg)


def _attn_kernel(x_ref, g_ref, qw_ref, qb_ref, sink_ref, owt_ref, ob_ref, kc_ref, kp_ref, vc_ref,
                 vp_ref, pos_ref, invc_ref, o_ref, qt_ref, att_ref, *, n_q_heads):
    tile = x_ref.shape[0]
    W = WINDOW
    nblk = tile // W
    q_per_kv = n_q_heads // N_KV_HEADS
    half = ROT_DIM // 2
    scale = HEAD_DIM ** -0.5
    first = pl.program_id(1) == 0

    h = _rms(x_ref[...], g_ref[...], NORM_EPS).astype(BF16)
    q = (_dot(h, qw_ref[...]) + qb_ref[...]) * scale
    qt = q.T
    ang = invc_ref[...] * pos_ref[...].astype(F32)
    cos, sin = jnp.cos(ang), jnp.sin(ang)
    for hh in range(n_q_heads):
        base = hh * HEAD_DIM
        t1 = qt[base:base + half, :]
        t2 = qt[base + half:base + ROT_DIM, :]
        qt_ref[base:base + half, :] = (t1 * cos - t2 * sin).astype(BF16)
        qt_ref[base + half:base + ROT_DIM, :] = (t2 * cos + t1 * sin).astype(BF16)
        qt_ref[base + ROT_DIM:base + HEAD_DIM, :] = qt[base + ROT_DIM:base + HEAD_DIM, :].astype(BF16)

    kj = lax.broadcasted_iota(jnp.int32, (2 * W, W), 0)
    qi = lax.broadcasted_iota(jnp.int32, (2 * W, W), 1)
    first_lo = jnp.where(first, W, 0)
    band = jnp.where((kj > qi) & (kj <= qi + W), 0.0, -jnp.inf)
    band_first = jnp.where(kj >= first_lo, band, -jnp.inf)
    band = jnp.concatenate([band] * q_per_kv, axis=1)
    band_first = jnp.concatenate([band_first] * q_per_kv, axis=1)

    for n in range(nblk):
        cols = slice(n * W, (n + 1) * W)
        if n == 0:
            kprev, vprev = kp_ref[...], vp_ref[...]
        else:
            kprev, vprev = kc_ref[(n - 1) * W:n * W, :], vc_ref[(n - 1) * W:n * W, :]
        kblk = jnp.concatenate([kprev, kc_ref[cols, :]], axis=0)
        vblk = jnp.concatenate([vprev, vc_ref[cols, :]], axis=0)
        bias = band_first if n == 0 else band
        for g in range(N_KV_HEADS):
            ks = kblk[:, g * HEAD_DIM:(g + 1) * HEAD_DIM]
            vs = vblk[:, g * HEAD_DIM:(g + 1) * HEAD_DIM]
            heads = range(g * q_per_kv, (g + 1) * q_per_kv)
            qs = jnp.concatenate([qt_ref[hh * HEAD_DIM:(hh + 1) * HEAD_DIM, cols] for hh in heads],
                                 axis=1)
            sink = jnp.concatenate([jnp.broadcast_to(sink_ref[:, hh:hh + 1], (1, W)) for hh in heads],
                                   axis=1)
            s = _dot(ks, qs) + bias
            m = jnp.maximum(jnp.max(s, axis=0, keepdims=True), sink)
            e = jnp.exp(s - m)
            denom = jnp.sum(e, axis=0, keepdims=True) + jnp.exp(sink - m)
            o = _dot(vs, e.astype(BF16), _TN) / denom
            for r, hh in enumerate(heads):
                att_ref[hh * HEAD_DIM:(hh + 1) * HEAD_DIM, cols] = o[:, r * W:(r + 1) * W].astype(BF16)

    o_ref[...] = x_ref[...] + _dot(owt_ref[...], att_ref[...]).T + ob_ref[...]


def _attn(x, g, qw, qb, sinks, ow_t, ob, k, v, pos, inv_col, batch, tile):
    T, D = x.shape
    kvd = k.shape[1]
    nq = qw.shape[1] // HEAD_DIM
    nt = T // batch // tile
    bpt = tile // WINDOW
    row = lambda b, i: (b * nt + i, 0)
    prev = lambda b, i: (jnp.maximum((b * nt + i) * bpt - 1, 0), 0)
    kern = functools.partial(_attn_kernel, n_q_heads=nq)
    return pl.pallas_call(
        kern,
        grid=(batch, nt),
        in_specs=[pl.BlockSpec((tile, D), row), _const_spec((1, D)), _const_spec(qw.shape),
                  _const_spec(qb.shape), _const_spec(sinks.shape), _const_spec(ow_t.shape),
                  _const_spec(ob.shape),
                  pl.BlockSpec((tile, kvd), row), pl.BlockSpec((WINDOW, kvd), prev),
                  pl.BlockSpec((tile, kvd), row), pl.BlockSpec((WINDOW, kvd), prev),
                  pl.BlockSpec((None, 1, tile), lambda b, i: (b * nt + i, 0, 0)),
                  _const_spec(inv_col.shape)],
        out_specs=pl.BlockSpec((tile, D), row),
        out_shape=jax.ShapeDtypeStruct((T, D), F32),
        scratch_shapes=[pltpu.VMEM((qw.shape[1], tile), BF16),
                        pltpu.VMEM((qw.shape[1], tile), BF16)],
        compiler_params=_params("arbitrary", "arbitrary"),
        name="swa_attn",
    )(x, g, qw, qb, sinks, ow_t, ob, k, k, v, v, pos.reshape(T // tile, 1, tile), inv_col)


def _row(v):
    return v.reshape(1, -1)


def kernel(x, p, positions, ssm_norm, ssm_in_w, ssm_conv_w, ssm_conv_b, ssm_dt_bias, ssm_A_log, ssm_D, ssm_gate_norm, ssm_out_w, kv_norm, kv_w, kv_b, attn_norm, q_w, q_b, sinks, o_w, o_b, peer_norm, peer_q_w, peer_sub_keys, peer_u, peer_v, ple_norm, ple_proj, ple_gate_w, final_norm):
    B, S, D = x.shape
    T = B * S
    depth = p.shape[0]
    n_a = ssm_norm.shape[0]
    H = ssm_D.shape[1]
    d_inner = H * SSM_HEADDIM
    conv_dim = ssm_conv_w.shape[2]

    xt = x.reshape(T, D)
    pos = positions.reshape(T, 1)
    lane = np.arange(LANES) % HEAD_DIM
    inv = np.where(lane < ROT_DIM,
                   ROPE_THETA ** (-(2.0 * (lane % (ROT_DIM // 2))) / ROT_DIM), 0.0)
    inv = jnp.asarray(inv.reshape(1, LANES), F32)
    inv_col = inv[0, :ROT_DIM // 2].reshape(-1, 1)
    expand = jnp.asarray(np.repeat(np.eye(H, dtype=np.float32), SSM_HEADDIM, axis=1))
    tril = jnp.asarray(np.tril(np.ones((SSD_CHUNK, SSD_CHUNK), np.float32)))

    k_sh = v_sh = None
    for i in range(depth):
        if i < n_a:
            w = ssm_in_w[i].astype(BF16)
            z, xs, bc, dtr = _inproj(xt, _row(ssm_norm[i]), w[:, :d_inner],
                                     w[:, d_inner:d_inner + conv_dim], w[:, d_inner + conv_dim:],
                                     ssm_conv_w[i], _row(ssm_conv_b[i]), tile=256, seq=S)
            y = _ssd(z, xs, bc, dtr, _row(ssm_dt_bias[i]), _row(ssm_A_log[i]),
                     _row(jnp.repeat(ssm_D[i], SSM_HEADDIM)), _row(ssm_gate_norm[i]), expand, tril,
                     batch=B)
            xt = _outproj(xt, y, ssm_out_w[i].astype(BF16), tile=512)
        else:
            j = i - n_a
            xt = _attn(xt, _row(attn_norm[j]), q_w[j].astype(BF16), _row(q_b[j]), _row(sinks[j]),
                       o_w[j].T.astype(BF16), _row(o_b[j]), k_sh, v_sh, pos, inv_col, batch=B,
                       tile=512)
        xt = _peer(xt, _row(peer_norm[i]), peer_q_w[i].T.astype(BF16),
                   peer_sub_keys[i].astype(BF16), peer_u[i].astype(BF16),
                   peer_v[i].T.astype(BF16), tile=512, ec=1024)
        if i == n_a - 1:
            xt, k_sh, v_sh = _ple_kv(xt, p[i].reshape(T, -1), _row(ple_norm[i]),
                                     ple_proj[i].astype(BF16), ple_gate_w[i].astype(BF16),
                                     _row(kv_norm), kv_w.astype(BF16), _row(kv_b), pos, inv,
                                     tile=512)
        elif i == depth - 1:
            xt = _ple_final(xt, p[i].reshape(T, -1), _row(ple_norm[i]), ple_proj[i].astype(BF16),
                            ple_gate_w[i].astype(BF16), _row(final_norm), tile=512)
        else:
            raise NotImplementedError("PLE without K/V or final norm")
    return xt.reshape(B, S, D)
```

```python
import functools
import math

import jax
import jax.numpy as jnp
import numpy as np
from jax import lax
from jax.experimental import pallas as pl
from jax.experimental.pallas import tpu as pltpu

F32 = jnp.float32
BF16 = jnp.bfloat16

NORM_EPS = 1e-6
GATED_NORM_EPS = 1e-5
SSM_HEADDIM = 64
SSM_GROUPS = 8
SSM_STATE = 128
CONV_K = 4
SSD_CHUNK = 128
CONV_COLS = 512
HEAD_DIM = 64
N_KV_HEADS = 2
WINDOW = 128
ROT_DIM = HEAD_DIM // 4
ROPE_THETA = 500000.0
PEER_HEADS = 8
N_KEYS = 128
PEER_TOPK = 16

LANES = 128
SUBLANES = 8
VMEM_LIMIT = 56 * 1024 * 1024

_CAND_PAIRS = [(r1, r2) for r1 in range(PEER_TOPK + 1) for r2 in range(PEER_TOPK + 1)
               if (r1 + 1) * (r2 + 1) <= PEER_TOPK + 1]


def _sort_network(n):
    pairs = []
    p = 1
    while p < n:
        k = p
        while k >= 1:
            for j in range(k % p, n - k, 2 * k):
                for i in range(min(k, n - j - k)):
                    if (i + j) // (2 * p) == (i + j + k) // (2 * p):
                        pairs.append((i + j, i + j + k))
            k //= 2
        p *= 2
    return pairs


_SORT_PAIRS = _sort_network(N_KEYS // SUBLANES)


def _params(*sem):
    return pltpu.CompilerParams(dimension_semantics=sem, vmem_limit_bytes=VMEM_LIMIT)


def _const_spec(shape):
    nd = len(shape)
    return pl.BlockSpec(shape, lambda *_: (0,) * nd, pipeline_mode=pl.Buffered(1))


def _rms(x, g, eps):
    return x * lax.rsqrt(jnp.mean(x * x, axis=-1, keepdims=True) + eps) * g


def _dot(a, b, dims=None, precision=None):
    if dims is None:
        dims = (((a.ndim - 1,), (0,)), ((), ()))
    return lax.dot_general(a, b, dims, precision=precision, preferred_element_type=F32)


_NT = (((1,), (1,)), ((), ()))
_TN = (((0,), (0,)), ((), ()))
_HI = lax.Precision.HIGHEST


def _inproj_kernel(x_ref, g_ref, w_ref, cw_ref, cb_ref, z_ref, xs_ref, bc_ref,
                   dt_ref, xb_ref, tail_ref, *, tiles_per_seq):
    tile = x_ref.shape[0]
    d_inner = xs_ref.shape[1]
    conv_dim = tail_ref.shape[1]
    tail = SUBLANES

    @pl.when(pl.program_id(0) % tiles_per_seq == 0)
    def _():
        tail_ref[...] = jnp.zeros(tail_ref.shape, F32)

    h = _rms(x_ref[...], g_ref[...], NORM_EPS).astype(BF16)
    dt_ref[...] = _dot(h, w_ref[:, d_inner + conv_dim:])

    for gi, lo in enumerate(range(0, conv_dim, CONV_COLS)):
        cs = slice(lo, lo + CONV_COLS)
        stage = xb_ref.at[gi % 2]
        stage[0:tail, :] = tail_ref[:, cs]
        stage[tail:tail + tile, :] = _dot(h, w_ref[:, d_inner + lo:d_inner + lo + CONV_COLS])
        if lo < d_inner:
            z_ref[:, cs] = _dot(h, w_ref[:, cs])
        u = stage[...]
        tail_ref[:, cs] = u[tile:tile + tail, :]
        acc = cb_ref[:, cs] + u[tail:, :] * cw_ref[CONV_K - 1:CONV_K, cs]
        for back in range(1, CONV_K):
            acc = acc + pltpu.roll(u, back, 0)[tail:, :] * cw_ref[CONV_K - 1 - back:CONV_K - back, cs]
        act = acc * jax.nn.sigmoid(acc)
        if lo < d_inner:
            xs_ref[:, cs] = act
        else:
            bc_ref[:, lo - d_inner:lo - d_inner + CONV_COLS] = act.astype(BF16)


def _inproj(x, g, w, cw, cb, d_inner, tile, seq):
    T, D = x.shape
    nz, nx = d_inner, cw.shape[1]
    nd = w.shape[1] - nz - nx
    assert nz % CONV_COLS == 0 and nx % CONV_COLS == 0 and seq % tile == 0
    tok = lambda w: pl.BlockSpec((tile, w), lambda i: (i, 0))
    kern = functools.partial(_inproj_kernel, tiles_per_seq=seq // tile)
    return pl.pallas_call(
        kern,
        grid=(T // tile,),
        in_specs=[tok(D), _const_spec((1, D)), _const_spec(w.shape), _const_spec(cw.shape),
                  _const_spec(cb.shape)],
        out_specs=[tok(nz), tok(nz), tok(nx - nz), tok(nd)],
        out_shape=[jax.ShapeDtypeStruct((T, nz), F32), jax.ShapeDtypeStruct((T, nz), F32),
                   jax.ShapeDtypeStruct((T, nx - nz), BF16), jax.ShapeDtypeStruct((T, nd), F32)],
        scratch_shapes=[pltpu.VMEM((2, tile + SUBLANES, CONV_COLS), F32),
                        pltpu.VMEM((SUBLANES, nx), F32)],
        compiler_params=_params("arbitrary"),
        name="ssm_inproj",
    )(x, g, w, cw, cb)


def _ssd_kernel(z_ref, xs_ref, bc_ref, dtr_ref, dtb_ref, alog_ref, dexp_ref, gn_ref,
                expand_ref, tril_ref, y_ref, st_ref, *, d_inner, n_heads):
    L = SSD_CHUNK
    gw = d_inner // SSM_GROUPS
    hpg = n_heads // SSM_GROUPS
    gn_state = SSM_GROUPS * SSM_STATE

    @pl.when(pl.program_id(1) == 0)
    def _():
        st_ref[...] = jnp.zeros(st_ref.shape, F32)

    dt_in = dtr_ref[...] + dtb_ref[...]
    dt = jnp.maximum(dt_in, 0.0) + jnp.log1p(jnp.exp(-jnp.abs(dt_in)))
    a = dt * (-jnp.exp(alog_ref[...]))
    tril = tril_ref[...]
    expand = expand_ref[...]
    a_cs = _dot(tril, a, precision=_HI)
    a_cs_t = a_cs.T
    a_cs_x = _dot(a_cs, expand, precision=_HI)
    dt_x = _dot(dt, expand, precision=_HI)
    a_last_x = a_cs_x[L - 1:L, :]
    causal = tril > 0.5

    for g in range(SSM_GROUPS):
        lo = g * gw
        xs = xs_ref[:, lo:lo + gw]
        bm = bc_ref[:, g * SSM_STATE:(g + 1) * SSM_STATE]
        cm = bc_ref[:, gn_state + g * SSM_STATE:gn_state + (g + 1) * SSM_STATE]
        xdt = xs * dt_x[:, lo:lo + gw]
        cb = _dot(cm, bm, _NT)
        yd = []
        for r in range(hpg):
            hh = g * hpg + r
            seg = a_cs[:, hh:hh + 1] - a_cs_t[hh:hh + 1, :]
            lmat = jnp.exp(jnp.where(causal, seg, -jnp.inf))
            m = (cb * lmat).astype(BF16)
            yd.append(_dot(m, xdt[:, r * SSM_HEADDIM:(r + 1) * SSM_HEADDIM].astype(BF16)))
        y = jnp.concatenate(yd, axis=1)
        acx = a_cs_x[:, lo:lo + gw]
        alx = a_last_x[:, lo:lo + gw]
        prev = st_ref[g]
        y = y + _dot(cm, prev.astype(BF16)) * jnp.exp(acx)
        xd = (xdt * jnp.exp(alx - acx)).astype(BF16)
        st_ref[g] = prev * jnp.exp(alx) + _dot(bm, xd, _TN)
        y = y + xs * dexp_ref[:, lo:lo + gw]
        zg = z_ref[:, lo:lo + gw]
        y = y * (zg * jax.nn.sigmoid(zg))
        y = y * lax.rsqrt(jnp.mean(y * y, axis=-1, keepdims=True) + GATED_NORM_EPS)
        y_ref[:, lo:lo + gw] = (y * gn_ref[:, lo:lo + gw]).astype(y_ref.dtype)


def _ssd(z, xs, bc, dtr, dtb, alog, dexp, gn, expand, tril, batch):
    T, d_inner = z.shape
    H = dtr.shape[1]
    L = SSD_CHUNK
    nc = T // batch // L
    row = lambda b, c: (b * nc + c, 0)
    kern = functools.partial(_ssd_kernel, d_inner=d_inner, n_heads=H)
    return pl.pallas_call(
        kern,
        grid=(batch, nc),
        in_specs=[pl.BlockSpec((L, d_inner), row), pl.BlockSpec((L, d_inner), row),
                  pl.BlockSpec((L, bc.shape[1]), row), pl.BlockSpec((L, H), row),
                  _const_spec(dtb.shape), _const_spec(alog.shape), _const_spec(dexp.shape),
                  _const_spec(gn.shape), _const_spec(expand.shape), _const_spec(tril.shape)],
        out_specs=pl.BlockSpec((L, d_inner), row),
        out_shape=jax.ShapeDtypeStruct((T, d_inner), BF16),
        scratch_shapes=[pltpu.VMEM((SSM_GROUPS, SSM_STATE, d_inner // SSM_GROUPS), F32)],
        compiler_params=_params("arbitrary", "arbitrary"),
        name="ssd_scan",
    )(z, xs, bc, dtr, dtb, alog, dexp, gn, expand, tril)


def _outproj_kernel(x_ref, y_ref, w_ref, o_ref):
    o_ref[...] = x_ref[...] + _dot(y_ref[...], w_ref[...])


def _outproj(x, y, w, tile):
    T, D = x.shape
    K = y.shape[1]
    return pl.pallas_call(
        _outproj_kernel,
        grid=(T // tile,),
        in_specs=[pl.BlockSpec((tile, D), lambda i: (i, 0)),
                  pl.BlockSpec((tile, K), lambda i: (i, 0)), _const_spec(w.shape)],
        out_specs=pl.BlockSpec((tile, D), lambda i: (i, 0)),
        out_shape=jax.ShapeDtypeStruct((T, D), F32),
        compiler_params=_params("arbitrary"),
        name="ssm_outproj",
    )(x, y, w)


def _peer_kernel(x_ref, g_ref, wq_ref, keys_ref, u0_ref, un_ref, vt_ref, o_ref,
                 hb_ref, e1_ref, e2_ref, gmin_ref, top_ref,
                 a0_ref, a1_ref, p0_ref, p1_ref, acca_ref, accb_ref):
    g = pl.program_id(1)
    last = pl.num_programs(1) - 1
    tt = x_ref.shape[0]
    ec = un_ref.shape[0] // 2
    rows_per_chunk = ec // N_KEYS
    mxu_cols = 2 * LANES
    n_piece = tt // mxu_cols
    half = N_KEYS
    ntop = PEER_TOPK + 1
    neg_inf = -jnp.inf

    @pl.when(g == 0)
    def _route():
        hb = _rms(x_ref[...], g_ref[...], NORM_EPS).T.astype(BF16)
        hb_ref[...] = hb
        for h in range(PEER_HEADS):
            q = _dot(wq_ref[h * 2 * half:(h + 1) * 2 * half, :], hb).astype(BF16)
            a0_ref[h * N_KEYS:(h + 1) * N_KEYS, :] = _dot(keys_ref[0], q[0:half, :])
            a1_ref[h * N_KEYS:(h + 1) * N_KEYS, :] = _dot(keys_ref[1], q[half:2 * half, :])

        def top_values(s_ref, h, cs):
            v = [s_ref[h * N_KEYS + r * SUBLANES:h * N_KEYS + (r + 1) * SUBLANES, cs]
                 for r in range(N_KEYS // SUBLANES)]
            for i, j in _SORT_PAIRS:
                v[i], v[j] = jnp.maximum(v[i], v[j]), jnp.minimum(v[i], v[j])
            depth = len(v)
            out = []
            for k in range(ntop):
                m = v[0]
                for shift in (4, 2, 1):
                    m = jnp.maximum(m, pltpu.roll(m, shift, 0))
                out.append(m)
                hit = v[0] == m
                for r in range(min(depth, ntop - 1 - k)):
                    v[r] = jnp.where(hit, v[r + 1] if r + 1 < depth else neg_inf, v[r])
            return out

        for tc in range(tt // LANES):
            cs = slice(tc * LANES, (tc + 1) * LANES)
            for c, s_ref in enumerate((a0_ref, a1_ref)):
                for h in range(PEER_HEADS):
                    for r, m in enumerate(top_values(s_ref, h, cs)):
                        top_ref[c, r, h:h + 1, cs] = m[0:1, :]

            a = [top_ref[0, r, :, cs] for r in range(ntop)]
            b = [top_ref[1, r, :, cs] for r in range(ntop)]
            cand = [a[r1] + b[r2] for r1, r2 in _CAND_PAIRS]
            a0, b0 = a[0], b[0]
            m0 = a0 + b0
            z = jnp.zeros(m0.shape, F32)
            for k in range(ntop):
                m = functools.reduce(jnp.maximum, cand)
                if k < PEER_TOPK:
                    z = z + jnp.exp(m - m0)
                if k == PEER_TOPK - 1:
                    v16 = m
                if k < ntop - 1:
                    cand = [jnp.where(cv == m, neg_inf, cv) for cv in cand]
            v17 = m
            zinv = 1.0 / z
            gmin = jnp.exp(0.5 * (v16 + v17) - m0) * zinv
            for h in range(PEER_HEADS):
                hk = slice(h * N_KEYS, (h + 1) * N_KEYS)
                e1_ref[h, tc] = jnp.exp(a0_ref[hk, cs] - a0[h:h + 1, :])
                e2_ref[h, tc] = jnp.exp(a1_ref[hk, cs] - b0[h:h + 1, :]) * zinv[h:h + 1, :]
                gmin_ref[h, tc] = gmin[h:h + 1, :]

            if (tc + 1) % (mxu_cols // LANES) == 0:
                ps = slice((tc + 1) * LANES - mxu_cols, (tc + 1) * LANES)
                a0_ref[:, ps] = _dot(u0_ref[0:ec, :], hb_ref[:, ps])
                a1_ref[:, ps] = _dot(u0_ref[ec:2 * ec, :], hb_ref[:, ps])

        acca_ref[...] = jnp.zeros(acca_ref.shape, F32)
        accb_ref[...] = jnp.zeros(accb_ref.shape, F32)

    def gate_gelu(a_ref, p_ref, chunk, tc):
        i0 = pl.multiple_of(chunk * rows_per_chunk, SUBLANES)
        cs = slice(tc * LANES, (tc + 1) * LANES)
        e18 = [e1_ref[h, tc, pl.ds(i0, rows_per_chunk), :] for h in range(PEER_HEADS)]
        gm = [gmin_ref[h, tc] for h in range(PEER_HEADS)]
        for ii in range(rows_per_chunk):
            rows = slice(ii * N_KEYS, (ii + 1) * N_KEYS)
            w = jnp.zeros((N_KEYS, LANES), F32)
            for h in range(PEER_HEADS):
                gate = e18[h][ii:ii + 1, :] * e2_ref[h, tc]
                w = w + jnp.where(gate >= gm[h], gate, 0.0)
            av = a_ref[rows, cs]
            gelu = 0.5 * av * (1.0 + lax.erf(av * np.float32(math.sqrt(0.5))))
            p_ref[rows, cs] = (w * gelu).astype(BF16)

    for par, (a_ref, p_ref, acc_ref) in enumerate(((a0_ref, p0_ref, acca_ref),
                                                   (a1_ref, p1_ref, accb_ref))):
        es = slice(par * ec, (par + 1) * ec)
        for piece in range(n_piece):
            cs = slice(piece * mxu_cols, (piece + 1) * mxu_cols)
            for tc in range(piece * (mxu_cols // LANES), (piece + 1) * (mxu_cols // LANES)):
                gate_gelu(a_ref, p_ref, 2 * g + par, tc)
            acc_ref[:, cs] += _dot(vt_ref[:, es], p_ref[:, cs])
            a_ref[:, cs] = _dot(un_ref[es, :], hb_ref[:, cs])

    @pl.when(g == last)
    def _finish():
        o_ref[...] = x_ref[...] + (acca_ref[...] + accb_ref[...]).T


def _peer(x, g, wq_t, keys, u, vt, layer, tile, ec):
    T, D = x.shape
    E = u.shape[1]
    nchunk = E // ec
    assert E == N_KEYS * N_KEYS and ec == SUBLANES * N_KEYS and tile % LANES == 0
    assert ec == PEER_HEADS * N_KEYS
    assert nchunk % 2 == 0
    ntc = tile // LANES
    nstep = nchunk // 2
    return pl.pallas_call(
        _peer_kernel,
        grid=(T // tile, nstep),
        in_specs=[pl.BlockSpec((tile, D), lambda i, s: (i, 0)),
                  _const_spec((1, D)),
                  pl.BlockSpec((None,) + wq_t.shape[1:], lambda i, s: (layer, 0, 0),
                               pipeline_mode=pl.Buffered(1)),
                  pl.BlockSpec((None,) + keys.shape[1:], lambda i, s: (layer, 0, 0, 0),
                               pipeline_mode=pl.Buffered(1)),
                  pl.BlockSpec((None, 2 * ec, D), lambda i, s: (layer, 0, 0),
                               pipeline_mode=pl.Buffered(1)),
                  pl.BlockSpec((None, 2 * ec, D),
                               lambda i, s: (layer, jnp.minimum(s + 1, nstep - 1), 0)),
                  pl.BlockSpec((None, D, 2 * ec), lambda i, s: (layer, 0, s))],
        out_specs=pl.BlockSpec((tile, D), lambda i, s: (i, 0)),
        out_shape=jax.ShapeDtypeStruct((T, D), F32),
        scratch_shapes=[pltpu.VMEM((D, tile), BF16),
                        pltpu.VMEM((PEER_HEADS, ntc, N_KEYS, LANES), F32),
                        pltpu.VMEM((PEER_HEADS, ntc, N_KEYS, LANES), F32),
                        pltpu.VMEM((PEER_HEADS, ntc, 1, LANES), F32),
                        pltpu.VMEM((2, PEER_TOPK + 1, PEER_HEADS, tile), F32),
                        pltpu.VMEM((ec, tile), F32), pltpu.VMEM((ec, tile), F32),
                        pltpu.VMEM((ec, tile), BF16), pltpu.VMEM((ec, tile), BF16),
                        pltpu.VMEM((D, tile), F32), pltpu.VMEM((D, tile), F32)],
        compiler_params=_params("arbitrary", "arbitrary"),
        name="peer",
    )(x, g, wq_t, keys, u, u, vt)


def _rope_tables(pos_ref, inv_ref):
    ang = pos_ref[...].astype(F32) * inv_ref[...]
    lane = lax.broadcasted_iota(jnp.int32, ang.shape, 1) % HEAD_DIM
    cos = jnp.cos(ang)
    sin = jnp.sin(ang)
    half = ROT_DIM // 2
    sin_lo = jnp.where(lane < half, -sin, 0.0)
    sin_hi = jnp.where((lane >= half) & (lane < ROT_DIM), sin, 0.0)
    return cos, sin_lo, sin_hi


def _rope_apply(t, cos, sin_lo, sin_hi):
    half = ROT_DIM // 2
    up = pltpu.roll(t, LANES - half, 1)
    dn = pltpu.roll(t, half, 1)
    return t * cos + up * sin_lo + dn * sin_hi


def _ple_core(x_ref, p_ref, g_ref, proj_ref, gw_ref):
    x = x_ref[...]
    hn = _rms(x, g_ref[...], NORM_EPS).astype(BF16)
    gate = jax.nn.sigmoid(_dot(hn, gw_ref[...]))
    return x + _dot(p_ref[...].astype(BF16), proj_ref[...]) * gate


def _ple_kv_kernel(x_ref, p_ref, g_ref, proj_ref, gw_ref, kvg_ref, kvw_ref, kvb_ref, pos_ref,
                   inv_ref, o_ref, k_ref, v_ref):
    x2 = _ple_core(x_ref, p_ref, g_ref, proj_ref, gw_ref)
    o_ref[...] = x2
    kv = _dot(_rms(x2, kvg_ref[...], NORM_EPS).astype(BF16), kvw_ref[...]) + kvb_ref[...]
    kvd = k_ref.shape[1]
    cos, sin_lo, sin_hi = _rope_tables(pos_ref, inv_ref)
    k_ref[...] = _rope_apply(kv[:, :kvd], cos, sin_lo, sin_hi).astype(k_ref.dtype)
    v_ref[...] = kv[:, kvd:].astype(v_ref.dtype)


def _ple_final_kernel(x_ref, p_ref, g_ref, proj_ref, gw_ref, fg_ref, o_ref):
    x2 = _ple_core(x_ref, p_ref, g_ref, proj_ref, gw_ref)
    o_ref[...] = _rms(x2, fg_ref[...], NORM_EPS)


def _ple_kv(x, p, g, proj, gw, kvg, kvw, kvb, pos, inv, tile):
    T, D = x.shape
    P = p.shape[1]
    kvd = kvw.shape[1] // 2
    tok = lambda w: pl.BlockSpec((tile, w), lambda i: (i, 0))
    return pl.pallas_call(
        _ple_kv_kernel,
        grid=(T // tile,),
        in_specs=[tok(D), tok(P), _const_spec((1, D)), _const_spec(proj.shape),
                  _const_spec(gw.shape), _const_spec((1, D)), _const_spec(kvw.shape),
                  _const_spec(kvb.shape), tok(1), _const_spec(inv.shape)],
        out_specs=[tok(D), tok(kvd), tok(kvd)],
        out_shape=[jax.ShapeDtypeStruct((T, D), F32), jax.ShapeDtypeStruct((T, kvd), BF16),
                   jax.ShapeDtypeStruct((T, kvd), BF16)],
        compiler_params=_params("arbitrary"),
        name="ple_kv",
    )(x, p, g, proj, gw, kvg, kvw, kvb, pos, inv)


def _ple_final(x, p, g, proj, gw, fg, tile):
    T, D = x.shape
    P = p.shape[1]
    tok = lambda w: pl.BlockSpec((tile, w), lambda i: (i, 0))
    return pl.pallas_call(
        _ple_final_kernel,
        grid=(T // tile,),
        in_specs=[tok(D), tok(P), _const_spec((1, D)), _const_spec(proj.shape),
                  _const_spec(gw.shape), _const_spec((1, D))],
        out_specs=tok(D),
        out_shape=jax.ShapeDtypeStruct((T, D), F32),
        compiler_params=_params("arbitrary"),
        name="ple_final",
    )(x, p, g, proj, gw, fg)


def _attn_kernel(x_ref, g_ref, qw_ref, qb_ref, sink_ref, owt_ref, ob_ref, kc_ref, kp_ref, vc_ref,
                 vp_ref, pos_ref, invc_ref, o_ref, qt_ref, att_ref, *, n_q_heads):
    tile = x_ref.shape[0]
    W = WINDOW
    nblk = tile // W
    q_per_kv = n_q_heads // N_KV_HEADS
    half = ROT_DIM // 2
    scale = HEAD_DIM ** -0.5
    first = pl.program_id(1) == 0

    h = _rms(x_ref[...], g_ref[...], NORM_EPS).astype(BF16)
    q = (_dot(h, qw_ref[...]) + qb_ref[...]) * scale
    qt = q.T
    ang = invc_ref[...] * pos_ref[...].astype(F32)
    cos, sin = jnp.cos(ang), jnp.sin(ang)
    for hh in range(n_q_heads):
        base = hh * HEAD_DIM
        t1 = qt[base:base + half, :]
        t2 = qt[base + half:base + ROT_DIM, :]
        qt_ref[base:base + half, :] = (t1 * cos - t2 * sin).astype(BF16)
        qt_ref[base + half:base + ROT_DIM, :] = (t2 * cos + t1 * sin).astype(BF16)
        qt_ref[base + ROT_DIM:base + HEAD_DIM, :] = qt[base + ROT_DIM:base + HEAD_DIM, :].astype(BF16)

    kj = lax.broadcasted_iota(jnp.int32, (2 * W, W), 0)
    qi = lax.broadcasted_iota(jnp.int32, (2 * W, W), 1)
    first_lo = jnp.where(first, W, 0)
    band = jnp.where((kj > qi) & (kj <= qi + W), 0.0, -jnp.inf)
    band_first = jnp.where(kj >= first_lo, band, -jnp.inf)
    band = jnp.concatenate([band] * q_per_kv, axis=1)
    band_first = jnp.concatenate([band_first] * q_per_kv, axis=1)

    for n in range(nblk):
        cols = slice(n * W, (n + 1) * W)
        if n == 0:
            kprev, vprev = kp_ref[...], vp_ref[...]
        else:
            kprev, vprev = kc_ref[(n - 1) * W:n * W, :], vc_ref[(n - 1) * W:n * W, :]
        kblk = jnp.concatenate([kprev, kc_ref[cols, :]], axis=0)
        vblk = jnp.concatenate([vprev, vc_ref[cols, :]], axis=0)
        bias = band_first if n == 0 else band
        for g in range(N_KV_HEADS):
            ks = kblk[:, g * HEAD_DIM:(g + 1) * HEAD_DIM]
            vs = vblk[:, g * HEAD_DIM:(g + 1) * HEAD_DIM]
            heads = range(g * q_per_kv, (g + 1) * q_per_kv)
            qs = jnp.concatenate([qt_ref[hh * HEAD_DIM:(hh + 1) * HEAD_DIM, cols] for hh in heads],
                                 axis=1)
            sink = jnp.concatenate([jnp.broadcast_to(sink_ref[:, hh:hh + 1], (1, W)) for hh in heads],
                                   axis=1)
            s = _dot(ks, qs) + bias
            m = jnp.maximum(jnp.max(s, axis=0, keepdims=True), sink)
            e = jnp.exp(s - m)
            denom = jnp.sum(e, axis=0, keepdims=True) + jnp.exp(sink - m)
            o = _dot(vs, e.astype(BF16), _TN) / denom
            for r, hh in enumerate(heads):
                att_ref[hh * HEAD_DIM:(hh + 1) * HEAD_DIM, cols] = o[:, r * W:(r + 1) * W].astype(BF16)

    o_ref[...] = x_ref[...] + _dot(owt_ref[...], att_ref[...]).T + ob_ref[...]


def _attn(x, g, qw, qb, sinks, ow_t, ob, k, v, pos, inv_col, batch, tile):
    T, D = x.shape
    kvd = k.shape[1]
    nq = qw.shape[1] // HEAD_DIM
    nt = T // batch // tile
    bpt = tile // WINDOW
    row = lambda b, i: (b * nt + i, 0)
    prev = lambda b, i: (jnp.maximum((b * nt + i) * bpt - 1, 0), 0)
    kern = functools.partial(_attn_kernel, n_q_heads=nq)
    return pl.pallas_call(
        kern,
        grid=(batch, nt),
        in_specs=[pl.BlockSpec((tile, D), row), _const_spec((1, D)), _const_spec(qw.shape),
                  _const_spec(qb.shape), _const_spec(sinks.shape), _const_spec(ow_t.shape),
                  _const_spec(ob.shape),
                  pl.BlockSpec((tile, kvd), row), pl.BlockSpec((WINDOW, kvd), prev),
                  pl.BlockSpec((tile, kvd), row), pl.BlockSpec((WINDOW, kvd), prev),
                  pl.BlockSpec((None, 1, tile), lambda b, i: (b * nt + i, 0, 0)),
                  _const_spec(inv_col.shape)],
        out_specs=pl.BlockSpec((tile, D), row),
        out_shape=jax.ShapeDtypeStruct((T, D), F32),
        scratch_shapes=[pltpu.VMEM((qw.shape[1], tile), BF16),
                        pltpu.VMEM((qw.shape[1], tile), BF16)],
        compiler_params=_params("arbitrary", "arbitrary"),
        name="swa_attn",
    )(x, g, qw, qb, sinks, ow_t, ob, k, k, v, v, pos.reshape(T // tile, 1, tile), inv_col)


def _row(v):
    return v.reshape(1, -1)


def kernel(x, p, positions, ssm_norm, ssm_in_w, ssm_conv_w, ssm_conv_b, ssm_dt_bias, ssm_A_log, ssm_D, ssm_gate_norm, ssm_out_w, kv_norm, kv_w, kv_b, attn_norm, q_w, q_b, sinks, o_w, o_b, peer_norm, peer_q_w, peer_sub_keys, peer_u, peer_v, ple_norm, ple_proj, ple_gate_w, final_norm):
    B, S, D = x.shape
    T = B * S
    depth = p.shape[0]
    n_a = ssm_norm.shape[0]
    H = ssm_D.shape[1]
    d_inner = H * SSM_HEADDIM
    conv_dim = ssm_conv_w.shape[2]

    xt = x.reshape(T, D)
    pos = positions.reshape(T, 1)
    lane = np.arange(LANES) % HEAD_DIM
    inv = np.where(lane < ROT_DIM,
                   ROPE_THETA ** (-(2.0 * (lane % (ROT_DIM // 2))) / ROT_DIM), 0.0)
    inv = jnp.asarray(inv.reshape(1, LANES), F32)
    inv_col = inv[0, :ROT_DIM // 2].reshape(-1, 1)
    expand = jnp.asarray(np.repeat(np.eye(H, dtype=np.float32), SSM_HEADDIM, axis=1))
    tril = jnp.asarray(np.tril(np.ones((SSD_CHUNK, SSD_CHUNK), np.float32)))

    peer_wq_t = peer_q_w.transpose(0, 2, 1).astype(BF16)
    peer_keys = peer_sub_keys.astype(BF16)
    peer_u_b = peer_u.astype(BF16)
    peer_vt = peer_v.transpose(0, 2, 1).astype(BF16)

    k_sh = v_sh = None
    for i in range(depth):
        if i < n_a:
            z, xs, bc, dtr = _inproj(xt, _row(ssm_norm[i]), ssm_in_w[i].astype(BF16), ssm_conv_w[i],
                                     _row(ssm_conv_b[i]), d_inner=d_inner, tile=256, seq=S)
            y = _ssd(z, xs, bc, dtr, _row(ssm_dt_bias[i]), _row(ssm_A_log[i]),
                     _row(jnp.repeat(ssm_D[i], SSM_HEADDIM)), _row(ssm_gate_norm[i]), expand, tril,
                     batch=B)
            xt = _outproj(xt, y, ssm_out_w[i].astype(BF16), tile=512)
        else:
            j = i - n_a
            xt = _attn(xt, _row(attn_norm[j]), q_w[j].astype(BF16), _row(q_b[j]), _row(sinks[j]),
                       o_w[j].T.astype(BF16), _row(o_b[j]), k_sh, v_sh, pos, inv_col, batch=B,
                       tile=512)
        xt = _peer(xt, _row(peer_norm[i]), peer_wq_t, peer_keys, peer_u_b, peer_vt, layer=i,
                   tile=512, ec=1024)
        if i == n_a - 1:
            xt, k_sh, v_sh = _ple_kv(xt, p[i].reshape(T, -1), _row(ple_norm[i]),
                                     ple_proj[i].astype(BF16), ple_gate_w[i].astype(BF16),
                                     _row(kv_norm), kv_w.astype(BF16), _row(kv_b), pos, inv,
                                     tile=512)
        elif i == depth - 1:
            xt = _ple_final(xt, p[i].reshape(T, -1), _row(ple_norm[i]), ple_proj[i].astype(BF16),
                            ple_gate_w[i].astype(BF16), _row(final_norm), tile=512)
        else:
            raise NotImplementedError("PLE without K/V or final norm")
    return xt.reshape(B, S, D)
```

```python
import functools
import math

import jax
import jax.numpy as jnp
import numpy as np
from jax import lax
from jax.experimental import pallas as pl
from jax.experimental.pallas import tpu as pltpu

F32 = jnp.float32
BF16 = jnp.bfloat16

NORM_EPS = 1e-6
GATED_NORM_EPS = 1e-5
SSM_HEADDIM = 64
SSM_GROUPS = 8
SSM_STATE = 128
CONV_K = 4
SSD_CHUNK = 128
SEQ_PER_STEP = 2
CONV_COLS = 512
HEAD_DIM = 64
N_KV_HEADS = 2
WINDOW = 128
ROT_DIM = HEAD_DIM // 4
ROPE_THETA = 500000.0
PEER_HEADS = 8
N_KEYS = 128
PEER_TOPK = 16

LANES = 128
SUBLANES = 8
VMEM_LIMIT = 56 * 1024 * 1024

_CAND_PAIRS = [(r1, r2) for r1 in range(PEER_TOPK + 1) for r2 in range(PEER_TOPK + 1)
               if (r1 + 1) * (r2 + 1) <= PEER_TOPK + 1]


def _sort_network(n):
    pairs = []
    p = 1
    while p < n:
        k = p
        while k >= 1:
            for j in range(k % p, n - k, 2 * k):
                for i in range(min(k, n - j - k)):
                    if (i + j) // (2 * p) == (i + j + k) // (2 * p):
                        pairs.append((i + j, i + j + k))
            k //= 2
        p *= 2
    return pairs


_SORT_PAIRS = _sort_network(N_KEYS // SUBLANES)


def _params(*sem):
    return pltpu.CompilerParams(dimension_semantics=sem, vmem_limit_bytes=VMEM_LIMIT)


def _const_spec(shape):
    nd = len(shape)
    return pl.BlockSpec(shape, lambda *_: (0,) * nd, pipeline_mode=pl.Buffered(1))


def _rms(x, g, eps):
    return x * lax.rsqrt(jnp.mean(x * x, axis=-1, keepdims=True) + eps) * g


def _dot(a, b, dims=None, precision=None):
    if dims is None:
        dims = (((a.ndim - 1,), (0,)), ((), ()))
    return lax.dot_general(a, b, dims, precision=precision, preferred_element_type=F32)


_NT = (((1,), (1,)), ((), ()))
_TN = (((0,), (0,)), ((), ()))
_HI = lax.Precision.HIGHEST


def _inproj_kernel(x_ref, g_ref, w_ref, cw_ref, cb_ref, z_ref, xs_ref, bc_ref,
                   dt_ref, xb_ref, tail_ref, *, tiles_per_seq):
    tile = x_ref.shape[0]
    d_inner = xs_ref.shape[1]
    conv_dim = tail_ref.shape[1]
    tail = SUBLANES

    @pl.when(pl.program_id(0) % tiles_per_seq == 0)
    def _():
        tail_ref[...] = jnp.zeros(tail_ref.shape, F32)

    h = _rms(x_ref[...], g_ref[...], NORM_EPS).astype(BF16)
    dt_ref[...] = _dot(h, w_ref[:, d_inner + conv_dim:])

    for gi, lo in enumerate(range(0, conv_dim, CONV_COLS)):
        cs = slice(lo, lo + CONV_COLS)
        stage = xb_ref.at[gi % 2]
        stage[0:tail, :] = tail_ref[:, cs]
        stage[tail:tail + tile, :] = _dot(h, w_ref[:, d_inner + lo:d_inner + lo + CONV_COLS])
        if lo < d_inner:
            z_ref[:, cs] = _dot(h, w_ref[:, cs])
        u = stage[...]
        tail_ref[:, cs] = u[tile:tile + tail, :]
        acc = cb_ref[:, cs] + u[tail:, :] * cw_ref[CONV_K - 1:CONV_K, cs]
        for back in range(1, CONV_K):
            acc = acc + pltpu.roll(u, back, 0)[tail:, :] * cw_ref[CONV_K - 1 - back:CONV_K - back, cs]
        act = acc * jax.nn.sigmoid(acc)
        if lo < d_inner:
            xs_ref[:, cs] = act
        else:
            bc_ref[:, lo - d_inner:lo - d_inner + CONV_COLS] = act.astype(BF16)


def _inproj(x, g, w, cw, cb, d_inner, tile, seq):
    T, D = x.shape
    nz, nx = d_inner, cw.shape[1]
    nd = w.shape[1] - nz - nx
    assert nz % CONV_COLS == 0 and nx % CONV_COLS == 0 and seq % tile == 0
    tok = lambda w: pl.BlockSpec((tile, w), lambda i: (i, 0))
    kern = functools.partial(_inproj_kernel, tiles_per_seq=seq // tile)
    return pl.pallas_call(
        kern,
        grid=(T // tile,),
        in_specs=[tok(D), _const_spec((1, D)), _const_spec(w.shape), _const_spec(cw.shape),
                  _const_spec(cb.shape)],
        out_specs=[tok(nz), tok(nz), tok(nx - nz), tok(nd)],
        out_shape=[jax.ShapeDtypeStruct((T, nz), F32), jax.ShapeDtypeStruct((T, nz), F32),
                   jax.ShapeDtypeStruct((T, nx - nz), BF16), jax.ShapeDtypeStruct((T, nd), F32)],
        scratch_shapes=[pltpu.VMEM((2, tile + SUBLANES, CONV_COLS), F32),
                        pltpu.VMEM((SUBLANES, nx), F32)],
        compiler_params=_params("arbitrary"),
        name="ssm_inproj",
    )(x, g, w, cw, cb)


def _ssd_kernel(z_ref, xs_ref, bc_ref, dtr_ref, dtb_ref, alog_ref, dexp_ref, gn_ref,
                expand_ref, tril_ref, y_ref, st_ref, *, d_inner, n_heads):
    L = SSD_CHUNK
    gw = d_inner // SSM_GROUPS
    hpg = n_heads // SSM_GROUPS
    gn_state = SSM_GROUPS * SSM_STATE
    seqs = range(z_ref.shape[0])

    @pl.when(pl.program_id(1) == 0)
    def _():
        st_ref[...] = jnp.zeros(st_ref.shape, F32)

    tril = tril_ref[...]
    expand = expand_ref[...]
    causal = tril > 0.5
    a_cs, a_cs_t, a_cs_x, dt_x = [], [], [], []
    for q in seqs:
        dt_in = dtr_ref[q] + dtb_ref[...]
        dt = jnp.maximum(dt_in, 0.0) + jnp.log1p(jnp.exp(-jnp.abs(dt_in)))
        a = dt * (-jnp.exp(alog_ref[...]))
        acs = _dot(tril, a, precision=_HI)
        a_cs.append(acs)
        a_cs_t.append(acs.T)
        a_cs_x.append(_dot(acs, expand, precision=_HI))
        dt_x.append(_dot(dt, expand, precision=_HI))

    for g in range(SSM_GROUPS):
        lo = g * gw
        for q in seqs:
            xs = xs_ref[q, :, lo:lo + gw]
            bm = bc_ref[q, :, g * SSM_STATE:(g + 1) * SSM_STATE]
            cm = bc_ref[q, :, gn_state + g * SSM_STATE:gn_state + (g + 1) * SSM_STATE]
            xdt = xs * dt_x[q][:, lo:lo + gw]
            cb = _dot(cm, bm, _NT)
            yd = []
            for r in range(hpg):
                hh = g * hpg + r
                seg = a_cs[q][:, hh:hh + 1] - a_cs_t[q][hh:hh + 1, :]
                lmat = jnp.exp(jnp.where(causal, seg, -jnp.inf))
                m = (cb * lmat).astype(BF16)
                yd.append(_dot(m, xdt[:, r * SSM_HEADDIM:(r + 1) * SSM_HEADDIM].astype(BF16)))
            y = jnp.concatenate(yd, axis=1)
            acx = a_cs_x[q][:, lo:lo + gw]
            alx = a_cs_x[q][L - 1:L, lo:lo + gw]
            prev = st_ref[q, g]
            y = y + _dot(cm, prev.astype(BF16)) * jnp.exp(acx)
            xd = (xdt * jnp.exp(alx - acx)).astype(BF16)
            st_ref[q, g] = prev * jnp.exp(alx) + _dot(bm, xd, _TN)
            y = y + xs * dexp_ref[:, lo:lo + gw]
            zg = z_ref[q, :, lo:lo + gw]
            y = y * (zg * jax.nn.sigmoid(zg))
            y = y * lax.rsqrt(jnp.mean(y * y, axis=-1, keepdims=True) + GATED_NORM_EPS)
            y_ref[q, :, lo:lo + gw] = (y * gn_ref[:, lo:lo + gw]).astype(y_ref.dtype)


def _ssd(z, xs, bc, dtr, dtb, alog, dexp, gn, expand, tril, batch):
    T, d_inner = z.shape
    H = dtr.shape[1]
    L = SSD_CHUNK
    seq = T // batch
    assert batch % SEQ_PER_STEP == 0 and seq % L == 0
    blk = lambda w: pl.BlockSpec((SEQ_PER_STEP, L, w), lambda b, c: (b, c, 0))
    per_seq = lambda v: v.reshape(batch, seq, v.shape[1])
    kern = functools.partial(_ssd_kernel, d_inner=d_inner, n_heads=H)
    y = pl.pallas_call(
        kern,
        grid=(batch // SEQ_PER_STEP, seq // L),
        in_specs=[blk(d_inner), blk(d_inner), blk(bc.shape[1]), blk(H),
                  _const_spec(dtb.shape), _const_spec(alog.shape), _const_spec(dexp.shape),
                  _const_spec(gn.shape), _const_spec(expand.shape), _const_spec(tril.shape)],
        out_specs=blk(d_inner),
        out_shape=jax.ShapeDtypeStruct((batch, seq, d_inner), BF16),
        scratch_shapes=[pltpu.VMEM((SEQ_PER_STEP, SSM_GROUPS, SSM_STATE, d_inner // SSM_GROUPS), F32)],
        compiler_params=_params("arbitrary", "arbitrary"),
        name="ssd_scan",
    )(per_seq(z), per_seq(xs), per_seq(bc), per_seq(dtr), dtb, alog, dexp, gn, expand, tril)
    return y.reshape(T, d_inner)


def _outproj_kernel(x_ref, y_ref, w_ref, o_ref):
    o_ref[...] = x_ref[...] + _dot(y_ref[...], w_ref[...])


def _outproj(x, y, w, tile):
    T, D = x.shape
    K = y.shape[1]
    return pl.pallas_call(
        _outproj_kernel,
        grid=(T // tile,),
        in_specs=[pl.BlockSpec((tile, D), lambda i: (i, 0)),
                  pl.BlockSpec((tile, K), lambda i: (i, 0)), _const_spec(w.shape)],
        out_specs=pl.BlockSpec((tile, D), lambda i: (i, 0)),
        out_shape=jax.ShapeDtypeStruct((T, D), F32),
        compiler_params=_params("arbitrary"),
        name="ssm_outproj",
    )(x, y, w)


def _peer_kernel(x_ref, g_ref, wq_ref, keys_ref, u0_ref, un_ref, vt_ref, o_ref,
                 hb_ref, e1_ref, e2_ref, gmin_ref, top_ref,
                 a0_ref, a1_ref, p0_ref, p1_ref, acca_ref, accb_ref):
    g = pl.program_id(1)
    last = pl.num_programs(1) - 1
    tt = x_ref.shape[0]
    ec = un_ref.shape[0] // 2
    rows_per_chunk = ec // N_KEYS
    mxu_cols = 2 * LANES
    n_piece = tt // mxu_cols
    half = N_KEYS
    ntop = PEER_TOPK + 1
    neg_inf = -jnp.inf

    @pl.when(g == 0)
    def _route():
        hb = _rms(x_ref[...], g_ref[...], NORM_EPS).T.astype(BF16)
        hb_ref[...] = hb
        for h in range(PEER_HEADS):
            q = _dot(wq_ref[h * 2 * half:(h + 1) * 2 * half, :], hb).astype(BF16)
            a0_ref[h * N_KEYS:(h + 1) * N_KEYS, :] = _dot(keys_ref[0], q[0:half, :])
            a1_ref[h * N_KEYS:(h + 1) * N_KEYS, :] = _dot(keys_ref[1], q[half:2 * half, :])

        def top_values(s_ref, h, cs):
            v = [s_ref[h * N_KEYS + r * SUBLANES:h * N_KEYS + (r + 1) * SUBLANES, cs]
                 for r in range(N_KEYS // SUBLANES)]
            for i, j in _SORT_PAIRS:
                v[i], v[j] = jnp.maximum(v[i], v[j]), jnp.minimum(v[i], v[j])
            depth = len(v)
            out = []
            for k in range(ntop):
                m = v[0]
                for shift in (4, 2, 1):
                    m = jnp.maximum(m, pltpu.roll(m, shift, 0))
                out.append(m)
                hit = v[0] == m
                for r in range(min(depth, ntop - 1 - k)):
                    v[r] = jnp.where(hit, v[r + 1] if r + 1 < depth else neg_inf, v[r])
            return out

        for tc in range(tt // LANES):
            cs = slice(tc * LANES, (tc + 1) * LANES)
            for c, s_ref in enumerate((a0_ref, a1_ref)):
                for h in range(PEER_HEADS):
                    for r, m in enumerate(top_values(s_ref, h, cs)):
                        top_ref[c, r, h:h + 1, cs] = m[0:1, :]

            a = [top_ref[0, r, :, cs] for r in range(ntop)]
            b = [top_ref[1, r, :, cs] for r in range(ntop)]
            cand = [a[r1] + b[r2] for r1, r2 in _CAND_PAIRS]
            a0, b0 = a[0], b[0]
            m0 = a0 + b0
            z = jnp.zeros(m0.shape, F32)
            for k in range(ntop):
                m = functools.reduce(jnp.maximum, cand)
                if k < PEER_TOPK:
                    z = z + jnp.exp(m - m0)
                if k == PEER_TOPK - 1:
                    v16 = m
                if k < ntop - 1:
                    cand = [jnp.where(cv == m, neg_inf, cv) for cv in cand]
            v17 = m
            zinv = 1.0 / z
            gmin = jnp.exp(0.5 * (v16 + v17) - m0) * zinv
            for h in range(PEER_HEADS):
                hk = slice(h * N_KEYS, (h + 1) * N_KEYS)
                e1_ref[h, tc] = jnp.exp(a0_ref[hk, cs] - a0[h:h + 1, :])
                e2_ref[h, tc] = jnp.exp(a1_ref[hk, cs] - b0[h:h + 1, :]) * zinv[h:h + 1, :]
                gmin_ref[h, tc] = gmin[h:h + 1, :]

            if (tc + 1) % (mxu_cols // LANES) == 0:
                ps = slice((tc + 1) * LANES - mxu_cols, (tc + 1) * LANES)
                a0_ref[:, ps] = _dot(u0_ref[0:ec, :], hb_ref[:, ps])
                a1_ref[:, ps] = _dot(u0_ref[ec:2 * ec, :], hb_ref[:, ps])

        acca_ref[...] = jnp.zeros(acca_ref.shape, F32)
        accb_ref[...] = jnp.zeros(accb_ref.shape, F32)

    def gate_gelu(a_ref, p_ref, chunk, tc):
        i0 = pl.multiple_of(chunk * rows_per_chunk, SUBLANES)
        cs = slice(tc * LANES, (tc + 1) * LANES)
        e18 = [e1_ref[h, tc, pl.ds(i0, rows_per_chunk), :] for h in range(PEER_HEADS)]
        gm = [gmin_ref[h, tc] for h in range(PEER_HEADS)]
        for ii in range(rows_per_chunk):
            rows = slice(ii * N_KEYS, (ii + 1) * N_KEYS)
            w = jnp.zeros((N_KEYS, LANES), F32)
            for h in range(PEER_HEADS):
                gate = e18[h][ii:ii + 1, :] * e2_ref[h, tc]
                w = w + jnp.where(gate >= gm[h], gate, 0.0)
            av = a_ref[rows, cs]
            gelu = 0.5 * av * (1.0 + lax.erf(av * np.float32(math.sqrt(0.5))))
            p_ref[rows, cs] = (w * gelu).astype(BF16)

    for par, (a_ref, p_ref, acc_ref) in enumerate(((a0_ref, p0_ref, acca_ref),
                                                   (a1_ref, p1_ref, accb_ref))):
        es = slice(par * ec, (par + 1) * ec)
        for piece in range(n_piece):
            cs = slice(piece * mxu_cols, (piece + 1) * mxu_cols)
            for tc in range(piece * (mxu_cols // LANES), (piece + 1) * (mxu_cols // LANES)):
                gate_gelu(a_ref, p_ref, 2 * g + par, tc)
            acc_ref[:, cs] += _dot(vt_ref[:, es], p_ref[:, cs])
            a_ref[:, cs] = _dot(un_ref[es, :], hb_ref[:, cs])

    @pl.when(g == last)
    def _finish():
        o_ref[...] = x_ref[...] + (acca_ref[...] + accb_ref[...]).T


def _peer(x, g, wq_t, keys, u, vt, layer, tile, ec):
    T, D = x.shape
    E = u.shape[1]
    nchunk = E // ec
    assert E == N_KEYS * N_KEYS and ec == SUBLANES * N_KEYS and tile % LANES == 0
    assert ec == PEER_HEADS * N_KEYS
    assert nchunk % 2 == 0
    ntc = tile // LANES
    nstep = nchunk // 2
    return pl.pallas_call(
        _peer_kernel,
        grid=(T // tile, nstep),
        in_specs=[pl.BlockSpec((tile, D), lambda i, s: (i, 0)),
                  _const_spec((1, D)),
                  pl.BlockSpec((None,) + wq_t.shape[1:], lambda i, s: (layer, 0, 0),
                               pipeline_mode=pl.Buffered(1)),
                  pl.BlockSpec((None,) + keys.shape[1:], lambda i, s: (layer, 0, 0, 0),
                               pipeline_mode=pl.Buffered(1)),
                  pl.BlockSpec((None, 2 * ec, D), lambda i, s: (layer, 0, 0),
                               pipeline_mode=pl.Buffered(1)),
                  pl.BlockSpec((None, 2 * ec, D),
                               lambda i, s: (layer, jnp.minimum(s + 1, nstep - 1), 0)),
                  pl.BlockSpec((None, D, 2 * ec), lambda i, s: (layer, 0, s))],
        out_specs=pl.BlockSpec((tile, D), lambda i, s: (i, 0)),
        out_shape=jax.ShapeDtypeStruct((T, D), F32),
        scratch_shapes=[pltpu.VMEM((D, tile), BF16),
                        pltpu.VMEM((PEER_HEADS, ntc, N_KEYS, LANES), F32),
                        pltpu.VMEM((PEER_HEADS, ntc, N_KEYS, LANES), F32),
                        pltpu.VMEM((PEER_HEADS, ntc, 1, LANES), F32),
                        pltpu.VMEM((2, PEER_TOPK + 1, PEER_HEADS, tile), F32),
                        pltpu.VMEM((ec, tile), F32), pltpu.VMEM((ec, tile), F32),
                        pltpu.VMEM((ec, tile), BF16), pltpu.VMEM((ec, tile), BF16),
                        pltpu.VMEM((D, tile), F32), pltpu.VMEM((D, tile), F32)],
        compiler_params=_params("arbitrary", "arbitrary"),
        name="peer",
    )(x, g, wq_t, keys, u, u, vt)


def _rope_tables(pos_ref, inv_ref):
    ang = pos_ref[...].astype(F32) * inv_ref[...]
    lane = lax.broadcasted_iota(jnp.int32, ang.shape, 1) % HEAD_DIM
    cos = jnp.cos(ang)
    sin = jnp.sin(ang)
    half = ROT_DIM // 2
    sin_lo = jnp.where(lane < half, -sin, 0.0)
    sin_hi = jnp.where((lane >= half) & (lane < ROT_DIM), sin, 0.0)
    return cos, sin_lo, sin_hi


def _rope_apply(t, cos, sin_lo, sin_hi):
    half = ROT_DIM // 2
    up = pltpu.roll(t, LANES - half, 1)
    dn = pltpu.roll(t, half, 1)
    return t * cos + up * sin_lo + dn * sin_hi


def _ple_core(x_ref, p_ref, g_ref, proj_ref, gw_ref):
    x = x_ref[...]
    hn = _rms(x, g_ref[...], NORM_EPS).astype(BF16)
    gate = jax.nn.sigmoid(_dot(hn, gw_ref[...]))
    return x + _dot(p_ref[...].astype(BF16), proj_ref[...]) * gate


def _ple_kv_kernel(x_ref, p_ref, g_ref, proj_ref, gw_ref, kvg_ref, kvw_ref, kvb_ref, pos_ref,
                   inv_ref, o_ref, k_ref, v_ref):
    x2 = _ple_core(x_ref, p_ref, g_ref, proj_ref, gw_ref)
    o_ref[...] = x2
    kv = _dot(_rms(x2, kvg_ref[...], NORM_EPS).astype(BF16), kvw_ref[...]) + kvb_ref[...]
    kvd = k_ref.shape[1]
    cos, sin_lo, sin_hi = _rope_tables(pos_ref, inv_ref)
    k_ref[...] = _rope_apply(kv[:, :kvd], cos, sin_lo, sin_hi).astype(k_ref.dtype)
    v_ref[...] = kv[:, kvd:].astype(v_ref.dtype)


def _ple_final_kernel(x_ref, p_ref, g_ref, proj_ref, gw_ref, fg_ref, o_ref):
    x2 = _ple_core(x_ref, p_ref, g_ref, proj_ref, gw_ref)
    o_ref[...] = _rms(x2, fg_ref[...], NORM_EPS)


def _ple_kv(x, p, g, proj, gw, kvg, kvw, kvb, pos, inv, tile):
    T, D = x.shape
    P = p.shape[1]
    kvd = kvw.shape[1] // 2
    tok = lambda w: pl.BlockSpec((tile, w), lambda i: (i, 0))
    return pl.pallas_call(
        _ple_kv_kernel,
        grid=(T // tile,),
        in_specs=[tok(D), tok(P), _const_spec((1, D)), _const_spec(proj.shape),
                  _const_spec(gw.shape), _const_spec((1, D)), _const_spec(kvw.shape),
                  _const_spec(kvb.shape), tok(1), _const_spec(inv.shape)],
        out_specs=[tok(D), tok(kvd), tok(kvd)],
        out_shape=[jax.ShapeDtypeStruct((T, D), F32), jax.ShapeDtypeStruct((T, kvd), BF16),
                   jax.ShapeDtypeStruct((T, kvd), BF16)],
        compiler_params=_params("arbitrary"),
        name="ple_kv",
    )(x, p, g, proj, gw, kvg, kvw, kvb, pos, inv)


def _ple_final(x, p, g, proj, gw, fg, tile):
    T, D = x.shape
    P = p.shape[1]
    tok = lambda w: pl.BlockSpec((tile, w), lambda i: (i, 0))
    return pl.pallas_call(
        _ple_final_kernel,
        grid=(T // tile,),
        in_specs=[tok(D), tok(P), _const_spec((1, D)), _const_spec(proj.shape),
                  _const_spec(gw.shape), _const_spec((1, D))],
        out_specs=tok(D),
        out_shape=jax.ShapeDtypeStruct((T, D), F32),
        compiler_params=_params("arbitrary"),
        name="ple_final",
    )(x, p, g, proj, gw, fg)


def _attn_kernel(x_ref, g_ref, qw_ref, qb_ref, sink_ref, owt_ref, ob_ref, kc_ref, kp_ref, vc_ref,
                 vp_ref, pos_ref, invc_ref, o_ref, qt_ref, att_ref, *, n_q_heads):
    tile = x_ref.shape[0]
    W = WINDOW
    nblk = tile // W
    q_per_kv = n_q_heads // N_KV_HEADS
    half = ROT_DIM // 2
    scale = HEAD_DIM ** -0.5
    first = pl.program_id(1) == 0

    h = _rms(x_ref[...], g_ref[...], NORM_EPS).astype(BF16)
    q = (_dot(h, qw_ref[...]) + qb_ref[...]) * scale
    qt = q.T
    ang = invc_ref[...] * pos_ref[...].astype(F32)
    cos, sin = jnp.cos(ang), jnp.sin(ang)
    for hh in range(n_q_heads):
        base = hh * HEAD_DIM
        t1 = qt[base:base + half, :]
        t2 = qt[base + half:base + ROT_DIM, :]
        qt_ref[base:base + half, :] = (t1 * cos - t2 * sin).astype(BF16)
        qt_ref[base + half:base + ROT_DIM, :] = (t2 * cos + t1 * sin).astype(BF16)
        qt_ref[base + ROT_DIM:base + HEAD_DIM, :] = qt[base + ROT_DIM:base + HEAD_DIM, :].astype(BF16)

    kj = lax.broadcasted_iota(jnp.int32, (2 * W, W), 0)
    qi = lax.broadcasted_iota(jnp.int32, (2 * W, W), 1)
    first_lo = jnp.where(first, W, 0)
    band = jnp.where((kj > qi) & (kj <= qi + W), 0.0, -jnp.inf)
    band_first = jnp.where(kj >= first_lo, band, -jnp.inf)
    band = jnp.concatenate([band] * q_per_kv, axis=1)
    band_first = jnp.concatenate([band_first] * q_per_kv, axis=1)

    for n in range(nblk):
        cols = slice(n * W, (n + 1) * W)
        if n == 0:
            kprev, vprev = kp_ref[...], vp_ref[...]
        else:
            kprev, vprev = kc_ref[(n - 1) * W:n * W, :], vc_ref[(n - 1) * W:n * W, :]
        kblk = jnp.concatenate([kprev, kc_ref[cols, :]], axis=0)
        vblk = jnp.concatenate([vprev, vc_ref[cols, :]], axis=0)
        bias = band_first if n == 0 else band
        for g in range(N_KV_HEADS):
            ks = kblk[:, g * HEAD_DIM:(g + 1) * HEAD_DIM]
            vs = vblk[:, g * HEAD_DIM:(g + 1) * HEAD_DIM]
            heads = range(g * q_per_kv, (g + 1) * q_per_kv)
            qs = jnp.concatenate([qt_ref[hh * HEAD_DIM:(hh + 1) * HEAD_DIM, cols] for hh in heads],
                                 axis=1)
            sink = jnp.concatenate([jnp.broadcast_to(sink_ref[:, hh:hh + 1], (1, W)) for hh in heads],
                                   axis=1)
            s = _dot(ks, qs) + bias
            m = jnp.maximum(jnp.max(s, axis=0, keepdims=True), sink)
            e = jnp.exp(s - m)
            denom = jnp.sum(e, axis=0, keepdims=True) + jnp.exp(sink - m)
            o = _dot(vs, e.astype(BF16), _TN) / denom
            for r, hh in enumerate(heads):
                att_ref[hh * HEAD_DIM:(hh + 1) * HEAD_DIM, cols] = o[:, r * W:(r + 1) * W].astype(BF16)

    o_ref[...] = x_ref[...] + _dot(owt_ref[...], att_ref[...]).T + ob_ref[...]


def _attn(x, g, qw, qb, sinks, ow_t, ob, k, v, pos, inv_col, batch, tile):
    T, D = x.shape
    kvd = k.shape[1]
    nq = qw.shape[1] // HEAD_DIM
    nt = T // batch // tile
    bpt = tile // WINDOW
    row = lambda b, i: (b * nt + i, 0)
    prev = lambda b, i: (jnp.maximum((b * nt + i) * bpt - 1, 0), 0)
    kern = functools.partial(_attn_kernel, n_q_heads=nq)
    return pl.pallas_call(
        kern,
        grid=(batch, nt),
        in_specs=[pl.BlockSpec((tile, D), row), _const_spec((1, D)), _const_spec(qw.shape),
                  _const_spec(qb.shape), _const_spec(sinks.shape), _const_spec(ow_t.shape),
                  _const_spec(ob.shape),
                  pl.BlockSpec((tile, kvd), row), pl.BlockSpec((WINDOW, kvd), prev),
                  pl.BlockSpec((tile, kvd), row), pl.BlockSpec((WINDOW, kvd), prev),
                  pl.BlockSpec((None, 1, tile), lambda b, i: (b * nt + i, 0, 0)),
                  _const_spec(inv_col.shape)],
        out_specs=pl.BlockSpec((tile, D), row),
        out_shape=jax.ShapeDtypeStruct((T, D), F32),
        scratch_shapes=[pltpu.VMEM((qw.shape[1], tile), BF16),
                        pltpu.VMEM((qw.shape[1], tile), BF16)],
        compiler_params=_params("arbitrary", "arbitrary"),
        name="swa_attn",
    )(x, g, qw, qb, sinks, ow_t, ob, k, k, v, v, pos.reshape(T // tile, 1, tile), inv_col)


def _row(v):
    return v.reshape(1, -1)


def kernel(x, p, positions, ssm_norm, ssm_in_w, ssm_conv_w, ssm_conv_b, ssm_dt_bias, ssm_A_log, ssm_D, ssm_gate_norm, ssm_out_w, kv_norm, kv_w, kv_b, attn_norm, q_w, q_b, sinks, o_w, o_b, peer_norm, peer_q_w, peer_sub_keys, peer_u, peer_v, ple_norm, ple_proj, ple_gate_w, final_norm):
    B, S, D = x.shape
    T = B * S
    depth = p.shape[0]
    n_a = ssm_norm.shape[0]
    H = ssm_D.shape[1]
    d_inner = H * SSM_HEADDIM
    conv_dim = ssm_conv_w.shape[2]

    xt = x.reshape(T, D)
    pos = positions.reshape(T, 1)
    lane = np.arange(LANES) % HEAD_DIM
    inv = np.where(lane < ROT_DIM,
                   ROPE_THETA ** (-(2.0 * (lane % (ROT_DIM // 2))) / ROT_DIM), 0.0)
    inv = jnp.asarray(inv.reshape(1, LANES), F32)
    inv_col = inv[0, :ROT_DIM // 2].reshape(-1, 1)
    expand = jnp.asarray(np.repeat(np.eye(H, dtype=np.float32), SSM_HEADDIM, axis=1))
    tril = jnp.asarray(np.tril(np.ones((SSD_CHUNK, SSD_CHUNK), np.float32)))

    peer_wq_t = peer_q_w.astype(BF16).transpose(0, 2, 1)
    peer_keys = peer_sub_keys.astype(BF16)
    peer_u_b = peer_u.astype(BF16)
    peer_vt = peer_v.astype(BF16).transpose(0, 2, 1)

    k_sh = v_sh = None
    for i in range(depth):
        if i < n_a:
            z, xs, bc, dtr = _inproj(xt, _row(ssm_norm[i]), ssm_in_w[i].astype(BF16), ssm_conv_w[i],
                                     _row(ssm_conv_b[i]), d_inner=d_inner, tile=256, seq=S)
            y = _ssd(z, xs, bc, dtr, _row(ssm_dt_bias[i]), _row(ssm_A_log[i]),
                     _row(jnp.repeat(ssm_D[i], SSM_HEADDIM)), _row(ssm_gate_norm[i]), expand, tril,
                     batch=B)
            xt = _outproj(xt, y, ssm_out_w[i].astype(BF16), tile=512)
        else:
            j = i - n_a
            xt = _attn(xt, _row(attn_norm[j]), q_w[j].astype(BF16), _row(q_b[j]), _row(sinks[j]),
                       o_w[j].T.astype(BF16), _row(o_b[j]), k_sh, v_sh, pos, inv_col, batch=B,
                       tile=512)
        xt = _peer(xt, _row(peer_norm[i]), peer_wq_t, peer_keys, peer_u_b, peer_vt, layer=i,
                   tile=512, ec=1024)
        if i == n_a - 1:
            xt, k_sh, v_sh = _ple_kv(xt, p[i].reshape(T, -1), _row(ple_norm[i]),
                                     ple_proj[i].astype(BF16), ple_gate_w[i].astype(BF16),
                                     _row(kv_norm), kv_w.astype(BF16), _row(kv_b), pos, inv,
                                     tile=512)
        elif i == depth - 1:
            xt = _ple_final(xt, p[i].reshape(T, -1), _row(ple_norm[i]), ple_proj[i].astype(BF16),
                            ple_gate_w[i].astype(BF16), _row(final_norm), tile=512)
        else:
            raise NotImplementedError("PLE without K/V or final norm")
    return xt.reshape(B, S, D)
```

```python
import functools
import math

import jax
import jax.numpy as jnp
import numpy as np
from jax import lax
from jax.experimental import pallas as pl
from jax.experimental.pallas import tpu as pltpu

F32 = jnp.float32
BF16 = jnp.bfloat16

NORM_EPS = 1e-6
GATED_NORM_EPS = 1e-5
SSM_HEADDIM = 64
SSM_GROUPS = 8
SSM_STATE = 128
CONV_K = 4
SSD_CHUNK = 128
SEQ_PER_STEP = 2
CONV_COLS = 512
HEAD_DIM = 64
N_KV_HEADS = 2
WINDOW = 128
ROT_DIM = HEAD_DIM // 4
ROPE_THETA = 500000.0
PEER_HEADS = 8
N_KEYS = 128
PEER_TOPK = 16

LANES = 128
SUBLANES = 8
VMEM_LIMIT = 56 * 1024 * 1024

_CAND_PAIRS = [(r1, r2) for r1 in range(PEER_TOPK + 1) for r2 in range(PEER_TOPK + 1)
               if (r1 + 1) * (r2 + 1) <= PEER_TOPK + 1]


def _sort_network(n):
    pairs = []
    p = 1
    while p < n:
        k = p
        while k >= 1:
            for j in range(k % p, n - k, 2 * k):
                for i in range(min(k, n - j - k)):
                    if (i + j) // (2 * p) == (i + j + k) // (2 * p):
                        pairs.append((i + j, i + j + k))
            k //= 2
        p *= 2
    return pairs


_SORT_PAIRS = _sort_network(N_KEYS // SUBLANES)


def _params(*sem):
    return pltpu.CompilerParams(dimension_semantics=sem, vmem_limit_bytes=VMEM_LIMIT)


def _const_spec(shape):
    nd = len(shape)
    return pl.BlockSpec(shape, lambda *_: (0,) * nd, pipeline_mode=pl.Buffered(1))


def _rms(x, g, eps):
    return x * lax.rsqrt(jnp.mean(x * x, axis=-1, keepdims=True) + eps) * g


def _dot(a, b, dims=None, precision=None):
    if dims is None:
        dims = (((a.ndim - 1,), (0,)), ((), ()))
    return lax.dot_general(a, b, dims, precision=precision, preferred_element_type=F32)


_NT = (((1,), (1,)), ((), ()))
_TN = (((0,), (0,)), ((), ()))
_HI = lax.Precision.HIGHEST


def _inproj_kernel(x_ref, g_ref, w_ref, cw_ref, cb_ref, z_ref, xs_ref, bc_ref,
                   dt_ref, xb_ref, tail_ref, *, tiles_per_seq):
    tile = x_ref.shape[0]
    d_inner = xs_ref.shape[1]
    conv_dim = tail_ref.shape[1]
    tail = SUBLANES

    @pl.when(pl.program_id(0) % tiles_per_seq == 0)
    def _():
        tail_ref[...] = jnp.zeros(tail_ref.shape, F32)

    h = _rms(x_ref[...], g_ref[...], NORM_EPS).astype(BF16)
    dt_ref[...] = _dot(h, w_ref[:, d_inner + conv_dim:])

    for gi, lo in enumerate(range(0, conv_dim, CONV_COLS)):
        cs = slice(lo, lo + CONV_COLS)
        stage = xb_ref.at[gi % 2]
        stage[0:tail, :] = tail_ref[:, cs]
        stage[tail:tail + tile, :] = _dot(h, w_ref[:, d_inner + lo:d_inner + lo + CONV_COLS])
        if lo < d_inner:
            z_ref[:, cs] = _dot(h, w_ref[:, cs])
        u = stage[...]
        tail_ref[:, cs] = u[tile:tile + tail, :]
        acc = cb_ref[:, cs] + u[tail:, :] * cw_ref[CONV_K - 1:CONV_K, cs]
        for back in range(1, CONV_K):
            acc = acc + pltpu.roll(u, back, 0)[tail:, :] * cw_ref[CONV_K - 1 - back:CONV_K - back, cs]
        act = acc * jax.nn.sigmoid(acc)
        if lo < d_inner:
            xs_ref[:, cs] = act
        else:
            bc_ref[:, lo - d_inner:lo - d_inner + CONV_COLS] = act.astype(BF16)


def _inproj(x, g, w, cw, cb, d_inner, tile, seq):
    T, D = x.shape
    nz, nx = d_inner, cw.shape[1]
    nd = w.shape[1] - nz - nx
    assert nz % CONV_COLS == 0 and nx % CONV_COLS == 0 and seq % tile == 0
    tok = lambda w: pl.BlockSpec((tile, w), lambda i: (i, 0))
    kern = functools.partial(_inproj_kernel, tiles_per_seq=seq // tile)
    return pl.pallas_call(
        kern,
        grid=(T // tile,),
        in_specs=[tok(D), _const_spec((1, D)), _const_spec(w.shape), _const_spec(cw.shape),
                  _const_spec(cb.shape)],
        out_specs=[tok(nz), tok(nz), tok(nx - nz), tok(nd)],
        out_shape=[jax.ShapeDtypeStruct((T, nz), F32), jax.ShapeDtypeStruct((T, nz), F32),
                   jax.ShapeDtypeStruct((T, nx - nz), BF16), jax.ShapeDtypeStruct((T, nd), F32)],
        scratch_shapes=[pltpu.VMEM((2, tile + SUBLANES, CONV_COLS), F32),
                        pltpu.VMEM((SUBLANES, nx), F32)],
        compiler_params=_params("arbitrary"),
        name="ssm_inproj",
    )(x, g, w, cw, cb)


def _ssd_kernel(z_ref, xs_ref, bc_ref, dtr_ref, dtb_ref, alog_ref, dexp_ref, gn_ref,
                expand_ref, tril_ref, y_ref, st_ref, *, d_inner, n_heads):
    L = SSD_CHUNK
    gw = d_inner // SSM_GROUPS
    hpg = n_heads // SSM_GROUPS
    gn_state = SSM_GROUPS * SSM_STATE
    seqs = range(z_ref.shape[0])

    @pl.when(pl.program_id(1) == 0)
    def _():
        st_ref[...] = jnp.zeros(st_ref.shape, F32)

    tril = tril_ref[...]
    expand = expand_ref[...]
    causal = tril > 0.5
    a_cs, a_cs_t, a_cs_x, dt_x = [], [], [], []
    for q in seqs:
        dt_in = dtr_ref[q] + dtb_ref[...]
        dt = jnp.maximum(dt_in, 0.0) + jnp.log1p(jnp.exp(-jnp.abs(dt_in)))
        a = dt * (-jnp.exp(alog_ref[...]))
        acs = _dot(tril, a, precision=_HI)
        a_cs.append(acs)
        a_cs_t.append(acs.T)
        a_cs_x.append(_dot(acs, expand, precision=_HI))
        dt_x.append(_dot(dt, expand, precision=_HI))

    for g in range(SSM_GROUPS):
        lo = g * gw
        for q in seqs:
            xs = xs_ref[q, :, lo:lo + gw]
            bm = bc_ref[q, :, g * SSM_STATE:(g + 1) * SSM_STATE]
            cm = bc_ref[q, :, gn_state + g * SSM_STATE:gn_state + (g + 1) * SSM_STATE]
            xdt = xs * dt_x[q][:, lo:lo + gw]
            cb = _dot(cm, bm, _NT)
            yd = []
            for r in range(hpg):
                hh = g * hpg + r
                seg = a_cs[q][:, hh:hh + 1] - a_cs_t[q][hh:hh + 1, :]
                lmat = jnp.exp(jnp.where(causal, seg, -jnp.inf))
                m = (cb * lmat).astype(BF16)
                yd.append(_dot(m, xdt[:, r * SSM_HEADDIM:(r + 1) * SSM_HEADDIM].astype(BF16)))
            y = jnp.concatenate(yd, axis=1)
            acx = a_cs_x[q][:, lo:lo + gw]
            alx = a_cs_x[q][L - 1:L, lo:lo + gw]
            prev = st_ref[q, g]
            y = y + _dot(cm, prev.astype(BF16)) * jnp.exp(acx)
            xd = (xdt * jnp.exp(alx - acx)).astype(BF16)
            st_ref[q, g] = prev * jnp.exp(alx) + _dot(bm, xd, _TN)
            y = y + xs * dexp_ref[:, lo:lo + gw]
            zg = z_ref[q, :, lo:lo + gw]
            y = y * (zg * jax.nn.sigmoid(zg))
            y = y * lax.rsqrt(jnp.mean(y * y, axis=-1, keepdims=True) + GATED_NORM_EPS)
            y_ref[q, :, lo:lo + gw] = (y * gn_ref[:, lo:lo + gw]).astype(y_ref.dtype)


def _ssd(z, xs, bc, dtr, dtb, alog, dexp, gn, expand, tril, batch):
    T, d_inner = z.shape
    H = dtr.shape[1]
    L = SSD_CHUNK
    seq = T // batch
    assert batch % SEQ_PER_STEP == 0 and seq % L == 0
    blk = lambda w: pl.BlockSpec((SEQ_PER_STEP, L, w), lambda b, c: (b, c, 0))
    per_seq = lambda v: v.reshape(batch, seq, v.shape[1])
    kern = functools.partial(_ssd_kernel, d_inner=d_inner, n_heads=H)
    y = pl.pallas_call(
        kern,
        grid=(batch // SEQ_PER_STEP, seq // L),
        in_specs=[blk(d_inner), blk(d_inner), blk(bc.shape[1]), blk(H),
                  _const_spec(dtb.shape), _const_spec(alog.shape), _const_spec(dexp.shape),
                  _const_spec(gn.shape), _const_spec(expand.shape), _const_spec(tril.shape)],
        out_specs=blk(d_inner),
        out_shape=jax.ShapeDtypeStruct((batch, seq, d_inner), BF16),
        scratch_shapes=[pltpu.VMEM((SEQ_PER_STEP, SSM_GROUPS, SSM_STATE, d_inner // SSM_GROUPS), F32)],
        compiler_params=_params("arbitrary", "arbitrary"),
        name="ssd_scan",
    )(per_seq(z), per_seq(xs), per_seq(bc), per_seq(dtr), dtb, alog, dexp, gn, expand, tril)
    return y.reshape(T, d_inner)


def _outproj_kernel(x_ref, y_ref, w_ref, o_ref):
    o_ref[...] = x_ref[...] + _dot(y_ref[...], w_ref[...])


def _outproj(x, y, w, tile):
    T, D = x.shape
    K = y.shape[1]
    return pl.pallas_call(
        _outproj_kernel,
        grid=(T // tile,),
        in_specs=[pl.BlockSpec((tile, D), lambda i: (i, 0)),
                  pl.BlockSpec((tile, K), lambda i: (i, 0)), _const_spec(w.shape)],
        out_specs=pl.BlockSpec((tile, D), lambda i: (i, 0)),
        out_shape=jax.ShapeDtypeStruct((T, D), F32),
        compiler_params=_params("arbitrary"),
        name="ssm_outproj",
    )(x, y, w)


def _peer_kernel(x_ref, g_ref, wq_ref, keys_ref, u0_ref, un_ref, vt_ref, o_ref,
                 hb_ref, e1_ref, e2_ref, gmin_ref, top_ref,
                 a0_ref, a1_ref, p0_ref, p1_ref, acca_ref, accb_ref):
    g = pl.program_id(1)
    last = pl.num_programs(1) - 1
    tt = x_ref.shape[0]
    ec = un_ref.shape[0] // 2
    rows_per_chunk = ec // N_KEYS
    mxu_cols = 2 * LANES
    n_piece = tt // mxu_cols
    half = N_KEYS
    ntop = PEER_TOPK + 1
    neg_inf = -jnp.inf

    @pl.when(g == 0)
    def _route():
        hb = _rms(x_ref[...], g_ref[...], NORM_EPS).T.astype(BF16)
        hb_ref[...] = hb
        for h in range(PEER_HEADS):
            q = _dot(wq_ref[h * 2 * half:(h + 1) * 2 * half, :], hb).astype(BF16)
            a0_ref[h * N_KEYS:(h + 1) * N_KEYS, :] = _dot(keys_ref[0], q[0:half, :])
            a1_ref[h * N_KEYS:(h + 1) * N_KEYS, :] = _dot(keys_ref[1], q[half:2 * half, :])

        def top_values(s_ref, h, cs):
            v = [s_ref[h * N_KEYS + r * SUBLANES:h * N_KEYS + (r + 1) * SUBLANES, cs]
                 for r in range(N_KEYS // SUBLANES)]
            for i, j in _SORT_PAIRS:
                v[i], v[j] = jnp.maximum(v[i], v[j]), jnp.minimum(v[i], v[j])
            depth = len(v)
            out = []
            for k in range(ntop):
                m = v[0]
                for shift in (4, 2, 1):
                    m = jnp.maximum(m, pltpu.roll(m, shift, 0))
                out.append(m)
                hit = v[0] == m
                for r in range(min(depth, ntop - 1 - k)):
                    v[r] = jnp.where(hit, v[r + 1] if r + 1 < depth else neg_inf, v[r])
            return out

        for tc in range(tt // LANES):
            cs = slice(tc * LANES, (tc + 1) * LANES)
            for c, s_ref in enumerate((a0_ref, a1_ref)):
                for h in range(PEER_HEADS):
                    for r, m in enumerate(top_values(s_ref, h, cs)):
                        top_ref[c, r, h:h + 1, cs] = m[0:1, :]

            a = [top_ref[0, r, :, cs] for r in range(ntop)]
            b = [top_ref[1, r, :, cs] for r in range(ntop)]
            cand = [a[r1] + b[r2] for r1, r2 in _CAND_PAIRS]
            a0, b0 = a[0], b[0]
            m0 = a0 + b0
            z = jnp.zeros(m0.shape, F32)
            for k in range(ntop):
                m = functools.reduce(jnp.maximum, cand)
                if k < PEER_TOPK:
                    z = z + jnp.exp(m - m0)
                if k == PEER_TOPK - 1:
                    v16 = m
                if k < ntop - 1:
                    cand = [jnp.where(cv == m, neg_inf, cv) for cv in cand]
            v17 = m
            zinv = 0.5 / z
            gmin = jnp.exp(0.5 * (v16 + v17) - m0) * zinv
            for h in range(PEER_HEADS):
                hk = slice(h * N_KEYS, (h + 1) * N_KEYS)
                e1_ref[h, tc] = jnp.exp(a0_ref[hk, cs] - a0[h:h + 1, :])
                e2_ref[h, tc] = jnp.exp(a1_ref[hk, cs] - b0[h:h + 1, :]) * zinv[h:h + 1, :]
                gmin_ref[h, tc] = gmin[h:h + 1, :]

            if (tc + 1) % (mxu_cols // LANES) == 0:
                ps = slice((tc + 1) * LANES - mxu_cols, (tc + 1) * LANES)
                a0_ref[:, ps] = _dot(u0_ref[0:ec, :], hb_ref[:, ps])
                a1_ref[:, ps] = _dot(u0_ref[ec:2 * ec, :], hb_ref[:, ps])

        acca_ref[...] = jnp.zeros(acca_ref.shape, F32)
        accb_ref[...] = jnp.zeros(accb_ref.shape, F32)

    def gate_gelu(a_ref, p_ref, chunk, tc):
        i0 = pl.multiple_of(chunk * rows_per_chunk, SUBLANES)
        cs = slice(tc * LANES, (tc + 1) * LANES)
        e18 = [e1_ref[h, tc, pl.ds(i0, rows_per_chunk), :] for h in range(PEER_HEADS)]
        gm = [gmin_ref[h, tc] for h in range(PEER_HEADS)]
        for ii in range(rows_per_chunk):
            rows = slice(ii * N_KEYS, (ii + 1) * N_KEYS)
            w = jnp.zeros((N_KEYS, LANES), F32)
            for h in range(PEER_HEADS):
                gate = e18[h][ii:ii + 1, :] * e2_ref[h, tc]
                w = w + jnp.where(gate >= gm[h], gate, 0.0)
            av = a_ref[rows, cs]
            cdf2 = 1.0 + lax.erf(av * np.float32(math.sqrt(0.5)))
            p_ref[rows, cs] = ((w * av) * cdf2).astype(BF16)

    for par, (a_ref, p_ref, acc_ref) in enumerate(((a0_ref, p0_ref, acca_ref),
                                                   (a1_ref, p1_ref, accb_ref))):
        es = slice(par * ec, (par + 1) * ec)
        for piece in range(n_piece):
            cs = slice(piece * mxu_cols, (piece + 1) * mxu_cols)
            for tc in range(piece * (mxu_cols // LANES), (piece + 1) * (mxu_cols // LANES)):
                gate_gelu(a_ref, p_ref, 2 * g + par, tc)
            acc_ref[:, cs] += _dot(vt_ref[:, es], p_ref[:, cs])
            a_ref[:, cs] = _dot(un_ref[es, :], hb_ref[:, cs])

    @pl.when(g == last)
    def _finish():
        o_ref[...] = x_ref[...] + (acca_ref[...] + accb_ref[...]).T


def _peer(x, g, wq_t, keys, u, vt, layer, tile, ec):
    T, D = x.shape
    E = u.shape[1]
    nchunk = E // ec
    assert E == N_KEYS * N_KEYS and ec == SUBLANES * N_KEYS and tile % LANES == 0
    assert ec == PEER_HEADS * N_KEYS
    assert nchunk % 2 == 0
    ntc = tile // LANES
    nstep = nchunk // 2
    return pl.pallas_call(
        _peer_kernel,
        grid=(T // tile, nstep),
        in_specs=[pl.BlockSpec((tile, D), lambda i, s: (i, 0)),
                  _const_spec((1, D)),
                  pl.BlockSpec((None,) + wq_t.shape[1:], lambda i, s: (layer, 0, 0),
                               pipeline_mode=pl.Buffered(1)),
                  pl.BlockSpec((None,) + keys.shape[1:], lambda i, s: (layer, 0, 0, 0),
                               pipeline_mode=pl.Buffered(1)),
                  pl.BlockSpec((None, 2 * ec, D), lambda i, s: (layer, 0, 0),
                               pipeline_mode=pl.Buffered(1)),
                  pl.BlockSpec((None, 2 * ec, D),
                               lambda i, s: (layer, jnp.minimum(s + 1, nstep - 1), 0)),
                  pl.BlockSpec((None, D, 2 * ec), lambda i, s: (layer, 0, s))],
        out_specs=pl.BlockSpec((tile, D), lambda i, s: (i, 0)),
        out_shape=jax.ShapeDtypeStruct((T, D), F32),
        scratch_shapes=[pltpu.VMEM((D, tile), BF16),
                        pltpu.VMEM((PEER_HEADS, ntc, N_KEYS, LANES), F32),
                        pltpu.VMEM((PEER_HEADS, ntc, N_KEYS, LANES), F32),
                        pltpu.VMEM((PEER_HEADS, ntc, 1, LANES), F32),
                        pltpu.VMEM((2, PEER_TOPK + 1, PEER_HEADS, tile), F32),
                        pltpu.VMEM((ec, tile), F32), pltpu.VMEM((ec, tile), F32),
                        pltpu.VMEM((ec, tile), BF16), pltpu.VMEM((ec, tile), BF16),
                        pltpu.VMEM((D, tile), F32), pltpu.VMEM((D, tile), F32)],
        compiler_params=_params("arbitrary", "arbitrary"),
        name="peer",
    )(x, g, wq_t, keys, u, u, vt)


def _rope_tables(pos_ref, inv_ref):
    ang = pos_ref[...].astype(F32) * inv_ref[...]
    lane = lax.broadcasted_iota(jnp.int32, ang.shape, 1) % HEAD_DIM
    cos = jnp.cos(ang)
    sin = jnp.sin(ang)
    half = ROT_DIM // 2
    sin_lo = jnp.where(lane < half, -sin, 0.0)
    sin_hi = jnp.where((lane >= half) & (lane < ROT_DIM), sin, 0.0)
    return cos, sin_lo, sin_hi


def _rope_apply(t, cos, sin_lo, sin_hi):
    half = ROT_DIM // 2
    up = pltpu.roll(t, LANES - half, 1)
    dn = pltpu.roll(t, half, 1)
    return t * cos + up * sin_lo + dn * sin_hi


def _ple_core(x_ref, p_ref, g_ref, proj_ref, gw_ref):
    x = x_ref[...]
    hn = _rms(x, g_ref[...], NORM_EPS).astype(BF16)
    gate = jax.nn.sigmoid(_dot(hn, gw_ref[...]))
    return x + _dot(p_ref[...].astype(BF16), proj_ref[...]) * gate


def _ple_kv_kernel(x_ref, p_ref, g_ref, proj_ref, gw_ref, kvg_ref, kvw_ref, kvb_ref, pos_ref,
                   inv_ref, o_ref, k_ref, v_ref):
    x2 = _ple_core(x_ref, p_ref, g_ref, proj_ref, gw_ref)
    o_ref[...] = x2
    kv = _dot(_rms(x2, kvg_ref[...], NORM_EPS).astype(BF16), kvw_ref[...]) + kvb_ref[...]
    kvd = k_ref.shape[1]
    cos, sin_lo, sin_hi = _rope_tables(pos_ref, inv_ref)
    k_ref[...] = _rope_apply(kv[:, :kvd], cos, sin_lo, sin_hi).astype(k_ref.dtype)
    v_ref[...] = kv[:, kvd:].astype(v_ref.dtype)


def _ple_final_kernel(x_ref, p_ref, g_ref, proj_ref, gw_ref, fg_ref, o_ref):
    x2 = _ple_core(x_ref, p_ref, g_ref, proj_ref, gw_ref)
    o_ref[...] = _rms(x2, fg_ref[...], NORM_EPS)


def _ple_kv(x, p, g, proj, gw, kvg, kvw, kvb, pos, inv, tile):
    T, D = x.shape
    P = p.shape[1]
    kvd = kvw.shape[1] // 2
    tok = lambda w: pl.BlockSpec((tile, w), lambda i: (i, 0))
    return pl.pallas_call(
        _ple_kv_kernel,
        grid=(T // tile,),
        in_specs=[tok(D), tok(P), _const_spec((1, D)), _const_spec(proj.shape),
                  _const_spec(gw.shape), _const_spec((1, D)), _const_spec(kvw.shape),
                  _const_spec(kvb.shape), tok(1), _const_spec(inv.shape)],
        out_specs=[tok(D), tok(kvd), tok(kvd)],
        out_shape=[jax.ShapeDtypeStruct((T, D), F32), jax.ShapeDtypeStruct((T, kvd), BF16),
                   jax.ShapeDtypeStruct((T, kvd), BF16)],
        compiler_params=_params("arbitrary"),
        name="ple_kv",
    )(x, p, g, proj, gw, kvg, kvw, kvb, pos, inv)


def _ple_final(x, p, g, proj, gw, fg, tile):
    T, D = x.shape
    P = p.shape[1]
    tok = lambda w: pl.BlockSpec((tile, w), lambda i: (i, 0))
    return pl.pallas_call(
        _ple_final_kernel,
        grid=(T // tile,),
        in_specs=[tok(D), tok(P), _const_spec((1, D)), _const_spec(proj.shape),
                  _const_spec(gw.shape), _const_spec((1, D))],
        out_specs=tok(D),
        out_shape=jax.ShapeDtypeStruct((T, D), F32),
        compiler_params=_params("arbitrary"),
        name="ple_final",
    )(x, p, g, proj, gw, fg)


def _attn_kernel(x_ref, g_ref, qw_ref, qb_ref, sink_ref, owt_ref, ob_ref, kc_ref, kp_ref, vc_ref,
                 vp_ref, pos_ref, invc_ref, o_ref, qt_ref, att_ref, *, n_q_heads):
    tile = x_ref.shape[0]
    W = WINDOW
    nblk = tile // W
    q_per_kv = n_q_heads // N_KV_HEADS
    half = ROT_DIM // 2
    scale = HEAD_DIM ** -0.5
    first = pl.program_id(1) == 0

    h = _rms(x_ref[...], g_ref[...], NORM_EPS).astype(BF16)
    q = (_dot(h, qw_ref[...]) + qb_ref[...]) * scale
    qt = q.T
    ang = invc_ref[...] * pos_ref[...].astype(F32)
    cos, sin = jnp.cos(ang), jnp.sin(ang)
    for hh in range(n_q_heads):
        base = hh * HEAD_DIM
        t1 = qt[base:base + half, :]
        t2 = qt[base + half:base + ROT_DIM, :]
        qt_ref[base:base + half, :] = (t1 * cos - t2 * sin).astype(BF16)
        qt_ref[base + half:base + ROT_DIM, :] = (t2 * cos + t1 * sin).astype(BF16)
        qt_ref[base + ROT_DIM:base + HEAD_DIM, :] = qt[base + ROT_DIM:base + HEAD_DIM, :].astype(BF16)

    kj = lax.broadcasted_iota(jnp.int32, (2 * W, W), 0)
    qi = lax.broadcasted_iota(jnp.int32, (2 * W, W), 1)
    first_lo = jnp.where(first, W, 0)
    band = jnp.where((kj > qi) & (kj <= qi + W), 0.0, -jnp.inf)
    band_first = jnp.where(kj >= first_lo, band, -jnp.inf)
    band = jnp.concatenate([band] * q_per_kv, axis=1)
    band_first = jnp.concatenate([band_first] * q_per_kv, axis=1)

    for n in range(nblk):
        cols = slice(n * W, (n + 1) * W)
        if n == 0:
            kprev, vprev = kp_ref[...], vp_ref[...]
        else:
            kprev, vprev = kc_ref[(n - 1) * W:n * W, :], vc_ref[(n - 1) * W:n * W, :]
        kblk = jnp.concatenate([kprev, kc_ref[cols, :]], axis=0)
        vblk = jnp.concatenate([vprev, vc_ref[cols, :]], axis=0)
        bias = band_first if n == 0 else band
        for g in range(N_KV_HEADS):
            ks = kblk[:, g * HEAD_DIM:(g + 1) * HEAD_DIM]
            vs = vblk[:, g * HEAD_DIM:(g + 1) * HEAD_DIM]
            heads = range(g * q_per_kv, (g + 1) * q_per_kv)
            qs = jnp.concatenate([qt_ref[hh * HEAD_DIM:(hh + 1) * HEAD_DIM, cols] for hh in heads],
                                 axis=1)
            sink = jnp.concatenate([jnp.broadcast_to(sink_ref[:, hh:hh + 1], (1, W)) for hh in heads],
                                   axis=1)
            s = _dot(ks, qs) + bias
            m = jnp.maximum(jnp.max(s, axis=0, keepdims=True), sink)
            e = jnp.exp(s - m)
            denom = jnp.sum(e, axis=0, keepdims=True) + jnp.exp(sink - m)
            o = _dot(vs, e.astype(BF16), _TN) / denom
            for r, hh in enumerate(heads):
                att_ref[hh * HEAD_DIM:(hh + 1) * HEAD_DIM, cols] = o[:, r * W:(r + 1) * W].astype(BF16)

    o_ref[...] = x_ref[...] + _dot(owt_ref[...], att_ref[...]).T + ob_ref[...]


def _attn(x, g, qw, qb, sinks, ow_t, ob, k, v, pos, inv_col, batch, tile):
    T, D = x.shape
    kvd = k.shape[1]
    nq = qw.shape[1] // HEAD_DIM
    nt = T // batch // tile
    bpt = tile // WINDOW
    row = lambda b, i: (b * nt + i, 0)
    prev = lambda b, i: (jnp.maximum((b * nt + i) * bpt - 1, 0), 0)
    kern = functools.partial(_attn_kernel, n_q_heads=nq)
    return pl.pallas_call(
        kern,
        grid=(batch, nt),
        in_specs=[pl.BlockSpec((tile, D), row), _const_spec((1, D)), _const_spec(qw.shape),
                  _const_spec(qb.shape), _const_spec(sinks.shape), _const_spec(ow_t.shape),
                  _const_spec(ob.shape),
                  pl.BlockSpec((tile, kvd), row), pl.BlockSpec((WINDOW, kvd), prev),
                  pl.BlockSpec((tile, kvd), row), pl.BlockSpec((WINDOW, kvd), prev),
                  pl.BlockSpec((None, 1, tile), lambda b, i: (b * nt + i, 0, 0)),
                  _const_spec(inv_col.shape)],
        out_specs=pl.BlockSpec((tile, D), row),
        out_shape=jax.ShapeDtypeStruct((T, D), F32),
        scratch_shapes=[pltpu.VMEM((qw.shape[1], tile), BF16),
                        pltpu.VMEM((qw.shape[1], tile), BF16)],
        compiler_params=_params("arbitrary", "arbitrary"),
        name="swa_attn",
    )(x, g, qw, qb, sinks, ow_t, ob, k, k, v, v, pos.reshape(T // tile, 1, tile), inv_col)


def _row(v):
    return v.reshape(1, -1)


def kernel(x, p, positions, ssm_norm, ssm_in_w, ssm_conv_w, ssm_conv_b, ssm_dt_bias, ssm_A_log, ssm_D, ssm_gate_norm, ssm_out_w, kv_norm, kv_w, kv_b, attn_norm, q_w, q_b, sinks, o_w, o_b, peer_norm, peer_q_w, peer_sub_keys, peer_u, peer_v, ple_norm, ple_proj, ple_gate_w, final_norm):
    B, S, D = x.shape
    T = B * S
    depth = p.shape[0]
    n_a = ssm_norm.shape[0]
    H = ssm_D.shape[1]
    d_inner = H * SSM_HEADDIM
    conv_dim = ssm_conv_w.shape[2]

    xt = x.reshape(T, D)
    pos = positions.reshape(T, 1)
    lane = np.arange(LANES) % HEAD_DIM
    inv = np.where(lane < ROT_DIM,
                   ROPE_THETA ** (-(2.0 * (lane % (ROT_DIM // 2))) / ROT_DIM), 0.0)
    inv = jnp.asarray(inv.reshape(1, LANES), F32)
    inv_col = inv[0, :ROT_DIM // 2].reshape(-1, 1)
    expand = jnp.asarray(np.repeat(np.eye(H, dtype=np.float32), SSM_HEADDIM, axis=1))
    tril = jnp.asarray(np.tril(np.ones((SSD_CHUNK, SSD_CHUNK), np.float32)))

    peer_wq_t = peer_q_w.astype(BF16).transpose(0, 2, 1)
    peer_keys = peer_sub_keys.astype(BF16)
    peer_u_b = peer_u.astype(BF16)
    peer_vt = peer_v.astype(BF16).transpose(0, 2, 1)

    k_sh = v_sh = None
    for i in range(depth):
        if i < n_a:
            z, xs, bc, dtr = _inproj(xt, _row(ssm_norm[i]), ssm_in_w[i].astype(BF16), ssm_conv_w[i],
                                     _row(ssm_conv_b[i]), d_inner=d_inner, tile=256, seq=S)
            y = _ssd(z, xs, bc, dtr, _row(ssm_dt_bias[i]), _row(ssm_A_log[i]),
                     _row(jnp.repeat(ssm_D[i], SSM_HEADDIM)), _row(ssm_gate_norm[i]), expand, tril,
                     batch=B)
            xt = _outproj(xt, y, ssm_out_w[i].astype(BF16), tile=512)
        else:
            j = i - n_a
            xt = _attn(xt, _row(attn_norm[j]), q_w[j].astype(BF16), _row(q_b[j]), _row(sinks[j]),
                       o_w[j].T.astype(BF16), _row(o_b[j]), k_sh, v_sh, pos, inv_col, batch=B,
                       tile=512)
        xt = _peer(xt, _row(peer_norm[i]), peer_wq_t, peer_keys, peer_u_b, peer_vt, layer=i,
                   tile=512, ec=1024)
        if i == n_a - 1:
            xt, k_sh, v_sh = _ple_kv(xt, p[i].reshape(T, -1), _row(ple_norm[i]),
                                     ple_proj[i].astype(BF16), ple_gate_w[i].astype(BF16),
                                     _row(kv_norm), kv_w.astype(BF16), _row(kv_b), pos, inv,
                                     tile=512)
        elif i == depth - 1:
            xt = _ple_final(xt, p[i].reshape(T, -1), _row(ple_norm[i]), ple_proj[i].astype(BF16),
                            ple_gate_w[i].astype(BF16), _row(final_norm), tile=512)
        else:
            raise NotImplementedError("PLE without K/V or final norm")
    return xt.reshape(B, S, D)
```

```python
import functools
import math

import jax
import jax.numpy as jnp
import numpy as np
from jax import lax
from jax.experimental import pallas as pl
from jax.experimental.pallas import tpu as pltpu

F32 = jnp.float32
BF16 = jnp.bfloat16

NORM_EPS = 1e-6
GATED_NORM_EPS = 1e-5
SSM_HEADDIM = 64
SSM_GROUPS = 8
SSM_STATE = 128
CONV_K = 4
SSD_CHUNK = 128
SEQ_PER_STEP = 2
CONV_COLS = 512
HEAD_DIM = 64
N_KV_HEADS = 2
WINDOW = 128
ROT_DIM = HEAD_DIM // 4
ROPE_THETA = 500000.0
PEER_HEADS = 8
N_KEYS = 128
PEER_TOPK = 16

LANES = 128
SUBLANES = 8
VMEM_LIMIT = 56 * 1024 * 1024

_CAND_PAIRS = [(r1, r2) for r1 in range(PEER_TOPK + 1) for r2 in range(PEER_TOPK + 1)
               if (r1 + 1) * (r2 + 1) <= PEER_TOPK + 1]


def _sort_network(n):
    pairs = []
    p = 1
    while p < n:
        k = p
        while k >= 1:
            for j in range(k % p, n - k, 2 * k):
                for i in range(min(k, n - j - k)):
                    if (i + j) // (2 * p) == (i + j + k) // (2 * p):
                        pairs.append((i + j, i + j + k))
            k //= 2
        p *= 2
    return pairs


_SORT_PAIRS = _sort_network(N_KEYS // SUBLANES)


def _params(*sem):
    return pltpu.CompilerParams(dimension_semantics=sem, vmem_limit_bytes=VMEM_LIMIT)


def _const_spec(shape):
    nd = len(shape)
    return pl.BlockSpec(shape, lambda *_: (0,) * nd, pipeline_mode=pl.Buffered(1))


def _rms(x, g, eps):
    return x * lax.rsqrt(jnp.mean(x * x, axis=-1, keepdims=True) + eps) * g


def _dot(a, b, dims=None, precision=None):
    if dims is None:
        dims = (((a.ndim - 1,), (0,)), ((), ()))
    return lax.dot_general(a, b, dims, precision=precision, preferred_element_type=F32)


_NT = (((1,), (1,)), ((), ()))
_TN = (((0,), (0,)), ((), ()))
_HI = lax.Precision.HIGHEST


def _inproj_kernel(x_ref, g_ref, w_ref, cw_ref, cb_ref, z_ref, xs_ref, bc_ref,
                   dt_ref, xb_ref, tail_ref, *, tiles_per_seq):
    tile = x_ref.shape[0]
    d_inner = xs_ref.shape[1]
    conv_dim = tail_ref.shape[1]
    tail = SUBLANES

    @pl.when(pl.program_id(0) % tiles_per_seq == 0)
    def _():
        tail_ref[...] = jnp.zeros(tail_ref.shape, F32)

    h = _rms(x_ref[...], g_ref[...], NORM_EPS).astype(BF16)
    dt_ref[...] = _dot(h, w_ref[:, d_inner + conv_dim:])

    for gi, lo in enumerate(range(0, conv_dim, CONV_COLS)):
        cs = slice(lo, lo + CONV_COLS)
        stage = xb_ref.at[gi % 2]
        stage[0:tail, :] = tail_ref[:, cs]
        stage[tail:tail + tile, :] = _dot(h, w_ref[:, d_inner + lo:d_inner + lo + CONV_COLS])
        if lo < d_inner:
            z_ref[:, cs] = _dot(h, w_ref[:, cs])
        u = stage[...]
        tail_ref[:, cs] = u[tile:tile + tail, :]
        acc = cb_ref[:, cs] + u[tail:, :] * cw_ref[CONV_K - 1:CONV_K, cs]
        for back in range(1, CONV_K):
            acc = acc + pltpu.roll(u, back, 0)[tail:, :] * cw_ref[CONV_K - 1 - back:CONV_K - back, cs]
        act = acc * jax.nn.sigmoid(acc)
        if lo < d_inner:
            xs_ref[:, cs] = act
        else:
            bc_ref[:, lo - d_inner:lo - d_inner + CONV_COLS] = act.astype(BF16)


def _inproj(x, g, w, cw, cb, d_inner, tile, seq):
    T, D = x.shape
    nz, nx = d_inner, cw.shape[1]
    nd = w.shape[1] - nz - nx
    assert nz % CONV_COLS == 0 and nx % CONV_COLS == 0 and seq % tile == 0
    tok = lambda w: pl.BlockSpec((tile, w), lambda i: (i, 0))
    kern = functools.partial(_inproj_kernel, tiles_per_seq=seq // tile)
    return pl.pallas_call(
        kern,
        grid=(T // tile,),
        in_specs=[tok(D), _const_spec((1, D)), _const_spec(w.shape), _const_spec(cw.shape),
                  _const_spec(cb.shape)],
        out_specs=[tok(nz), tok(nz), tok(nx - nz), tok(nd)],
        out_shape=[jax.ShapeDtypeStruct((T, nz), F32), jax.ShapeDtypeStruct((T, nz), F32),
                   jax.ShapeDtypeStruct((T, nx - nz), BF16), jax.ShapeDtypeStruct((T, nd), F32)],
        scratch_shapes=[pltpu.VMEM((2, tile + SUBLANES, CONV_COLS), F32),
                        pltpu.VMEM((SUBLANES, nx), F32)],
        compiler_params=_params("arbitrary"),
        name="ssm_inproj",
    )(x, g, w, cw, cb)


def _ssd_kernel(z_ref, xs_ref, bc_ref, dtr_ref, dtb_ref, alog_ref, dexp_ref, gn_ref,
                expand_ref, tril_ref, y_ref, st_ref, *, d_inner, n_heads):
    L = SSD_CHUNK
    gw = d_inner // SSM_GROUPS
    hpg = n_heads // SSM_GROUPS
    gn_state = SSM_GROUPS * SSM_STATE
    seqs = range(z_ref.shape[0])

    @pl.when(pl.program_id(1) == 0)
    def _():
        st_ref[...] = jnp.zeros(st_ref.shape, F32)

    tril = tril_ref[...]
    expand = expand_ref[...]
    causal = tril > 0.5
    a_cs, a_cs_t, a_cs_x, dt_x = [], [], [], []
    for q in seqs:
        dt_in = dtr_ref[q] + dtb_ref[...]
        dt = jnp.maximum(dt_in, 0.0) + jnp.log1p(jnp.exp(-jnp.abs(dt_in)))
        a = dt * (-jnp.exp(alog_ref[...]))
        acs = _dot(tril, a, precision=_HI)
        a_cs.append(acs)
        a_cs_t.append(acs.T)
        a_cs_x.append(_dot(acs, expand, precision=_HI))
        dt_x.append(_dot(dt, expand, precision=_HI))

    for g in range(SSM_GROUPS):
        lo = g * gw
        for q in seqs:
            xs = xs_ref[q, :, lo:lo + gw]
            bm = bc_ref[q, :, g * SSM_STATE:(g + 1) * SSM_STATE]
            cm = bc_ref[q, :, gn_state + g * SSM_STATE:gn_state + (g + 1) * SSM_STATE]
            xdt = xs * dt_x[q][:, lo:lo + gw]
            cb = _dot(cm, bm, _NT)
            yd = []
            for r in range(hpg):
                hh = g * hpg + r
                seg = a_cs[q][:, hh:hh + 1] - a_cs_t[q][hh:hh + 1, :]
                lmat = jnp.exp(jnp.where(causal, seg, -jnp.inf))
                m = (cb * lmat).astype(BF16)
                yd.append(_dot(m, xdt[:, r * SSM_HEADDIM:(r + 1) * SSM_HEADDIM].astype(BF16)))
            y = jnp.concatenate(yd, axis=1)
            acx = a_cs_x[q][:, lo:lo + gw]
            alx = a_cs_x[q][L - 1:L, lo:lo + gw]
            prev = st_ref[q, g]
            y = y + _dot(cm, prev.astype(BF16)) * jnp.exp(acx)
            xd = (xdt * jnp.exp(alx - acx)).astype(BF16)
            st_ref[q, g] = prev * jnp.exp(alx) + _dot(bm, xd, _TN)
            y = y + xs * dexp_ref[:, lo:lo + gw]
            zg = z_ref[q, :, lo:lo + gw]
            y = y * (zg * jax.nn.sigmoid(zg))
            y = y * lax.rsqrt(jnp.mean(y * y, axis=-1, keepdims=True) + GATED_NORM_EPS)
            y_ref[q, :, lo:lo + gw] = (y * gn_ref[:, lo:lo + gw]).astype(y_ref.dtype)


def _ssd(z, xs, bc, dtr, dtb, alog, dexp, gn, expand, tril, batch):
    T, d_inner = z.shape
    H = dtr.shape[1]
    L = SSD_CHUNK
    seq = T // batch
    assert batch % SEQ_PER_STEP == 0 and seq % L == 0
    blk = lambda w: pl.BlockSpec((SEQ_PER_STEP, L, w), lambda b, c: (b, c, 0))
    per_seq = lambda v: v.reshape(batch, seq, v.shape[1])
    kern = functools.partial(_ssd_kernel, d_inner=d_inner, n_heads=H)
    y = pl.pallas_call(
        kern,
        grid=(batch // SEQ_PER_STEP, seq // L),
        in_specs=[blk(d_inner), blk(d_inner), blk(bc.shape[1]), blk(H),
                  _const_spec(dtb.shape), _const_spec(alog.shape), _const_spec(dexp.shape),
                  _const_spec(gn.shape), _const_spec(expand.shape), _const_spec(tril.shape)],
        out_specs=blk(d_inner),
        out_shape=jax.ShapeDtypeStruct((batch, seq, d_inner), BF16),
        scratch_shapes=[pltpu.VMEM((SEQ_PER_STEP, SSM_GROUPS, SSM_STATE, d_inner // SSM_GROUPS), F32)],
        compiler_params=_params("arbitrary", "arbitrary"),
        name="ssd_scan",
    )(per_seq(z), per_seq(xs), per_seq(bc), per_seq(dtr), dtb, alog, dexp, gn, expand, tril)
    return y.reshape(T, d_inner)


def _outproj_kernel(x_ref, y_ref, w_ref, o_ref):
    o_ref[...] = x_ref[...] + _dot(y_ref[...], w_ref[...])


def _outproj(x, y, w, tile):
    T, D = x.shape
    K = y.shape[1]
    return pl.pallas_call(
        _outproj_kernel,
        grid=(T // tile,),
        in_specs=[pl.BlockSpec((tile, D), lambda i: (i, 0)),
                  pl.BlockSpec((tile, K), lambda i: (i, 0)), _const_spec(w.shape)],
        out_specs=pl.BlockSpec((tile, D), lambda i: (i, 0)),
        out_shape=jax.ShapeDtypeStruct((T, D), F32),
        compiler_params=_params("arbitrary"),
        name="ssm_outproj",
    )(x, y, w)


def _peer_kernel(x_ref, g_ref, wq_ref, keys_ref, u0_ref, un_ref, vt_ref, o_ref,
                 hb_ref, e1_ref, e2_ref, gmin_ref, top_ref,
                 a0_ref, a1_ref, p0_ref, p1_ref, acca_ref, accb_ref):
    g = pl.program_id(1)
    last = pl.num_programs(1) - 1
    tt = x_ref.shape[0]
    ec = un_ref.shape[0] // 2
    rows_per_chunk = ec // N_KEYS
    mxu_cols = 2 * LANES
    n_piece = tt // mxu_cols
    half = N_KEYS
    ntop = PEER_TOPK + 1
    neg_inf = -jnp.inf

    @pl.when(g == 0)
    def _route():
        hb = _rms(x_ref[...], g_ref[...], NORM_EPS).T.astype(BF16)
        hb_ref[...] = hb
        for h in range(PEER_HEADS):
            q = _dot(wq_ref[h * 2 * half:(h + 1) * 2 * half, :], hb).astype(BF16)
            a0_ref[h * N_KEYS:(h + 1) * N_KEYS, :] = _dot(keys_ref[0], q[0:half, :])
            a1_ref[h * N_KEYS:(h + 1) * N_KEYS, :] = _dot(keys_ref[1], q[half:2 * half, :])

        def top_values(s_ref, h, cs):
            v = [s_ref[h * N_KEYS + r * SUBLANES:h * N_KEYS + (r + 1) * SUBLANES, cs]
                 for r in range(N_KEYS // SUBLANES)]
            for i, j in _SORT_PAIRS:
                v[i], v[j] = jnp.maximum(v[i], v[j]), jnp.minimum(v[i], v[j])
            depth = len(v)
            out = []
            for k in range(ntop):
                m = v[0]
                for shift in (4, 2, 1):
                    m = jnp.maximum(m, pltpu.roll(m, shift, 0))
                out.append(m)
                hit = v[0] == m
                for r in range(min(depth, ntop - 1 - k)):
                    v[r] = jnp.where(hit, v[r + 1] if r + 1 < depth else neg_inf, v[r])
            return out

        for tc in range(tt // LANES):
            cs = slice(tc * LANES, (tc + 1) * LANES)
            for c, s_ref in enumerate((a0_ref, a1_ref)):
                for h in range(PEER_HEADS):
                    for r, m in enumerate(top_values(s_ref, h, cs)):
                        top_ref[c, r, h:h + 1, cs] = m[0:1, :]

            a = [top_ref[0, r, :, cs] for r in range(ntop)]
            b = [top_ref[1, r, :, cs] for r in range(ntop)]
            cand = [a[r1] + b[r2] for r1, r2 in _CAND_PAIRS]
            a0, b0 = a[0], b[0]
            m0 = a0 + b0
            z = jnp.zeros(m0.shape, F32)
            for k in range(ntop):
                m = functools.reduce(jnp.maximum, cand)
                if k < PEER_TOPK:
                    z = z + jnp.exp(m - m0)
                if k == PEER_TOPK - 1:
                    v16 = m
                if k < ntop - 1:
                    cand = [jnp.where(cv == m, neg_inf, cv) for cv in cand]
            v17 = m
            zinv = np.float32(math.sqrt(0.5)) / z
            gmin = jnp.exp(0.5 * (v16 + v17) - m0) * zinv
            for h in range(PEER_HEADS):
                hk = slice(h * N_KEYS, (h + 1) * N_KEYS)
                e1_ref[h, tc] = jnp.exp(a0_ref[hk, cs] - a0[h:h + 1, :])
                e2_ref[h, tc] = jnp.exp(a1_ref[hk, cs] - b0[h:h + 1, :]) * zinv[h:h + 1, :]
                gmin_ref[h, tc] = gmin[h:h + 1, :]

            if (tc + 1) % (mxu_cols // LANES) == 0:
                ps = slice((tc + 1) * LANES - mxu_cols, (tc + 1) * LANES)
                a0_ref[:, ps] = _dot(u0_ref[0:ec, :], hb_ref[:, ps])
                a1_ref[:, ps] = _dot(u0_ref[ec:2 * ec, :], hb_ref[:, ps])

        acca_ref[...] = jnp.zeros(acca_ref.shape, F32)
        accb_ref[...] = jnp.zeros(accb_ref.shape, F32)

    def gate_gelu(a_ref, p_ref, chunk, tc):
        i0 = pl.multiple_of(chunk * rows_per_chunk, SUBLANES)
        cs = slice(tc * LANES, (tc + 1) * LANES)
        e18 = [e1_ref[h, tc, pl.ds(i0, rows_per_chunk), :] for h in range(PEER_HEADS)]
        gm = [gmin_ref[h, tc] for h in range(PEER_HEADS)]
        for ii in range(rows_per_chunk):
            rows = slice(ii * N_KEYS, (ii + 1) * N_KEYS)
            w = jnp.zeros((N_KEYS, LANES), F32)
            for h in range(PEER_HEADS):
                gate = e18[h][ii:ii + 1, :] * e2_ref[h, tc]
                w = w + jnp.where(gate >= gm[h], gate, 0.0)
            av = a_ref[rows, cs]
            p_ref[rows, cs] = ((w * av) * (1.0 + lax.erf(av))).astype(BF16)

    for par, (a_ref, p_ref, acc_ref) in enumerate(((a0_ref, p0_ref, acca_ref),
                                                   (a1_ref, p1_ref, accb_ref))):
        es = slice(par * ec, (par + 1) * ec)
        for piece in range(n_piece):
            cs = slice(piece * mxu_cols, (piece + 1) * mxu_cols)
            for tc in range(piece * (mxu_cols // LANES), (piece + 1) * (mxu_cols // LANES)):
                gate_gelu(a_ref, p_ref, 2 * g + par, tc)
            acc_ref[:, cs] += _dot(vt_ref[:, es], p_ref[:, cs])
            a_ref[:, cs] = _dot(un_ref[es, :], hb_ref[:, cs])

    @pl.when(g == last)
    def _finish():
        o_ref[...] = x_ref[...] + (acca_ref[...] + accb_ref[...]).T


def _peer(x, g, wq_t, keys, u, vt, layer, tile, ec):
    T, D = x.shape
    E = u.shape[1]
    nchunk = E // ec
    assert E == N_KEYS * N_KEYS and ec == SUBLANES * N_KEYS and tile % LANES == 0
    assert ec == PEER_HEADS * N_KEYS
    assert nchunk % 2 == 0
    ntc = tile // LANES
    nstep = nchunk // 2
    return pl.pallas_call(
        _peer_kernel,
        grid=(T // tile, nstep),
        in_specs=[pl.BlockSpec((tile, D), lambda i, s: (i, 0)),
                  _const_spec((1, D)),
                  pl.BlockSpec((None,) + wq_t.shape[1:], lambda i, s: (layer, 0, 0),
                               pipeline_mode=pl.Buffered(1)),
                  pl.BlockSpec((None,) + keys.shape[1:], lambda i, s: (layer, 0, 0, 0),
                               pipeline_mode=pl.Buffered(1)),
                  pl.BlockSpec((None, 2 * ec, D), lambda i, s: (layer, 0, 0),
                               pipeline_mode=pl.Buffered(1)),
                  pl.BlockSpec((None, 2 * ec, D),
                               lambda i, s: (layer, jnp.minimum(s + 1, nstep - 1), 0)),
                  pl.BlockSpec((None, D, 2 * ec), lambda i, s: (layer, 0, s))],
        out_specs=pl.BlockSpec((tile, D), lambda i, s: (i, 0)),
        out_shape=jax.ShapeDtypeStruct((T, D), F32),
        scratch_shapes=[pltpu.VMEM((D, tile), BF16),
                        pltpu.VMEM((PEER_HEADS, ntc, N_KEYS, LANES), F32),
                        pltpu.VMEM((PEER_HEADS, ntc, N_KEYS, LANES), F32),
                        pltpu.VMEM((PEER_HEADS, ntc, 1, LANES), F32),
                        pltpu.VMEM((2, PEER_TOPK + 1, PEER_HEADS, tile), F32),
                        pltpu.VMEM((ec, tile), F32), pltpu.VMEM((ec, tile), F32),
                        pltpu.VMEM((ec, tile), BF16), pltpu.VMEM((ec, tile), BF16),
                        pltpu.VMEM((D, tile), F32), pltpu.VMEM((D, tile), F32)],
        compiler_params=_params("arbitrary", "arbitrary"),
        name="peer",
    )(x, g, wq_t, keys, u, u, vt)


def _rope_tables(pos_ref, inv_ref):
    ang = pos_ref[...].astype(F32) * inv_ref[...]
    lane = lax.broadcasted_iota(jnp.int32, ang.shape, 1) % HEAD_DIM
    cos = jnp.cos(ang)
    sin = jnp.sin(ang)
    half = ROT_DIM // 2
    sin_lo = jnp.where(lane < half, -sin, 0.0)
    sin_hi = jnp.where((lane >= half) & (lane < ROT_DIM), sin, 0.0)
    return cos, sin_lo, sin_hi


def _rope_apply(t, cos, sin_lo, sin_hi):
    half = ROT_DIM // 2
    up = pltpu.roll(t, LANES - half, 1)
    dn = pltpu.roll(t, half, 1)
    return t * cos + up * sin_lo + dn * sin_hi


def _ple_core(x_ref, p_ref, g_ref, proj_ref, gw_ref):
    x = x_ref[...]
    hn = _rms(x, g_ref[...], NORM_EPS).astype(BF16)
    gate = jax.nn.sigmoid(_dot(hn, gw_ref[...]))
    return x + _dot(p_ref[...].astype(BF16), proj_ref[...]) * gate


def _ple_kv_kernel(x_ref, p_ref, g_ref, proj_ref, gw_ref, kvg_ref, kvw_ref, kvb_ref, pos_ref,
                   inv_ref, o_ref, k_ref, v_ref):
    x2 = _ple_core(x_ref, p_ref, g_ref, proj_ref, gw_ref)
    o_ref[...] = x2
    kv = _dot(_rms(x2, kvg_ref[...], NORM_EPS).astype(BF16), kvw_ref[...]) + kvb_ref[...]
    kvd = k_ref.shape[1]
    cos, sin_lo, sin_hi = _rope_tables(pos_ref, inv_ref)
    k_ref[...] = _rope_apply(kv[:, :kvd], cos, sin_lo, sin_hi).astype(k_ref.dtype)
    v_ref[...] = kv[:, kvd:].astype(v_ref.dtype)


def _ple_final_kernel(x_ref, p_ref, g_ref, proj_ref, gw_ref, fg_ref, o_ref):
    x2 = _ple_core(x_ref, p_ref, g_ref, proj_ref, gw_ref)
    o_ref[...] = _rms(x2, fg_ref[...], NORM_EPS)


def _ple_kv(x, p, g, proj, gw, kvg, kvw, kvb, pos, inv, tile):
    T, D = x.shape
    P = p.shape[1]
    kvd = kvw.shape[1] // 2
    tok = lambda w: pl.BlockSpec((tile, w), lambda i: (i, 0))
    return pl.pallas_call(
        _ple_kv_kernel,
        grid=(T // tile,),
        in_specs=[tok(D), tok(P), _const_spec((1, D)), _const_spec(proj.shape),
                  _const_spec(gw.shape), _const_spec((1, D)), _const_spec(kvw.shape),
                  _const_spec(kvb.shape), tok(1), _const_spec(inv.shape)],
        out_specs=[tok(D), tok(kvd), tok(kvd)],
        out_shape=[jax.ShapeDtypeStruct((T, D), F32), jax.ShapeDtypeStruct((T, kvd), BF16),
                   jax.ShapeDtypeStruct((T, kvd), BF16)],
        compiler_params=_params("arbitrary"),
        name="ple_kv",
    )(x, p, g, proj, gw, kvg, kvw, kvb, pos, inv)


def _ple_final(x, p, g, proj, gw, fg, tile):
    T, D = x.shape
    P = p.shape[1]
    tok = lambda w: pl.BlockSpec((tile, w), lambda i: (i, 0))
    return pl.pallas_call(
        _ple_final_kernel,
        grid=(T // tile,),
        in_specs=[tok(D), tok(P), _const_spec((1, D)), _const_spec(proj.shape),
                  _const_spec(gw.shape), _const_spec((1, D))],
        out_specs=tok(D),
        out_shape=jax.ShapeDtypeStruct((T, D), F32),
        compiler_params=_params("arbitrary"),
        name="ple_final",
    )(x, p, g, proj, gw, fg)


def _attn_kernel(x_ref, g_ref, qw_ref, qb_ref, sink_ref, owt_ref, ob_ref, kc_ref, kp_ref, vc_ref,
                 vp_ref, pos_ref, invc_ref, o_ref, qt_ref, att_ref, *, n_q_heads):
    tile = x_ref.shape[0]
    W = WINDOW
    nblk = tile // W
    q_per_kv = n_q_heads // N_KV_HEADS
    half = ROT_DIM // 2
    scale = HEAD_DIM ** -0.5
    first = pl.program_id(1) == 0

    h = _rms(x_ref[...], g_ref[...], NORM_EPS).astype(BF16)
    q = (_dot(h, qw_ref[...]) + qb_ref[...]) * scale
    qt = q.T
    ang = invc_ref[...] * pos_ref[...].astype(F32)
    cos, sin = jnp.cos(ang), jnp.sin(ang)
    for hh in range(n_q_heads):
        base = hh * HEAD_DIM
        t1 = qt[base:base + half, :]
        t2 = qt[base + half:base + ROT_DIM, :]
        qt_ref[base:base + half, :] = (t1 * cos - t2 * sin).astype(BF16)
        qt_ref[base + half:base + ROT_DIM, :] = (t2 * cos + t1 * sin).astype(BF16)
        qt_ref[base + ROT_DIM:base + HEAD_DIM, :] = qt[base + ROT_DIM:base + HEAD_DIM, :].astype(BF16)

    kj = lax.broadcasted_iota(jnp.int32, (2 * W, W), 0)
    qi = lax.broadcasted_iota(jnp.int32, (2 * W, W), 1)
    first_lo = jnp.where(first, W, 0)
    band = jnp.where((kj > qi) & (kj <= qi + W), 0.0, -jnp.inf)
    band_first = jnp.where(kj >= first_lo, band, -jnp.inf)
    band = jnp.concatenate([band] * q_per_kv, axis=1)
    band_first = jnp.concatenate([band_first] * q_per_kv, axis=1)

    for n in range(nblk):
        cols = slice(n * W, (n + 1) * W)
        if n == 0:
            kprev, vprev = kp_ref[...], vp_ref[...]
        else:
            kprev, vprev = kc_ref[(n - 1) * W:n * W, :], vc_ref[(n - 1) * W:n * W, :]
        kblk = jnp.concatenate([kprev, kc_ref[cols, :]], axis=0)
        vblk = jnp.concatenate([vprev, vc_ref[cols, :]], axis=0)
        bias = band_first if n == 0 else band
        for g in range(N_KV_HEADS):
            ks = kblk[:, g * HEAD_DIM:(g + 1) * HEAD_DIM]
            vs = vblk[:, g * HEAD_DIM:(g + 1) * HEAD_DIM]
            heads = range(g * q_per_kv, (g + 1) * q_per_kv)
            qs = jnp.concatenate([qt_ref[hh * HEAD_DIM:(hh + 1) * HEAD_DIM, cols] for hh in heads],
                                 axis=1)
            sink = jnp.concatenate([jnp.broadcast_to(sink_ref[:, hh:hh + 1], (1, W)) for hh in heads],
                                   axis=1)
            s = _dot(ks, qs) + bias
            m = jnp.maximum(jnp.max(s, axis=0, keepdims=True), sink)
            e = jnp.exp(s - m)
            denom = jnp.sum(e, axis=0, keepdims=True) + jnp.exp(sink - m)
            o = _dot(vs, e.astype(BF16), _TN) / denom
            for r, hh in enumerate(heads):
                att_ref[hh * HEAD_DIM:(hh + 1) * HEAD_DIM, cols] = o[:, r * W:(r + 1) * W].astype(BF16)

    o_ref[...] = x_ref[...] + _dot(owt_ref[...], att_ref[...]).T + ob_ref[...]


def _attn(x, g, qw, qb, sinks, ow_t, ob, k, v, pos, inv_col, batch, tile):
    T, D = x.shape
    kvd = k.shape[1]
    nq = qw.shape[1] // HEAD_DIM
    nt = T // batch // tile
    bpt = tile // WINDOW
    row = lambda b, i: (b * nt + i, 0)
    prev = lambda b, i: (jnp.maximum((b * nt + i) * bpt - 1, 0), 0)
    kern = functools.partial(_attn_kernel, n_q_heads=nq)
    return pl.pallas_call(
        kern,
        grid=(batch, nt),
        in_specs=[pl.BlockSpec((tile, D), row), _const_spec((1, D)), _const_spec(qw.shape),
                  _const_spec(qb.shape), _const_spec(sinks.shape), _const_spec(ow_t.shape),
                  _const_spec(ob.shape),
                  pl.BlockSpec((tile, kvd), row), pl.BlockSpec((WINDOW, kvd), prev),
                  pl.BlockSpec((tile, kvd), row), pl.BlockSpec((WINDOW, kvd), prev),
                  pl.BlockSpec((None, 1, tile), lambda b, i: (b * nt + i, 0, 0)),
                  _const_spec(inv_col.shape)],
        out_specs=pl.BlockSpec((tile, D), row),
        out_shape=jax.ShapeDtypeStruct((T, D), F32),
        scratch_shapes=[pltpu.VMEM((qw.shape[1], tile), BF16),
                        pltpu.VMEM((qw.shape[1], tile), BF16)],
        compiler_params=_params("arbitrary", "arbitrary"),
        name="swa_attn",
    )(x, g, qw, qb, sinks, ow_t, ob, k, k, v, v, pos.reshape(T // tile, 1, tile), inv_col)


def _row(v):
    return v.reshape(1, -1)


def kernel(x, p, positions, ssm_norm, ssm_in_w, ssm_conv_w, ssm_conv_b, ssm_dt_bias, ssm_A_log, ssm_D, ssm_gate_norm, ssm_out_w, kv_norm, kv_w, kv_b, attn_norm, q_w, q_b, sinks, o_w, o_b, peer_norm, peer_q_w, peer_sub_keys, peer_u, peer_v, ple_norm, ple_proj, ple_gate_w, final_norm):
    B, S, D = x.shape
    T = B * S
    depth = p.shape[0]
    n_a = ssm_norm.shape[0]
    H = ssm_D.shape[1]
    d_inner = H * SSM_HEADDIM
    conv_dim = ssm_conv_w.shape[2]

    xt = x.reshape(T, D)
    pos = positions.reshape(T, 1)
    lane = np.arange(LANES) % HEAD_DIM
    inv = np.where(lane < ROT_DIM,
                   ROPE_THETA ** (-(2.0 * (lane % (ROT_DIM // 2))) / ROT_DIM), 0.0)
    inv = jnp.asarray(inv.reshape(1, LANES), F32)
    inv_col = inv[0, :ROT_DIM // 2].reshape(-1, 1)
    expand = jnp.asarray(np.repeat(np.eye(H, dtype=np.float32), SSM_HEADDIM, axis=1))
    tril = jnp.asarray(np.tril(np.ones((SSD_CHUNK, SSD_CHUNK), np.float32)))

    peer_wq_t = peer_q_w.astype(BF16).transpose(0, 2, 1)
    peer_keys = peer_sub_keys.astype(BF16)
    peer_u_b = (peer_u * np.float32(math.sqrt(0.5))).astype(BF16)
    peer_vt = peer_v.astype(BF16).transpose(0, 2, 1)

    k_sh = v_sh = None
    for i in range(depth):
        if i < n_a:
            z, xs, bc, dtr = _inproj(xt, _row(ssm_norm[i]), ssm_in_w[i].astype(BF16), ssm_conv_w[i],
                                     _row(ssm_conv_b[i]), d_inner=d_inner, tile=256, seq=S)
            y = _ssd(z, xs, bc, dtr, _row(ssm_dt_bias[i]), _row(ssm_A_log[i]),
                     _row(jnp.repeat(ssm_D[i], SSM_HEADDIM)), _row(ssm_gate_norm[i]), expand, tril,
                     batch=B)
            xt = _outproj(xt, y, ssm_out_w[i].astype(BF16), tile=512)
        else:
            j = i - n_a
            xt = _attn(xt, _row(attn_norm[j]), q_w[j].astype(BF16), _row(q_b[j]), _row(sinks[j]),
                       o_w[j].T.astype(BF16), _row(o_b[j]), k_sh, v_sh, pos, inv_col, batch=B,
                       tile=512)
        xt = _peer(xt, _row(peer_norm[i]), peer_wq_t, peer_keys, peer_u_b, peer_vt, layer=i,
                   tile=512, ec=1024)
        if i == n_a - 1:
            xt, k_sh, v_sh = _ple_kv(xt, p[i].reshape(T, -1), _row(ple_norm[i]),
                                     ple_proj[i].astype(BF16), ple_gate_w[i].astype(BF16),
                                     _row(kv_norm), kv_w.astype(BF16), _row(kv_b), pos, inv,
                                     tile=512)
        elif i == depth - 1:
            xt = _ple_final(xt, p[i].reshape(T, -1), _row(ple_norm[i]), ple_proj[i].astype(BF16),
                            ple_gate_w[i].astype(BF16), _row(final_norm), tile=512)
        else:
            raise NotImplementedError("PLE without K/V or final norm")
    return xt.reshape(B, S, D)
```

```python
import functools
import math

import jax
import jax.numpy as jnp
import numpy as np
from jax import lax
from jax.experimental import pallas as pl
from jax.experimental.pallas import tpu as pltpu

F32 = jnp.float32
BF16 = jnp.bfloat16

NORM_EPS = 1e-6
GATED_NORM_EPS = 1e-5
SSM_HEADDIM = 64
SSM_GROUPS = 8
SSM_STATE = 128
CONV_K = 4
SSD_CHUNK = 128
SEQ_PER_STEP = 2
CONV_COLS = 512
HEAD_DIM = 64
N_KV_HEADS = 2
WINDOW = 128
ROT_DIM = HEAD_DIM // 4
ROPE_THETA = 500000.0
PEER_HEADS = 8
N_KEYS = 128
PEER_TOPK = 16

LANES = 128
SUBLANES = 8
VMEM_LIMIT = 60 * 1024 * 1024

_CAND_PAIRS = [(r1, r2) for r1 in range(PEER_TOPK + 1) for r2 in range(PEER_TOPK + 1)
               if (r1 + 1) * (r2 + 1) <= PEER_TOPK + 1]


def _sort_network(n):
    pairs = []
    p = 1
    while p < n:
        k = p
        while k >= 1:
            for j in range(k % p, n - k, 2 * k):
                for i in range(min(k, n - j - k)):
                    if (i + j) // (2 * p) == (i + j + k) // (2 * p):
                        pairs.append((i + j, i + j + k))
            k //= 2
        p *= 2
    return pairs


_SORT_PAIRS = _sort_network(N_KEYS // SUBLANES)


def _params(*sem):
    return pltpu.CompilerParams(dimension_semantics=sem, vmem_limit_bytes=VMEM_LIMIT)


def _const_spec(shape):
    nd = len(shape)
    return pl.BlockSpec(shape, lambda *_: (0,) * nd, pipeline_mode=pl.Buffered(1))


def _rms(x, g, eps):
    return x * lax.rsqrt(jnp.mean(x * x, axis=-1, keepdims=True) + eps) * g


def _dot(a, b, dims=None, precision=None):
    if dims is None:
        dims = (((a.ndim - 1,), (0,)), ((), ()))
    return lax.dot_general(a, b, dims, precision=precision, preferred_element_type=F32)


_NT = (((1,), (1,)), ((), ()))
_TN = (((0,), (0,)), ((), ()))
_HI = lax.Precision.HIGHEST


def _inproj_kernel(x_ref, g_ref, w_ref, cw_ref, cb_ref, z_ref, xs_ref, bc_ref,
                   dt_ref, xb_ref, tail_ref, *, tiles_per_seq):
    tile = x_ref.shape[0]
    d_inner = xs_ref.shape[1]
    conv_dim = tail_ref.shape[1]
    tail = SUBLANES

    @pl.when(pl.program_id(0) % tiles_per_seq == 0)
    def _():
        tail_ref[...] = jnp.zeros(tail_ref.shape, F32)

    h = _rms(x_ref[...], g_ref[...], NORM_EPS).astype(BF16)
    dt_ref[...] = _dot(h, w_ref[:, d_inner + conv_dim:])

    for gi, lo in enumerate(range(0, conv_dim, CONV_COLS)):
        cs = slice(lo, lo + CONV_COLS)
        stage = xb_ref.at[gi % 2]
        stage[0:tail, :] = tail_ref[:, cs]
        stage[tail:tail + tile, :] = _dot(h, w_ref[:, d_inner + lo:d_inner + lo + CONV_COLS])
        if lo < d_inner:
            z_ref[:, cs] = _dot(h, w_ref[:, cs])
        u = stage[...]
        tail_ref[:, cs] = u[tile:tile + tail, :]
        acc = cb_ref[:, cs] + u[tail:, :] * cw_ref[CONV_K - 1:CONV_K, cs]
        for back in range(1, CONV_K):
            acc = acc + pltpu.roll(u, back, 0)[tail:, :] * cw_ref[CONV_K - 1 - back:CONV_K - back, cs]
        act = acc * jax.nn.sigmoid(acc)
        if lo < d_inner:
            xs_ref[:, cs] = act
        else:
            bc_ref[:, lo - d_inner:lo - d_inner + CONV_COLS] = act.astype(BF16)


def _inproj(x, g, w, cw, cb, d_inner, tile, seq):
    T, D = x.shape
    nz, nx = d_inner, cw.shape[1]
    nd = w.shape[1] - nz - nx
    assert nz % CONV_COLS == 0 and nx % CONV_COLS == 0 and seq % tile == 0
    tok = lambda w: pl.BlockSpec((tile, w), lambda i: (i, 0))
    kern = functools.partial(_inproj_kernel, tiles_per_seq=seq // tile)
    return pl.pallas_call(
        kern,
        grid=(T // tile,),
        in_specs=[tok(D), _const_spec((1, D)), _const_spec(w.shape), _const_spec(cw.shape),
                  _const_spec(cb.shape)],
        out_specs=[tok(nz), tok(nz), tok(nx - nz), tok(nd)],
        out_shape=[jax.ShapeDtypeStruct((T, nz), F32), jax.ShapeDtypeStruct((T, nz), F32),
                   jax.ShapeDtypeStruct((T, nx - nz), BF16), jax.ShapeDtypeStruct((T, nd), F32)],
        scratch_shapes=[pltpu.VMEM((2, tile + SUBLANES, CONV_COLS), F32),
                        pltpu.VMEM((SUBLANES, nx), F32)],
        compiler_params=_params("arbitrary"),
        name="ssm_inproj",
    )(x, g, w, cw, cb)


def _ssd_kernel(z_ref, xs_ref, bc_ref, dtr_ref, dtb_ref, alog_ref, dexp_ref, gn_ref,
                expand_ref, tril_ref, y_ref, st_ref, *, d_inner, n_heads):
    L = SSD_CHUNK
    gw = d_inner // SSM_GROUPS
    hpg = n_heads // SSM_GROUPS
    gn_state = SSM_GROUPS * SSM_STATE
    seqs = range(z_ref.shape[0])

    @pl.when(pl.program_id(1) == 0)
    def _():
        st_ref[...] = jnp.zeros(st_ref.shape, F32)

    tril = tril_ref[...]
    expand = expand_ref[...]
    causal = tril > 0.5
    a_cs, a_cs_t, a_cs_x, dt_x = [], [], [], []
    for q in seqs:
        dt_in = dtr_ref[q] + dtb_ref[...]
        dt = jnp.maximum(dt_in, 0.0) + jnp.log1p(jnp.exp(-jnp.abs(dt_in)))
        a = dt * (-jnp.exp(alog_ref[...]))
        acs = _dot(tril, a, precision=_HI)
        a_cs.append(acs)
        a_cs_t.append(acs.T)
        a_cs_x.append(_dot(acs, expand, precision=_HI))
        dt_x.append(_dot(dt, expand, precision=_HI))

    for g in range(SSM_GROUPS):
        lo = g * gw
        for q in seqs:
            xs = xs_ref[q, :, lo:lo + gw]
            bm = bc_ref[q, :, g * SSM_STATE:(g + 1) * SSM_STATE]
            cm = bc_ref[q, :, gn_state + g * SSM_STATE:gn_state + (g + 1) * SSM_STATE]
            xdt = xs * dt_x[q][:, lo:lo + gw]
            cb = _dot(cm, bm, _NT)
            yd = []
            for r in range(hpg):
                hh = g * hpg + r
                seg = a_cs[q][:, hh:hh + 1] - a_cs_t[q][hh:hh + 1, :]
                lmat = jnp.exp(jnp.where(causal, seg, -jnp.inf))
                m = (cb * lmat).astype(BF16)
                yd.append(_dot(m, xdt[:, r * SSM_HEADDIM:(r + 1) * SSM_HEADDIM].astype(BF16)))
            y = jnp.concatenate(yd, axis=1)
            acx = a_cs_x[q][:, lo:lo + gw]
            alx = a_cs_x[q][L - 1:L, lo:lo + gw]
            prev = st_ref[q, g]
            y = y + _dot(cm, prev.astype(BF16)) * jnp.exp(acx)
            xd = (xdt * jnp.exp(alx - acx)).astype(BF16)
            st_ref[q, g] = prev * jnp.exp(alx) + _dot(bm, xd, _TN)
            y = y + xs * dexp_ref[:, lo:lo + gw]
            zg = z_ref[q, :, lo:lo + gw]
            y = y * (zg * jax.nn.sigmoid(zg))
            y = y * lax.rsqrt(jnp.mean(y * y, axis=-1, keepdims=True) + GATED_NORM_EPS)
            y_ref[q, :, lo:lo + gw] = (y * gn_ref[:, lo:lo + gw]).astype(y_ref.dtype)


def _ssd(z, xs, bc, dtr, dtb, alog, dexp, gn, expand, tril, batch):
    T, d_inner = z.shape
    H = dtr.shape[1]
    L = SSD_CHUNK
    seq = T // batch
    assert batch % SEQ_PER_STEP == 0 and seq % L == 0
    blk = lambda w: pl.BlockSpec((SEQ_PER_STEP, L, w), lambda b, c: (b, c, 0))
    per_seq = lambda v: v.reshape(batch, seq, v.shape[1])
    kern = functools.partial(_ssd_kernel, d_inner=d_inner, n_heads=H)
    y = pl.pallas_call(
        kern,
        grid=(batch // SEQ_PER_STEP, seq // L),
        in_specs=[blk(d_inner), blk(d_inner), blk(bc.shape[1]), blk(H),
                  _const_spec(dtb.shape), _const_spec(alog.shape), _const_spec(dexp.shape),
                  _const_spec(gn.shape), _const_spec(expand.shape), _const_spec(tril.shape)],
        out_specs=blk(d_inner),
        out_shape=jax.ShapeDtypeStruct((batch, seq, d_inner), BF16),
        scratch_shapes=[pltpu.VMEM((SEQ_PER_STEP, SSM_GROUPS, SSM_STATE, d_inner // SSM_GROUPS), F32)],
        compiler_params=_params("arbitrary", "arbitrary"),
        name="ssd_scan",
    )(per_seq(z), per_seq(xs), per_seq(bc), per_seq(dtr), dtb, alog, dexp, gn, expand, tril)
    return y.reshape(T, d_inner)


def _outproj_kernel(x_ref, y_ref, w_ref, o_ref):
    o_ref[...] = x_ref[...] + _dot(y_ref[...], w_ref[...])


def _outproj(x, y, w, tile):
    T, D = x.shape
    K = y.shape[1]
    return pl.pallas_call(
        _outproj_kernel,
        grid=(T // tile,),
        in_specs=[pl.BlockSpec((tile, D), lambda i: (i, 0)),
                  pl.BlockSpec((tile, K), lambda i: (i, 0)), _const_spec(w.shape)],
        out_specs=pl.BlockSpec((tile, D), lambda i: (i, 0)),
        out_shape=jax.ShapeDtypeStruct((T, D), F32),
        compiler_params=_params("arbitrary"),
        name="ssm_outproj",
    )(x, y, w)


def _peer_kernel(*refs, with_kv):
    (x_ref, g_ref, wq_ref, keys_ref, u0_ref, un_ref, vt_ref,
     ple_ref, pg_ref, proj_ref, gw_ref), refs = refs[:11], refs[11:]
    if with_kv:
        (kvg_ref, kvw_ref, kvb_ref, pos_ref, inv_ref, o_ref, k_ref, v_ref), refs = refs[:8], refs[8:]
    else:
        (fg_ref, o_ref), refs = refs[:2], refs[2:]
    (hb_ref, e1_ref, e2_ref, gmin_ref, top_ref,
     a0_ref, a1_ref, p0_ref, p1_ref, acca_ref, accb_ref) = refs
    g = pl.program_id(1)
    last = pl.num_programs(1) - 1
    tt = x_ref.shape[0]
    ec = un_ref.shape[0] // 2
    rows_per_chunk = ec // N_KEYS
    mxu_cols = 2 * LANES
    n_piece = tt // mxu_cols
    half = N_KEYS
    ntop = PEER_TOPK + 1
    neg_inf = -jnp.inf

    @pl.when(g == 0)
    def _route():
        hb = _rms(x_ref[...], g_ref[...], NORM_EPS).T.astype(BF16)
        hb_ref[...] = hb
        for h in range(PEER_HEADS):
            q = _dot(wq_ref[h * 2 * half:(h + 1) * 2 * half, :], hb).astype(BF16)
            a0_ref[h * N_KEYS:(h + 1) * N_KEYS, :] = _dot(keys_ref[0], q[0:half, :])
            a1_ref[h * N_KEYS:(h + 1) * N_KEYS, :] = _dot(keys_ref[1], q[half:2 * half, :])

        def top_values(s_ref, h, cs):
            v = [s_ref[h * N_KEYS + r * SUBLANES:h * N_KEYS + (r + 1) * SUBLANES, cs]
                 for r in range(N_KEYS // SUBLANES)]
            for i, j in _SORT_PAIRS:
                v[i], v[j] = jnp.maximum(v[i], v[j]), jnp.minimum(v[i], v[j])
            depth = len(v)
            out = []
            for k in range(ntop):
                m = v[0]
                for shift in (4, 2, 1):
                    m = jnp.maximum(m, pltpu.roll(m, shift, 0))
                out.append(m)
                hit = v[0] == m
                for r in range(min(depth, ntop - 1 - k)):
                    v[r] = jnp.where(hit, v[r + 1] if r + 1 < depth else neg_inf, v[r])
            return out

        for tc in range(tt // LANES):
            cs = slice(tc * LANES, (tc + 1) * LANES)
            for c, s_ref in enumerate((a0_ref, a1_ref)):
                for h in range(PEER_HEADS):
                    for r, m in enumerate(top_values(s_ref, h, cs)):
                        top_ref[c, r, h:h + 1, cs] = m[0:1, :]

            a = [top_ref[0, r, :, cs] for r in range(ntop)]
            b = [top_ref[1, r, :, cs] for r in range(ntop)]
            cand = [a[r1] + b[r2] for r1, r2 in _CAND_PAIRS]
            a0, b0 = a[0], b[0]
            m0 = a0 + b0
            z = jnp.zeros(m0.shape, F32)
            for k in range(ntop):
                m = functools.reduce(jnp.maximum, cand)
                if k < PEER_TOPK:
                    z = z + jnp.exp(m - m0)
                if k == PEER_TOPK - 1:
                    v16 = m
                if k < ntop - 1:
                    cand = [jnp.where(cv == m, neg_inf, cv) for cv in cand]
            v17 = m
            zinv = np.float32(math.sqrt(0.5)) / z
            gmin = jnp.exp(0.5 * (v16 + v17) - m0) * zinv
            for h in range(PEER_HEADS):
                hk = slice(h * N_KEYS, (h + 1) * N_KEYS)
                e1_ref[h, tc] = jnp.exp(a0_ref[hk, cs] - a0[h:h + 1, :])
                e2_ref[h, tc] = jnp.exp(a1_ref[hk, cs] - b0[h:h + 1, :]) * zinv[h:h + 1, :]
                gmin_ref[h, tc] = gmin[h:h + 1, :]

            if (tc + 1) % (mxu_cols // LANES) == 0:
                ps = slice((tc + 1) * LANES - mxu_cols, (tc + 1) * LANES)
                a0_ref[:, ps] = _dot(u0_ref[0:ec, :], hb_ref[:, ps])
                a1_ref[:, ps] = _dot(u0_ref[ec:2 * ec, :], hb_ref[:, ps])

        acca_ref[...] = jnp.zeros(acca_ref.shape, F32)
        accb_ref[...] = jnp.zeros(accb_ref.shape, F32)

    def gate_gelu(a_ref, p_ref, chunk, tc):
        i0 = pl.multiple_of(chunk * rows_per_chunk, SUBLANES)
        cs = slice(tc * LANES, (tc + 1) * LANES)
        e18 = [e1_ref[h, tc, pl.ds(i0, rows_per_chunk), :] for h in range(PEER_HEADS)]
        gm = [gmin_ref[h, tc] for h in range(PEER_HEADS)]
        for ii in range(rows_per_chunk):
            rows = slice(ii * N_KEYS, (ii + 1) * N_KEYS)
            w = jnp.zeros((N_KEYS, LANES), F32)
            for h in range(PEER_HEADS):
                gate = e18[h][ii:ii + 1, :] * e2_ref[h, tc]
                w = w + jnp.where(gate >= gm[h], gate, 0.0)
            av = a_ref[rows, cs]
            p_ref[rows, cs] = ((w * av) * (1.0 + lax.erf(av))).astype(BF16)

    for par, (a_ref, p_ref, acc_ref) in enumerate(((a0_ref, p0_ref, acca_ref),
                                                   (a1_ref, p1_ref, accb_ref))):
        es = slice(par * ec, (par + 1) * ec)
        for piece in range(n_piece):
            cs = slice(piece * mxu_cols, (piece + 1) * mxu_cols)
            for tc in range(piece * (mxu_cols // LANES), (piece + 1) * (mxu_cols // LANES)):
                gate_gelu(a_ref, p_ref, 2 * g + par, tc)
            acc_ref[:, cs] += _dot(vt_ref[:, es], p_ref[:, cs])
            a_ref[:, cs] = _dot(un_ref[es, :], hb_ref[:, cs])

    @pl.when(g == last)
    def _finish():
        x1 = x_ref[...] + (acca_ref[...] + accb_ref[...]).T
        x2 = _ple_core(x1, ple_ref, pg_ref, proj_ref, gw_ref)
        if with_kv:
            o_ref[...] = x2
            kv = _dot(_rms(x2, kvg_ref[...], NORM_EPS).astype(BF16), kvw_ref[...]) + kvb_ref[...]
            kvd = k_ref.shape[1]
            cos, sin_lo, sin_hi = _rope_tables(pos_ref, inv_ref)
            k_ref[...] = _rope_apply(kv[:, :kvd], cos, sin_lo, sin_hi).astype(k_ref.dtype)
            v_ref[...] = kv[:, kvd:].astype(v_ref.dtype)
        else:
            o_ref[...] = _rms(x2, fg_ref[...], NORM_EPS)


def _peer(x, g, wq_t, keys, u, vt, layer, ple, kv, final_g, tile, ec):
    T, D = x.shape
    tok = lambda w: pl.BlockSpec((tile, w), lambda i, s: (i, 0))
    ple_specs = [tok(ple[0].shape[1])] + [_const_spec(a.shape) for a in ple[1:]]
    if kv is not None:
        kvd = kv[1].shape[1] // 2
        tail = list(kv)
        tail_specs = [_const_spec(kv[0].shape), _const_spec(kv[1].shape), _const_spec(kv[2].shape),
                      tok(1), _const_spec(kv[4].shape)]
        out_specs = [tok(D), tok(kvd), tok(kvd)]
        out_shape = [jax.ShapeDtypeStruct((T, D), F32), jax.ShapeDtypeStruct((T, kvd), BF16),
                     jax.ShapeDtypeStruct((T, kvd), BF16)]
    else:
        tail, tail_specs = [final_g], [_const_spec(final_g.shape)]
        out_specs = tok(D)
        out_shape = jax.ShapeDtypeStruct((T, D), F32)
    E = u.shape[1]
    nchunk = E // ec
    assert E == N_KEYS * N_KEYS and ec == SUBLANES * N_KEYS and tile % LANES == 0
    assert ec == PEER_HEADS * N_KEYS
    assert nchunk % 2 == 0
    ntc = tile // LANES
    nstep = nchunk // 2
    return pl.pallas_call(
        functools.partial(_peer_kernel, with_kv=kv is not None),
        grid=(T // tile, nstep),
        in_specs=[tok(D),
                  _const_spec((1, D)),
                  pl.BlockSpec((None,) + wq_t.shape[1:], lambda i, s: (layer, 0, 0),
                               pipeline_mode=pl.Buffered(1)),
                  pl.BlockSpec((None,) + keys.shape[1:], lambda i, s: (layer, 0, 0, 0),
                               pipeline_mode=pl.Buffered(1)),
                  pl.BlockSpec((None, 2 * ec, D), lambda i, s: (layer, 0, 0),
                               pipeline_mode=pl.Buffered(1)),
                  pl.BlockSpec((None, 2 * ec, D),
                               lambda i, s: (layer, jnp.minimum(s + 1, nstep - 1), 0)),
                  pl.BlockSpec((None, D, 2 * ec), lambda i, s: (layer, 0, s))]
                 + ple_specs + tail_specs,
        out_specs=out_specs,
        out_shape=out_shape,
        scratch_shapes=[pltpu.VMEM((D, tile), BF16),
                        pltpu.VMEM((PEER_HEADS, ntc, N_KEYS, LANES), F32),
                        pltpu.VMEM((PEER_HEADS, ntc, N_KEYS, LANES), F32),
                        pltpu.VMEM((PEER_HEADS, ntc, 1, LANES), F32),
                        pltpu.VMEM((2, PEER_TOPK + 1, PEER_HEADS, tile), F32),
                        pltpu.VMEM((ec, tile), F32), pltpu.VMEM((ec, tile), F32),
                        pltpu.VMEM((ec, tile), BF16), pltpu.VMEM((ec, tile), BF16),
                        pltpu.VMEM((D, tile), F32), pltpu.VMEM((D, tile), F32)],
        compiler_params=_params("arbitrary", "arbitrary"),
        name="peer",
    )(x, g, wq_t, keys, u, u, vt, *ple, *tail)


def _rope_tables(pos_ref, inv_ref):
    ang = pos_ref[...].astype(F32) * inv_ref[...]
    lane = lax.broadcasted_iota(jnp.int32, ang.shape, 1) % HEAD_DIM
    cos = jnp.cos(ang)
    sin = jnp.sin(ang)
    half = ROT_DIM // 2
    sin_lo = jnp.where(lane < half, -sin, 0.0)
    sin_hi = jnp.where((lane >= half) & (lane < ROT_DIM), sin, 0.0)
    return cos, sin_lo, sin_hi


def _rope_apply(t, cos, sin_lo, sin_hi):
    half = ROT_DIM // 2
    up = pltpu.roll(t, LANES - half, 1)
    dn = pltpu.roll(t, half, 1)
    return t * cos + up * sin_lo + dn * sin_hi


def _ple_core(x, p_ref, g_ref, proj_ref, gw_ref):
    hn = _rms(x, g_ref[...], NORM_EPS).astype(BF16)
    gate = jax.nn.sigmoid(_dot(hn, gw_ref[...]))
    return x + _dot(p_ref[...].astype(BF16), proj_ref[...]) * gate


def _attn_kernel(x_ref, g_ref, qw_ref, qb_ref, sink_ref, owt_ref, ob_ref, kc_ref, kp_ref, vc_ref,
                 vp_ref, pos_ref, invc_ref, o_ref, qt_ref, att_ref, *, n_q_heads):
    tile = x_ref.shape[0]
    W = WINDOW
    nblk = tile // W
    q_per_kv = n_q_heads // N_KV_HEADS
    half = ROT_DIM // 2
    scale = HEAD_DIM ** -0.5
    first = pl.program_id(1) == 0

    h = _rms(x_ref[...], g_ref[...], NORM_EPS).astype(BF16)
    q = (_dot(h, qw_ref[...]) + qb_ref[...]) * scale
    qt = q.T
    ang = invc_ref[...] * pos_ref[...].astype(F32)
    cos, sin = jnp.cos(ang), jnp.sin(ang)
    for hh in range(n_q_heads):
        base = hh * HEAD_DIM
        t1 = qt[base:base + half, :]
        t2 = qt[base + half:base + ROT_DIM, :]
        qt_ref[base:base + half, :] = (t1 * cos - t2 * sin).astype(BF16)
        qt_ref[base + half:base + ROT_DIM, :] = (t2 * cos + t1 * sin).astype(BF16)
        qt_ref[base + ROT_DIM:base + HEAD_DIM, :] = qt[base + ROT_DIM:base + HEAD_DIM, :].astype(BF16)

    kj = lax.broadcasted_iota(jnp.int32, (2 * W, W), 0)
    qi = lax.broadcasted_iota(jnp.int32, (2 * W, W), 1)
    first_lo = jnp.where(first, W, 0)
    band = jnp.where((kj > qi) & (kj <= qi + W), 0.0, -jnp.inf)
    band_first = jnp.where(kj >= first_lo, band, -jnp.inf)
    band = jnp.concatenate([band] * q_per_kv, axis=1)
    band_first = jnp.concatenate([band_first] * q_per_kv, axis=1)

    for n in range(nblk):
        cols = slice(n * W, (n + 1) * W)
        if n == 0:
            kprev, vprev = kp_ref[...], vp_ref[...]
        else:
            kprev, vprev = kc_ref[(n - 1) * W:n * W, :], vc_ref[(n - 1) * W:n * W, :]
        kblk = jnp.concatenate([kprev, kc_ref[cols, :]], axis=0)
        vblk = jnp.concatenate([vprev, vc_ref[cols, :]], axis=0)
        bias = band_first if n == 0 else band
        for g in range(N_KV_HEADS):
            ks = kblk[:, g * HEAD_DIM:(g + 1) * HEAD_DIM]
            vs = vblk[:, g * HEAD_DIM:(g + 1) * HEAD_DIM]
            heads = range(g * q_per_kv, (g + 1) * q_per_kv)
            qs = jnp.concatenate([qt_ref[hh * HEAD_DIM:(hh + 1) * HEAD_DIM, cols] for hh in heads],
                                 axis=1)
            sink = jnp.concatenate([jnp.broadcast_to(sink_ref[:, hh:hh + 1], (1, W)) for hh in heads],
                                   axis=1)
            s = _dot(ks, qs) + bias
            m = jnp.maximum(jnp.max(s, axis=0, keepdims=True), sink)
            e = jnp.exp(s - m)
            denom = jnp.sum(e, axis=0, keepdims=True) + jnp.exp(sink - m)
            o = _dot(vs, e.astype(BF16), _TN) / denom
            for r, hh in enumerate(heads):
                att_ref[hh * HEAD_DIM:(hh + 1) * HEAD_DIM, cols] = o[:, r * W:(r + 1) * W].astype(BF16)

    o_ref[...] = x_ref[...] + _dot(owt_ref[...], att_ref[...]).T + ob_ref[...]


def _attn(x, g, qw, qb, sinks, ow_t, ob, k, v, pos, inv_col, batch, tile):
    T, D = x.shape
    kvd = k.shape[1]
    nq = qw.shape[1] // HEAD_DIM
    nt = T // batch // tile
    bpt = tile // WINDOW
    row = lambda b, i: (b * nt + i, 0)
    prev = lambda b, i: (jnp.maximum((b * nt + i) * bpt - 1, 0), 0)
    kern = functools.partial(_attn_kernel, n_q_heads=nq)
    return pl.pallas_call(
        kern,
        grid=(batch, nt),
        in_specs=[pl.BlockSpec((tile, D), row), _const_spec((1, D)), _const_spec(qw.shape),
                  _const_spec(qb.shape), _const_spec(sinks.shape), _const_spec(ow_t.shape),
                  _const_spec(ob.shape),
                  pl.BlockSpec((tile, kvd), row), pl.BlockSpec((WINDOW, kvd), prev),
                  pl.BlockSpec((tile, kvd), row), pl.BlockSpec((WINDOW, kvd), prev),
                  pl.BlockSpec((None, 1, tile), lambda b, i: (b * nt + i, 0, 0)),
                  _const_spec(inv_col.shape)],
        out_specs=pl.BlockSpec((tile, D), row),
        out_shape=jax.ShapeDtypeStruct((T, D), F32),
        scratch_shapes=[pltpu.VMEM((qw.shape[1], tile), BF16),
                        pltpu.VMEM((qw.shape[1], tile), BF16)],
        compiler_params=_params("arbitrary", "arbitrary"),
        name="swa_attn",
    )(x, g, qw, qb, sinks, ow_t, ob, k, k, v, v, pos.reshape(T // tile, 1, tile), inv_col)


def _row(v):
    return v.reshape(1, -1)


def kernel(x, p, positions, ssm_norm, ssm_in_w, ssm_conv_w, ssm_conv_b, ssm_dt_bias, ssm_A_log, ssm_D, ssm_gate_norm, ssm_out_w, kv_norm, kv_w, kv_b, attn_norm, q_w, q_b, sinks, o_w, o_b, peer_norm, peer_q_w, peer_sub_keys, peer_u, peer_v, ple_norm, ple_proj, ple_gate_w, final_norm):
    B, S, D = x.shape
    T = B * S
    depth = p.shape[0]
    n_a = ssm_norm.shape[0]
    H = ssm_D.shape[1]
    d_inner = H * SSM_HEADDIM
    conv_dim = ssm_conv_w.shape[2]

    xt = x.reshape(T, D)
    pos = positions.reshape(T, 1)
    lane = np.arange(LANES) % HEAD_DIM
    inv = np.where(lane < ROT_DIM,
                   ROPE_THETA ** (-(2.0 * (lane % (ROT_DIM // 2))) / ROT_DIM), 0.0)
    inv = jnp.asarray(inv.reshape(1, LANES), F32)
    inv_col = inv[0, :ROT_DIM // 2].reshape(-1, 1)
    expand = jnp.asarray(np.repeat(np.eye(H, dtype=np.float32), SSM_HEADDIM, axis=1))
    tril = jnp.asarray(np.tril(np.ones((SSD_CHUNK, SSD_CHUNK), np.float32)))

    peer_wq_t = peer_q_w.astype(BF16).transpose(0, 2, 1)
    peer_keys = peer_sub_keys.astype(BF16)
    peer_u_b = (peer_u * np.float32(math.sqrt(0.5))).astype(BF16)
    peer_vt = peer_v.astype(BF16).transpose(0, 2, 1)

    k_sh = v_sh = None
    for i in range(depth):
        if i < n_a:
            z, xs, bc, dtr = _inproj(xt, _row(ssm_norm[i]), ssm_in_w[i].astype(BF16), ssm_conv_w[i],
                                     _row(ssm_conv_b[i]), d_inner=d_inner, tile=256, seq=S)
            y = _ssd(z, xs, bc, dtr, _row(ssm_dt_bias[i]), _row(ssm_A_log[i]),
                     _row(jnp.repeat(ssm_D[i], SSM_HEADDIM)), _row(ssm_gate_norm[i]), expand, tril,
                     batch=B)
            xt = _outproj(xt, y, ssm_out_w[i].astype(BF16), tile=512)
        else:
            j = i - n_a
            xt = _attn(xt, _row(attn_norm[j]), q_w[j].astype(BF16), _row(q_b[j]), _row(sinks[j]),
                       o_w[j].T.astype(BF16), _row(o_b[j]), k_sh, v_sh, pos, inv_col, batch=B,
                       tile=512)
        ple = (p[i].reshape(T, -1), _row(ple_norm[i]), ple_proj[i].astype(BF16),
               ple_gate_w[i].astype(BF16))
        peer_w = (_row(peer_norm[i]), peer_wq_t, peer_keys, peer_u_b, peer_vt)
        if i == n_a - 1:
            kv = (_row(kv_norm), kv_w.astype(BF16), _row(kv_b), pos, inv)
            xt, k_sh, v_sh = _peer(xt, *peer_w, layer=i, ple=ple, kv=kv, final_g=None,
                                   tile=512, ec=1024)
        elif i == depth - 1:
            xt = _peer(xt, *peer_w, layer=i, ple=ple, kv=None, final_g=_row(final_norm),
                       tile=512, ec=1024)
        else:
            raise NotImplementedError("PLE without K/V or final norm")
    return xt.reshape(B, S, D)
```

```python
import functools
import math

import jax
import jax.numpy as jnp
import numpy as np
from jax import lax
from jax.experimental import pallas as pl
from jax.experimental.pallas import tpu as pltpu

F32 = jnp.float32
BF16 = jnp.bfloat16

NORM_EPS = 1e-6
GATED_NORM_EPS = 1e-5
SSM_HEADDIM = 64
SSM_GROUPS = 8
SSM_STATE = 128
CONV_K = 4
SSD_CHUNK = 128
SEQ_PER_STEP = 2
CONV_COLS = 512
HEAD_DIM = 64
N_KV_HEADS = 2
WINDOW = 128
ROT_DIM = HEAD_DIM // 4
ROPE_THETA = 500000.0
PEER_HEADS = 8
N_KEYS = 128
PEER_TOPK = 16

LANES = 128
SUBLANES = 8
VMEM_LIMIT = 56 * 1024 * 1024

_CAND_PAIRS = [(r1, r2) for r1 in range(PEER_TOPK + 1) for r2 in range(PEER_TOPK + 1)
               if (r1 + 1) * (r2 + 1) <= PEER_TOPK + 1]


def _sort_network(n):
    pairs = []
    p = 1
    while p < n:
        k = p
        while k >= 1:
            for j in range(k % p, n - k, 2 * k):
                for i in range(min(k, n - j - k)):
                    if (i + j) // (2 * p) == (i + j + k) // (2 * p):
                        pairs.append((i + j, i + j + k))
            k //= 2
        p *= 2
    return pairs


_SORT_PAIRS = _sort_network(N_KEYS // SUBLANES)


def _params(*sem):
    return pltpu.CompilerParams(dimension_semantics=sem, vmem_limit_bytes=VMEM_LIMIT)


def _const_spec(shape):
    nd = len(shape)
    return pl.BlockSpec(shape, lambda *_: (0,) * nd, pipeline_mode=pl.Buffered(1))


def _rms(x, g, eps):
    return x * lax.rsqrt(jnp.mean(x * x, axis=-1, keepdims=True) + eps) * g


def _dot(a, b, dims=None, precision=None):
    if dims is None:
        dims = (((a.ndim - 1,), (0,)), ((), ()))
    return lax.dot_general(a, b, dims, precision=precision, preferred_element_type=F32)


_NT = (((1,), (1,)), ((), ()))
_TN = (((0,), (0,)), ((), ()))
_HI = lax.Precision.HIGHEST


def _inproj_kernel(x_ref, g_ref, w_ref, cw_ref, cb_ref, z_ref, xs_ref, bc_ref,
                   dt_ref, xb_ref, tail_ref, *, tiles_per_seq):
    tile = x_ref.shape[0]
    d_inner = xs_ref.shape[1]
    conv_dim = tail_ref.shape[1]
    tail = SUBLANES

    @pl.when(pl.program_id(0) % tiles_per_seq == 0)
    def _():
        tail_ref[...] = jnp.zeros(tail_ref.shape, F32)

    h = _rms(x_ref[...], g_ref[...], NORM_EPS).astype(BF16)
    dt_ref[...] = _dot(h, w_ref[:, d_inner + conv_dim:])

    for gi, lo in enumerate(range(0, conv_dim, CONV_COLS)):
        cs = slice(lo, lo + CONV_COLS)
        stage = xb_ref.at[gi % 2]
        stage[0:tail, :] = tail_ref[:, cs]
        stage[tail:tail + tile, :] = _dot(h, w_ref[:, d_inner + lo:d_inner + lo + CONV_COLS])
        if lo < d_inner:
            z_ref[:, cs] = _dot(h, w_ref[:, cs])
        u = stage[...]
        tail_ref[:, cs] = u[tile:tile + tail, :]
        acc = cb_ref[:, cs] + u[tail:, :] * cw_ref[CONV_K - 1:CONV_K, cs]
        for back in range(1, CONV_K):
            acc = acc + pltpu.roll(u, back, 0)[tail:, :] * cw_ref[CONV_K - 1 - back:CONV_K - back, cs]
        act = acc * jax.nn.sigmoid(acc)
        if lo < d_inner:
            xs_ref[:, cs] = act
        else:
            bc_ref[:, lo - d_inner:lo - d_inner + CONV_COLS] = act.astype(BF16)


def _inproj(x, g, w, cw, cb, d_inner, tile, seq):
    T, D = x.shape
    nz, nx = d_inner, cw.shape[1]
    nd = w.shape[1] - nz - nx
    assert nz % CONV_COLS == 0 and nx % CONV_COLS == 0 and seq % tile == 0
    tok = lambda w: pl.BlockSpec((tile, w), lambda i: (i, 0))
    kern = functools.partial(_inproj_kernel, tiles_per_seq=seq // tile)
    return pl.pallas_call(
        kern,
        grid=(T // tile,),
        in_specs=[tok(D), _const_spec((1, D)), _const_spec(w.shape), _const_spec(cw.shape),
                  _const_spec(cb.shape)],
        out_specs=[tok(nz), tok(nz), tok(nx - nz), tok(nd)],
        out_shape=[jax.ShapeDtypeStruct((T, nz), F32), jax.ShapeDtypeStruct((T, nz), F32),
                   jax.ShapeDtypeStruct((T, nx - nz), BF16), jax.ShapeDtypeStruct((T, nd), F32)],
        scratch_shapes=[pltpu.VMEM((2, tile + SUBLANES, CONV_COLS), F32),
                        pltpu.VMEM((SUBLANES, nx), F32)],
        compiler_params=_params("arbitrary"),
        name="ssm_inproj",
    )(x, g, w, cw, cb)


def _ssd_kernel(z_ref, xs_ref, bc_ref, dtr_ref, dtb_ref, alog_ref, dexp_ref, gn_ref,
                expand_ref, tril_ref, y_ref, st_ref, *, d_inner, n_heads):
    L = SSD_CHUNK
    gw = d_inner // SSM_GROUPS
    hpg = n_heads // SSM_GROUPS
    gn_state = SSM_GROUPS * SSM_STATE
    seqs = range(z_ref.shape[0])

    @pl.when(pl.program_id(1) == 0)
    def _():
        st_ref[...] = jnp.zeros(st_ref.shape, F32)

    tril = tril_ref[...]
    causal = tril > 0.5

    def per_channel(v):
        hi = v.astype(BF16)
        r1 = v - hi.astype(F32)
        mid = r1.astype(BF16)
        lo = (r1 - mid.astype(F32)).astype(BF16)
        return _dot(jnp.concatenate([hi, mid, lo], axis=1), expand_ref[...])

    a_cs, a_cs_t, a_cs_x, dt_x = [], [], [], []
    for q in seqs:
        dt_in = dtr_ref[q] + dtb_ref[...]
        dt = jnp.maximum(dt_in, 0.0) + jnp.log1p(jnp.exp(-jnp.abs(dt_in)))
        a = dt * (-jnp.exp(alog_ref[...]))
        acs = _dot(tril, a, precision=_HI)
        a_cs.append(acs)
        a_cs_t.append(acs.T)
        a_cs_x.append(per_channel(acs))
        dt_x.append(per_channel(dt))

    for g in range(SSM_GROUPS):
        lo = g * gw
        for q in seqs:
            xs = xs_ref[q, :, lo:lo + gw]
            bm = bc_ref[q, :, g * SSM_STATE:(g + 1) * SSM_STATE]
            cm = bc_ref[q, :, gn_state + g * SSM_STATE:gn_state + (g + 1) * SSM_STATE]
            xdt = xs * dt_x[q][:, lo:lo + gw]
            cb = _dot(cm, bm, _NT)
            yd = []
            for r in range(hpg):
                hh = g * hpg + r
                seg = a_cs[q][:, hh:hh + 1] - a_cs_t[q][hh:hh + 1, :]
                lmat = jnp.exp(jnp.where(causal, seg, -jnp.inf))
                m = (cb * lmat).astype(BF16)
                yd.append(_dot(m, xdt[:, r * SSM_HEADDIM:(r + 1) * SSM_HEADDIM].astype(BF16)))
            y = jnp.concatenate(yd, axis=1)
            acx = a_cs_x[q][:, lo:lo + gw]
            alx = a_cs_x[q][L - 1:L, lo:lo + gw]
            prev = st_ref[q, g]
            y = y + _dot(cm, prev.astype(BF16)) * jnp.exp(acx)
            xd = (xdt * jnp.exp(alx - acx)).astype(BF16)
            st_ref[q, g] = prev * jnp.exp(alx) + _dot(bm, xd, _TN)
            y = y + xs * dexp_ref[:, lo:lo + gw]
            zg = z_ref[q, :, lo:lo + gw]
            y = y * (zg * jax.nn.sigmoid(zg))
            y = y * lax.rsqrt(jnp.mean(y * y, axis=-1, keepdims=True) + GATED_NORM_EPS)
            y_ref[q, :, lo:lo + gw] = (y * gn_ref[:, lo:lo + gw]).astype(y_ref.dtype)


def _ssd(z, xs, bc, dtr, dtb, alog, dexp, gn, expand, tril, batch):
    T, d_inner = z.shape
    H = dtr.shape[1]
    L = SSD_CHUNK
    seq = T // batch
    assert batch % SEQ_PER_STEP == 0 and seq % L == 0
    blk = lambda w: pl.BlockSpec((SEQ_PER_STEP, L, w), lambda b, c: (b, c, 0))
    per_seq = lambda v: v.reshape(batch, seq, v.shape[1])
    kern = functools.partial(_ssd_kernel, d_inner=d_inner, n_heads=H)
    y = pl.pallas_call(
        kern,
        grid=(batch // SEQ_PER_STEP, seq // L),
        in_specs=[blk(d_inner), blk(d_inner), blk(bc.shape[1]), blk(H),
                  _const_spec(dtb.shape), _const_spec(alog.shape), _const_spec(dexp.shape),
                  _const_spec(gn.shape), _const_spec(expand.shape), _const_spec(tril.shape)],
        out_specs=blk(d_inner),
        out_shape=jax.ShapeDtypeStruct((batch, seq, d_inner), BF16),
        scratch_shapes=[pltpu.VMEM((SEQ_PER_STEP, SSM_GROUPS, SSM_STATE, d_inner // SSM_GROUPS), F32)],
        compiler_params=_params("arbitrary", "arbitrary"),
        name="ssd_scan",
    )(per_seq(z), per_seq(xs), per_seq(bc), per_seq(dtr), dtb, alog, dexp, gn, expand, tril)
    return y.reshape(T, d_inner)


def _outproj_kernel(x_ref, y_ref, w_ref, o_ref):
    o_ref[...] = x_ref[...] + _dot(y_ref[...], w_ref[...])


def _outproj(x, y, w, tile):
    T, D = x.shape
    K = y.shape[1]
    return pl.pallas_call(
        _outproj_kernel,
        grid=(T // tile,),
        in_specs=[pl.BlockSpec((tile, D), lambda i: (i, 0)),
                  pl.BlockSpec((tile, K), lambda i: (i, 0)), _const_spec(w.shape)],
        out_specs=pl.BlockSpec((tile, D), lambda i: (i, 0)),
        out_shape=jax.ShapeDtypeStruct((T, D), F32),
        compiler_params=_params("arbitrary"),
        name="ssm_outproj",
    )(x, y, w)


def _peer_kernel(x_ref, g_ref, wq_ref, keys_ref, u0_ref, un_ref, vt_ref, o_ref,
                 hb_ref, e1_ref, e2_ref, gmin_ref, top_ref,
                 a0_ref, a1_ref, p0_ref, p1_ref, acca_ref, accb_ref):
    g = pl.program_id(1)
    last = pl.num_programs(1) - 1
    tt = x_ref.shape[0]
    ec = un_ref.shape[0] // 2
    rows_per_chunk = ec // N_KEYS
    mxu_cols = 2 * LANES
    n_piece = tt // mxu_cols
    half = N_KEYS
    ntop = PEER_TOPK + 1
    neg_inf = -jnp.inf

    @pl.when(g == 0)
    def _route():
        hb = _rms(x_ref[...], g_ref[...], NORM_EPS).T.astype(BF16)
        hb_ref[...] = hb
        for h in range(PEER_HEADS):
            q = _dot(wq_ref[h * 2 * half:(h + 1) * 2 * half, :], hb).astype(BF16)
            a0_ref[h * N_KEYS:(h + 1) * N_KEYS, :] = _dot(keys_ref[0], q[0:half, :])
            a1_ref[h * N_KEYS:(h + 1) * N_KEYS, :] = _dot(keys_ref[1], q[half:2 * half, :])

        def top_values(s_ref, h, cs):
            v = [s_ref[h * N_KEYS + r * SUBLANES:h * N_KEYS + (r + 1) * SUBLANES, cs]
                 for r in range(N_KEYS // SUBLANES)]
            for i, j in _SORT_PAIRS:
                v[i], v[j] = jnp.maximum(v[i], v[j]), jnp.minimum(v[i], v[j])
            depth = len(v)
            out = []
            for k in range(ntop):
                m = v[0]
                for shift in (4, 2, 1):
                    m = jnp.maximum(m, pltpu.roll(m, shift, 0))
                out.append(m)
                hit = v[0] == m
                for r in range(min(depth, ntop - 1 - k)):
                    v[r] = jnp.where(hit, v[r + 1] if r + 1 < depth else neg_inf, v[r])
            return out

        for tc in range(tt // LANES):
            cs = slice(tc * LANES, (tc + 1) * LANES)
            for c, s_ref in enumerate((a0_ref, a1_ref)):
                for h in range(PEER_HEADS):
                    for r, m in enumerate(top_values(s_ref, h, cs)):
                        top_ref[c, r, h:h + 1, cs] = m[0:1, :]

            a = [top_ref[0, r, :, cs] for r in range(ntop)]
            b = [top_ref[1, r, :, cs] for r in range(ntop)]
            cand = [a[r1] + b[r2] for r1, r2 in _CAND_PAIRS]
            a0, b0 = a[0], b[0]
            m0 = a0 + b0
            z = jnp.zeros(m0.shape, F32)
            for k in range(ntop):
                m = functools.reduce(jnp.maximum, cand)
                if k < PEER_TOPK:
                    z = z + jnp.exp(m - m0)
                if k == PEER_TOPK - 1:
                    v16 = m
                if k < ntop - 1:
                    cand = [jnp.where(cv == m, neg_inf, cv) for cv in cand]
            v17 = m
            zinv = np.float32(math.sqrt(0.5)) / z
            gmin = jnp.exp(0.5 * (v16 + v17) - m0) * zinv
            for h in range(PEER_HEADS):
                hk = slice(h * N_KEYS, (h + 1) * N_KEYS)
                e1_ref[h, tc] = jnp.exp(a0_ref[hk, cs] - a0[h:h + 1, :])
                e2_ref[h, tc] = jnp.exp(a1_ref[hk, cs] - b0[h:h + 1, :]) * zinv[h:h + 1, :]
                gmin_ref[h, tc] = gmin[h:h + 1, :]

            if (tc + 1) % (mxu_cols // LANES) == 0:
                ps = slice((tc + 1) * LANES - mxu_cols, (tc + 1) * LANES)
                a0_ref[:, ps] = _dot(u0_ref[0:ec, :], hb_ref[:, ps])
                a1_ref[:, ps] = _dot(u0_ref[ec:2 * ec, :], hb_ref[:, ps])

        acca_ref[...] = jnp.zeros(acca_ref.shape, F32)
        accb_ref[...] = jnp.zeros(accb_ref.shape, F32)

    def gate_gelu(a_ref, p_ref, chunk, tc):
        i0 = pl.multiple_of(chunk * rows_per_chunk, SUBLANES)
        cs = slice(tc * LANES, (tc + 1) * LANES)
        e18 = [e1_ref[h, tc, pl.ds(i0, rows_per_chunk), :] for h in range(PEER_HEADS)]
        gm = [gmin_ref[h, tc] for h in range(PEER_HEADS)]
        for ii in range(rows_per_chunk):
            rows = slice(ii * N_KEYS, (ii + 1) * N_KEYS)
            w = jnp.zeros((N_KEYS, LANES), F32)
            for h in range(PEER_HEADS):
                gate = e18[h][ii:ii + 1, :] * e2_ref[h, tc]
                w = w + jnp.where(gate >= gm[h], gate, 0.0)
            av = a_ref[rows, cs]
            p_ref[rows, cs] = ((w * av) * (1.0 + lax.erf(av))).astype(BF16)

    for par, (a_ref, p_ref, acc_ref) in enumerate(((a0_ref, p0_ref, acca_ref),
                                                   (a1_ref, p1_ref, accb_ref))):
        es = slice(par * ec, (par + 1) * ec)
        for piece in range(n_piece):
            cs = slice(piece * mxu_cols, (piece + 1) * mxu_cols)
            for tc in range(piece * (mxu_cols // LANES), (piece + 1) * (mxu_cols // LANES)):
                gate_gelu(a_ref, p_ref, 2 * g + par, tc)
            acc_ref[:, cs] += _dot(vt_ref[:, es], p_ref[:, cs])
            a_ref[:, cs] = _dot(un_ref[es, :], hb_ref[:, cs])

    @pl.when(g == last)
    def _finish():
        o_ref[...] = x_ref[...] + (acca_ref[...] + accb_ref[...]).T


def _peer(x, g, wq_t, keys, u, vt, layer, tile, ec):
    T, D = x.shape
    E = u.shape[1]
    nchunk = E // ec
    assert E == N_KEYS * N_KEYS and ec == SUBLANES * N_KEYS and tile % LANES == 0
    assert ec == PEER_HEADS * N_KEYS
    assert nchunk % 2 == 0
    ntc = tile // LANES
    nstep = nchunk // 2
    return pl.pallas_call(
        _peer_kernel,
        grid=(T // tile, nstep),
        in_specs=[pl.BlockSpec((tile, D), lambda i, s: (i, 0)),
                  _const_spec((1, D)),
                  pl.BlockSpec((None,) + wq_t.shape[1:], lambda i, s: (layer, 0, 0),
                               pipeline_mode=pl.Buffered(1)),
                  pl.BlockSpec((None,) + keys.shape[1:], lambda i, s: (layer, 0, 0, 0),
                               pipeline_mode=pl.Buffered(1)),
                  pl.BlockSpec((None, 2 * ec, D), lambda i, s: (layer, 0, 0),
                               pipeline_mode=pl.Buffered(1)),
                  pl.BlockSpec((None, 2 * ec, D),
                               lambda i, s: (layer, jnp.minimum(s + 1, nstep - 1), 0)),
                  pl.BlockSpec((None, D, 2 * ec), lambda i, s: (layer, 0, s))],
        out_specs=pl.BlockSpec((tile, D), lambda i, s: (i, 0)),
        out_shape=jax.ShapeDtypeStruct((T, D), F32),
        scratch_shapes=[pltpu.VMEM((D, tile), BF16),
                        pltpu.VMEM((PEER_HEADS, ntc, N_KEYS, LANES), F32),
                        pltpu.VMEM((PEER_HEADS, ntc, N_KEYS, LANES), F32),
                        pltpu.VMEM((PEER_HEADS, ntc, 1, LANES), F32),
                        pltpu.VMEM((2, PEER_TOPK + 1, PEER_HEADS, tile), F32),
                        pltpu.VMEM((ec, tile), F32), pltpu.VMEM((ec, tile), F32),
                        pltpu.VMEM((ec, tile), BF16), pltpu.VMEM((ec, tile), BF16),
                        pltpu.VMEM((D, tile), F32), pltpu.VMEM((D, tile), F32)],
        compiler_params=_params("arbitrary", "arbitrary"),
        name="peer",
    )(x, g, wq_t, keys, u, u, vt)


def _rope_tables(pos_ref, inv_ref):
    ang = pos_ref[...].astype(F32) * inv_ref[...]
    lane = lax.broadcasted_iota(jnp.int32, ang.shape, 1) % HEAD_DIM
    cos = jnp.cos(ang)
    sin = jnp.sin(ang)
    half = ROT_DIM // 2
    sin_lo = jnp.where(lane < half, -sin, 0.0)
    sin_hi = jnp.where((lane >= half) & (lane < ROT_DIM), sin, 0.0)
    return cos, sin_lo, sin_hi


def _rope_apply(t, cos, sin_lo, sin_hi):
    half = ROT_DIM // 2
    up = pltpu.roll(t, LANES - half, 1)
    dn = pltpu.roll(t, half, 1)
    return t * cos + up * sin_lo + dn * sin_hi


def _ple_core(x_ref, p_ref, g_ref, proj_ref, gw_ref):
    x = x_ref[...]
    hn = _rms(x, g_ref[...], NORM_EPS).astype(BF16)
    gate = jax.nn.sigmoid(_dot(hn, gw_ref[...]))
    return x + _dot(p_ref[...].astype(BF16), proj_ref[...]) * gate


def _ple_kv_kernel(x_ref, p_ref, g_ref, proj_ref, gw_ref, kvg_ref, kvw_ref, kvb_ref, pos_ref,
                   inv_ref, o_ref, k_ref, v_ref):
    x2 = _ple_core(x_ref, p_ref, g_ref, proj_ref, gw_ref)
    o_ref[...] = x2
    kv = _dot(_rms(x2, kvg_ref[...], NORM_EPS).astype(BF16), kvw_ref[...]) + kvb_ref[...]
    kvd = k_ref.shape[1]
    cos, sin_lo, sin_hi = _rope_tables(pos_ref, inv_ref)
    k_ref[...] = _rope_apply(kv[:, :kvd], cos, sin_lo, sin_hi).astype(k_ref.dtype)
    v_ref[...] = kv[:, kvd:].astype(v_ref.dtype)


def _ple_final_kernel(x_ref, p_ref, g_ref, proj_ref, gw_ref, fg_ref, o_ref):
    x2 = _ple_core(x_ref, p_ref, g_ref, proj_ref, gw_ref)
    o_ref[...] = _rms(x2, fg_ref[...], NORM_EPS)


def _ple_kv(x, p, g, proj, gw, kvg, kvw, kvb, pos, inv, tile):
    T, D = x.shape
    P = p.shape[1]
    kvd = kvw.shape[1] // 2
    tok = lambda w: pl.BlockSpec((tile, w), lambda i: (i, 0))
    return pl.pallas_call(
        _ple_kv_kernel,
        grid=(T // tile,),
        in_specs=[tok(D), tok(P), _const_spec((1, D)), _const_spec(proj.shape),
                  _const_spec(gw.shape), _const_spec((1, D)), _const_spec(kvw.shape),
                  _const_spec(kvb.shape), tok(1), _const_spec(inv.shape)],
        out_specs=[tok(D), tok(kvd), tok(kvd)],
        out_shape=[jax.ShapeDtypeStruct((T, D), F32), jax.ShapeDtypeStruct((T, kvd), BF16),
                   jax.ShapeDtypeStruct((T, kvd), BF16)],
        compiler_params=_params("arbitrary"),
        name="ple_kv",
    )(x, p, g, proj, gw, kvg, kvw, kvb, pos, inv)


def _ple_final(x, p, g, proj, gw, fg, tile):
    T, D = x.shape
    P = p.shape[1]
    tok = lambda w: pl.BlockSpec((tile, w), lambda i: (i, 0))
    return pl.pallas_call(
        _ple_final_kernel,
        grid=(T // tile,),
        in_specs=[tok(D), tok(P), _const_spec((1, D)), _const_spec(proj.shape),
                  _const_spec(gw.shape), _const_spec((1, D))],
        out_specs=tok(D),
        out_shape=jax.ShapeDtypeStruct((T, D), F32),
        compiler_params=_params("arbitrary"),
        name="ple_final",
    )(x, p, g, proj, gw, fg)


def _attn_kernel(x_ref, g_ref, qw_ref, qb_ref, sink_ref, owt_ref, ob_ref, kc_ref, kp_ref, vc_ref,
                 vp_ref, pos_ref, invc_ref, o_ref, qt_ref, att_ref, *, n_q_heads):
    tile = x_ref.shape[0]
    W = WINDOW
    nblk = tile // W
    q_per_kv = n_q_heads // N_KV_HEADS
    half = ROT_DIM // 2
    scale = HEAD_DIM ** -0.5
    first = pl.program_id(1) == 0

    h = _rms(x_ref[...], g_ref[...], NORM_EPS).astype(BF16)
    q = (_dot(h, qw_ref[...]) + qb_ref[...]) * scale
    qt = q.T
    ang = invc_ref[...] * pos_ref[...].astype(F32)
    cos, sin = jnp.cos(ang), jnp.sin(ang)
    for hh in range(n_q_heads):
        base = hh * HEAD_DIM
        t1 = qt[base:base + half, :]
        t2 = qt[base + half:base + ROT_DIM, :]
        qt_ref[base:base + half, :] = (t1 * cos - t2 * sin).astype(BF16)
        qt_ref[base + half:base + ROT_DIM, :] = (t2 * cos + t1 * sin).astype(BF16)
        qt_ref[base + ROT_DIM:base + HEAD_DIM, :] = qt[base + ROT_DIM:base + HEAD_DIM, :].astype(BF16)

    kj = lax.broadcasted_iota(jnp.int32, (2 * W, W), 0)
    qi = lax.broadcasted_iota(jnp.int32, (2 * W, W), 1)
    first_lo = jnp.where(first, W, 0)
    band = jnp.where((kj > qi) & (kj <= qi + W), 0.0, -jnp.inf)
    band_first = jnp.where(kj >= first_lo, band, -jnp.inf)
    band = jnp.concatenate([band] * q_per_kv, axis=1)
    band_first = jnp.concatenate([band_first] * q_per_kv, axis=1)

    for n in range(nblk):
        cols = slice(n * W, (n + 1) * W)
        if n == 0:
            kprev, vprev = kp_ref[...], vp_ref[...]
        else:
            kprev, vprev = kc_ref[(n - 1) * W:n * W, :], vc_ref[(n - 1) * W:n * W, :]
        kblk = jnp.concatenate([kprev, kc_ref[cols, :]], axis=0)
        vblk = jnp.concatenate([vprev, vc_ref[cols, :]], axis=0)
        bias = band_first if n == 0 else band
        for g in range(N_KV_HEADS):
            ks = kblk[:, g * HEAD_DIM:(g + 1) * HEAD_DIM]
            vs = vblk[:, g * HEAD_DIM:(g + 1) * HEAD_DIM]
            heads = range(g * q_per_kv, (g + 1) * q_per_kv)
            qs = jnp.concatenate([qt_ref[hh * HEAD_DIM:(hh + 1) * HEAD_DIM, cols] for hh in heads],
                                 axis=1)
            sink = jnp.concatenate([jnp.broadcast_to(sink_ref[:, hh:hh + 1], (1, W)) for hh in heads],
                                   axis=1)
            s = _dot(ks, qs) + bias
            m = jnp.maximum(jnp.max(s, axis=0, keepdims=True), sink)
            e = jnp.exp(s - m)
            denom = jnp.sum(e, axis=0, keepdims=True) + jnp.exp(sink - m)
            o = _dot(vs, e.astype(BF16), _TN) / denom
            for r, hh in enumerate(heads):
                att_ref[hh * HEAD_DIM:(hh + 1) * HEAD_DIM, cols] = o[:, r * W:(r + 1) * W].astype(BF16)

    o_ref[...] = x_ref[...] + _dot(owt_ref[...], att_ref[...]).T + ob_ref[...]


def _attn(x, g, qw, qb, sinks, ow_t, ob, k, v, pos, inv_col, batch, tile):
    T, D = x.shape
    kvd = k.shape[1]
    nq = qw.shape[1] // HEAD_DIM
    nt = T // batch // tile
    bpt = tile // WINDOW
    row = lambda b, i: (b * nt + i, 0)
    prev = lambda b, i: (jnp.maximum((b * nt + i) * bpt - 1, 0), 0)
    kern = functools.partial(_attn_kernel, n_q_heads=nq)
    return pl.pallas_call(
        kern,
        grid=(batch, nt),
        in_specs=[pl.BlockSpec((tile, D), row), _const_spec((1, D)), _const_spec(qw.shape),
                  _const_spec(qb.shape), _const_spec(sinks.shape), _const_spec(ow_t.shape),
                  _const_spec(ob.shape),
                  pl.BlockSpec((tile, kvd), row), pl.BlockSpec((WINDOW, kvd), prev),
                  pl.BlockSpec((tile, kvd), row), pl.BlockSpec((WINDOW, kvd), prev),
                  pl.BlockSpec((None, 1, tile), lambda b, i: (b * nt + i, 0, 0)),
                  _const_spec(inv_col.shape)],
        out_specs=pl.BlockSpec((tile, D), row),
        out_shape=jax.ShapeDtypeStruct((T, D), F32),
        scratch_shapes=[pltpu.VMEM((qw.shape[1], tile), BF16),
                        pltpu.VMEM((qw.shape[1], tile), BF16)],
        compiler_params=_params("arbitrary", "arbitrary"),
        name="swa_attn",
    )(x, g, qw, qb, sinks, ow_t, ob, k, k, v, v, pos.reshape(T // tile, 1, tile), inv_col)


def _row(v):
    return v.reshape(1, -1)


def kernel(x, p, positions, ssm_norm, ssm_in_w, ssm_conv_w, ssm_conv_b, ssm_dt_bias, ssm_A_log, ssm_D, ssm_gate_norm, ssm_out_w, kv_norm, kv_w, kv_b, attn_norm, q_w, q_b, sinks, o_w, o_b, peer_norm, peer_q_w, peer_sub_keys, peer_u, peer_v, ple_norm, ple_proj, ple_gate_w, final_norm):
    B, S, D = x.shape
    T = B * S
    depth = p.shape[0]
    n_a = ssm_norm.shape[0]
    H = ssm_D.shape[1]
    d_inner = H * SSM_HEADDIM
    conv_dim = ssm_conv_w.shape[2]

    xt = x.reshape(T, D)
    pos = positions.reshape(T, 1)
    lane = np.arange(LANES) % HEAD_DIM
    inv = np.where(lane < ROT_DIM,
                   ROPE_THETA ** (-(2.0 * (lane % (ROT_DIM // 2))) / ROT_DIM), 0.0)
    inv = jnp.asarray(inv.reshape(1, LANES), F32)
    inv_col = inv[0, :ROT_DIM // 2].reshape(-1, 1)
    expand = jnp.asarray(np.tile(np.repeat(np.eye(H, dtype=np.float32), SSM_HEADDIM, axis=1),
                                 (3, 1)), BF16)
    tril = jnp.asarray(np.tril(np.ones((SSD_CHUNK, SSD_CHUNK), np.float32)))

    peer_wq_t = peer_q_w.astype(BF16).transpose(0, 2, 1)
    peer_keys = peer_sub_keys.astype(BF16)
    peer_u_b = (peer_u * np.float32(math.sqrt(0.5))).astype(BF16)
    peer_vt = peer_v.astype(BF16).transpose(0, 2, 1)

    k_sh = v_sh = None
    for i in range(depth):
        if i < n_a:
            z, xs, bc, dtr = _inproj(xt, _row(ssm_norm[i]), ssm_in_w[i].astype(BF16), ssm_conv_w[i],
                                     _row(ssm_conv_b[i]), d_inner=d_inner, tile=256, seq=S)
            y = _ssd(z, xs, bc, dtr, _row(ssm_dt_bias[i]), _row(ssm_A_log[i]),
                     _row(jnp.repeat(ssm_D[i], SSM_HEADDIM)), _row(ssm_gate_norm[i]), expand, tril,
                     batch=B)
            xt = _outproj(xt, y, ssm_out_w[i].astype(BF16), tile=512)
        else:
            j = i - n_a
            xt = _attn(xt, _row(attn_norm[j]), q_w[j].astype(BF16), _row(q_b[j]), _row(sinks[j]),
                       o_w[j].T.astype(BF16), _row(o_b[j]), k_sh, v_sh, pos, inv_col, batch=B,
                       tile=512)
        xt = _peer(xt, _row(peer_norm[i]), peer_wq_t, peer_keys, peer_u_b, peer_vt, layer=i,
                   tile=512, ec=1024)
        if i == n_a - 1:
            xt, k_sh, v_sh = _ple_kv(xt, p[i].reshape(T, -1), _row(ple_norm[i]),
                                     ple_proj[i].astype(BF16), ple_gate_w[i].astype(BF16),
                                     _row(kv_norm), kv_w.astype(BF16), _row(kv_b), pos, inv,
                                     tile=512)
        elif i == depth - 1:
            xt = _ple_final(xt, p[i].reshape(T, -1), _row(ple_norm[i]), ple_proj[i].astype(BF16),
                            ple_gate_w[i].astype(BF16), _row(final_norm), tile=512)
        else:
            raise NotImplementedError("PLE without K/V or final norm")
    return xt.reshape(B, S, D)
```

```python
import functools
import math

import jax
import jax.numpy as jnp
import numpy as np
from jax import lax
from jax.experimental import pallas as pl
from jax.experimental.pallas import tpu as pltpu

F32 = jnp.float32
BF16 = jnp.bfloat16

NORM_EPS = 1e-6
GATED_NORM_EPS = 1e-5
SSM_HEADDIM = 64
SSM_GROUPS = 8
SSM_STATE = 128
CONV_K = 4
SSD_CHUNK = 128
SEQ_PER_STEP = 2
CONV_COLS = 256
HEAD_DIM = 64
N_KV_HEADS = 2
WINDOW = 128
ROT_DIM = HEAD_DIM // 4
ROPE_THETA = 500000.0
PEER_HEADS = 8
N_KEYS = 128
PEER_TOPK = 16

LANES = 128
SUBLANES = 8
VMEM_LIMIT = 56 * 1024 * 1024

INPROJ_TILE = 256
OUTPROJ_TILE = 512
ATTN_TILE = 512
PLE_TILE = 512
PEER_TILE = 512
PEER_CHUNK = SUBLANES * N_KEYS

_CAND_PAIRS = [(r1, r2) for r1 in range(PEER_TOPK + 1) for r2 in range(PEER_TOPK + 1)
               if (r1 + 1) * (r2 + 1) <= PEER_TOPK + 1]


def _sort_network(n):
    pairs = []
    p = 1
    while p < n:
        k = p
        while k >= 1:
            for j in range(k % p, n - k, 2 * k):
                for i in range(min(k, n - j - k)):
                    if (i + j) // (2 * p) == (i + j + k) // (2 * p):
                        pairs.append((i + j, i + j + k))
            k //= 2
        p *= 2
    return pairs


_SORT_PAIRS = _sort_network(N_KEYS // SUBLANES)


def _params(*sem):
    return pltpu.CompilerParams(dimension_semantics=sem, vmem_limit_bytes=VMEM_LIMIT)


def _const_spec(shape):
    nd = len(shape)
    return pl.BlockSpec(shape, lambda *_: (0,) * nd, pipeline_mode=pl.Buffered(1))


def _rms(x, g, eps):
    return x * lax.rsqrt(jnp.mean(x * x, axis=-1, keepdims=True) + eps) * g


def _dot(a, b, dims=None, precision=None):
    if dims is None:
        dims = (((a.ndim - 1,), (0,)), ((), ()))
    return lax.dot_general(a, b, dims, precision=precision, preferred_element_type=F32)


_NT = (((1,), (1,)), ((), ()))
_TN = (((0,), (0,)), ((), ()))
_HI = lax.Precision.HIGHEST


def _inproj_kernel(x_ref, g_ref, w_ref, cw_ref, cb_ref, z_ref, xs_ref, bc_ref,
                   dt_ref, xb_ref, tail_ref, *, tiles_per_seq):
    tile = x_ref.shape[0]
    d_inner = xs_ref.shape[1]
    conv_dim = tail_ref.shape[1]
    tail = SUBLANES

    @pl.when(pl.program_id(0) % tiles_per_seq == 0)
    def _():
        tail_ref[...] = jnp.zeros(tail_ref.shape, F32)

    h = _rms(x_ref[...], g_ref[...], NORM_EPS).astype(BF16)
    dt_ref[...] = _dot(h, w_ref[:, d_inner + conv_dim:])

    for gi, lo in enumerate(range(0, conv_dim, CONV_COLS)):
        cs = slice(lo, lo + CONV_COLS)
        stage = xb_ref.at[gi % 2]
        stage[0:tail, :] = tail_ref[:, cs]
        stage[tail:tail + tile, :] = _dot(h, w_ref[:, d_inner + lo:d_inner + lo + CONV_COLS])
        if lo < d_inner:
            z_ref[:, cs] = _dot(h, w_ref[:, cs])
        u = stage[...]
        tail_ref[:, cs] = u[tile:tile + tail, :]
        acc = cb_ref[:, cs] + u[tail:, :] * cw_ref[CONV_K - 1:CONV_K, cs]
        for back in range(1, CONV_K):
            acc = acc + pltpu.roll(u, back, 0)[tail:, :] * cw_ref[CONV_K - 1 - back:CONV_K - back, cs]
        act = acc * jax.nn.sigmoid(acc)
        if lo < d_inner:
            xs_ref[:, cs] = act
        else:
            bc_ref[:, lo - d_inner:lo - d_inner + CONV_COLS] = act.astype(BF16)


def _inproj(x, g, w, cw, cb, d_inner, tile, seq):
    T, D = x.shape
    nz, nx = d_inner, cw.shape[1]
    nd = w.shape[1] - nz - nx
    assert nz % CONV_COLS == 0 and nx % CONV_COLS == 0 and seq % tile == 0
    tok = lambda w: pl.BlockSpec((tile, w), lambda i: (i, 0))
    kern = functools.partial(_inproj_kernel, tiles_per_seq=seq // tile)
    return pl.pallas_call(
        kern,
        grid=(T // tile,),
        in_specs=[tok(D), _const_spec((1, D)), _const_spec(w.shape), _const_spec(cw.shape),
                  _const_spec(cb.shape)],
        out_specs=[tok(nz), tok(nz), tok(nx - nz), tok(nd)],
        out_shape=[jax.ShapeDtypeStruct((T, nz), F32), jax.ShapeDtypeStruct((T, nz), F32),
                   jax.ShapeDtypeStruct((T, nx - nz), BF16), jax.ShapeDtypeStruct((T, nd), F32)],
        scratch_shapes=[pltpu.VMEM((2, tile + SUBLANES, CONV_COLS), F32),
                        pltpu.VMEM((SUBLANES, nx), F32)],
        compiler_params=_params("arbitrary"),
        name="ssm_inproj",
    )(x, g, w, cw, cb)


def _ssd_kernel(z_ref, xs_ref, bc_ref, dtr_ref, dtb_ref, alog_ref, dexp_ref, gn_ref,
                expand_ref, tril_ref, y_ref, st_ref, *, d_inner, n_heads):
    L = SSD_CHUNK
    gw = d_inner // SSM_GROUPS
    hpg = n_heads // SSM_GROUPS
    gn_state = SSM_GROUPS * SSM_STATE
    seqs = range(z_ref.shape[0])

    @pl.when(pl.program_id(1) == 0)
    def _():
        st_ref[...] = jnp.zeros(st_ref.shape, F32)

    tril = tril_ref[...]
    causal = tril > 0.5

    def per_channel(v):
        hi = v.astype(BF16)
        r1 = v - hi.astype(F32)
        mid = r1.astype(BF16)
        lo = (r1 - mid.astype(F32)).astype(BF16)
        return _dot(jnp.concatenate([hi, mid, lo], axis=1), expand_ref[...])

    a_cs, a_cs_t, a_cs_x, dt_x = [], [], [], []
    for q in seqs:
        dt_in = dtr_ref[q] + dtb_ref[...]
        dt = jnp.maximum(dt_in, 0.0) + jnp.log1p(jnp.exp(-jnp.abs(dt_in)))
        a = dt * (-jnp.exp(alog_ref[...]))
        acs = _dot(tril, a, precision=_HI)
        a_cs.append(acs)
        a_cs_t.append(acs.T)
        a_cs_x.append(per_channel(acs))
        dt_x.append(per_channel(dt))

    for g in range(SSM_GROUPS):
        lo = g * gw
        for q in seqs:
            xs = xs_ref[q, :, lo:lo + gw]
            bm = bc_ref[q, :, g * SSM_STATE:(g + 1) * SSM_STATE]
            cm = bc_ref[q, :, gn_state + g * SSM_STATE:gn_state + (g + 1) * SSM_STATE]
            xdt = xs * dt_x[q][:, lo:lo + gw]
            cb = _dot(cm, bm, _NT)
            yd = []
            for r in range(hpg):
                hh = g * hpg + r
                seg = a_cs[q][:, hh:hh + 1] - a_cs_t[q][hh:hh + 1, :]
                lmat = jnp.exp(jnp.where(causal, seg, -jnp.inf))
                m = (cb * lmat).astype(BF16)
                yd.append(_dot(m, xdt[:, r * SSM_HEADDIM:(r + 1) * SSM_HEADDIM].astype(BF16)))
            y = jnp.concatenate(yd, axis=1)
            acx = a_cs_x[q][:, lo:lo + gw]
            alx = a_cs_x[q][L - 1:L, lo:lo + gw]
            prev = st_ref[q, g]
            y = y + _dot(cm, prev.astype(BF16)) * jnp.exp(acx)
            xd = (xdt * jnp.exp(alx - acx)).astype(BF16)
            st_ref[q, g] = prev * jnp.exp(alx) + _dot(bm, xd, _TN)
            y = y + xs * dexp_ref[:, lo:lo + gw]
            zg = z_ref[q, :, lo:lo + gw]
            y = y * (zg * jax.nn.sigmoid(zg))
            y = y * lax.rsqrt(jnp.mean(y * y, axis=-1, keepdims=True) + GATED_NORM_EPS)
            y_ref[q, :, lo:lo + gw] = (y * gn_ref[:, lo:lo + gw]).astype(y_ref.dtype)


def _ssd(z, xs, bc, dtr, dtb, alog, dexp, gn, expand, tril, batch):
    T, d_inner = z.shape
    H = dtr.shape[1]
    L = SSD_CHUNK
    seq = T // batch
    assert batch % SEQ_PER_STEP == 0 and seq % L == 0
    blk = lambda w: pl.BlockSpec((SEQ_PER_STEP, L, w), lambda b, c: (b, c, 0))
    per_seq = lambda v: v.reshape(batch, seq, v.shape[1])
    kern = functools.partial(_ssd_kernel, d_inner=d_inner, n_heads=H)
    y = pl.pallas_call(
        kern,
        grid=(batch // SEQ_PER_STEP, seq // L),
        in_specs=[blk(d_inner), blk(d_inner), blk(bc.shape[1]), blk(H),
                  _const_spec(dtb.shape), _const_spec(alog.shape), _const_spec(dexp.shape),
                  _const_spec(gn.shape), _const_spec(expand.shape), _const_spec(tril.shape)],
        out_specs=blk(d_inner),
        out_shape=jax.ShapeDtypeStruct((batch, seq, d_inner), BF16),
        scratch_shapes=[pltpu.VMEM((SEQ_PER_STEP, SSM_GROUPS, SSM_STATE, d_inner // SSM_GROUPS), F32)],
        compiler_params=_params("arbitrary", "arbitrary"),
        name="ssd_scan",
    )(per_seq(z), per_seq(xs), per_seq(bc), per_seq(dtr), dtb, alog, dexp, gn, expand, tril)
    return y.reshape(T, d_inner)


def _outproj_kernel(x_ref, y_ref, w_ref, o_ref):
    o_ref[...] = x_ref[...] + _dot(y_ref[...], w_ref[...])


def _outproj(x, y, w, tile):
    T, D = x.shape
    K = y.shape[1]
    return pl.pallas_call(
        _outproj_kernel,
        grid=(T // tile,),
        in_specs=[pl.BlockSpec((tile, D), lambda i: (i, 0)),
                  pl.BlockSpec((tile, K), lambda i: (i, 0)), _const_spec(w.shape)],
        out_specs=pl.BlockSpec((tile, D), lambda i: (i, 0)),
        out_shape=jax.ShapeDtypeStruct((T, D), F32),
        compiler_params=_params("arbitrary"),
        name="ssm_outproj",
    )(x, y, w)


def _peer_kernel(x_ref, g_ref, wq_ref, keys_ref, u0_ref, un_ref, vt_ref, o_ref,
                 hb_ref, e1_ref, e2_ref, gmin_ref, top_ref,
                 a0_ref, a1_ref, p0_ref, p1_ref, acca_ref, accb_ref):
    g = pl.program_id(1)
    last = pl.num_programs(1) - 1
    tt = x_ref.shape[0]
    ec = un_ref.shape[0] // 2
    rows_per_chunk = ec // N_KEYS
    mxu_cols = 2 * LANES
    n_piece = tt // mxu_cols
    half = N_KEYS
    ntop = PEER_TOPK + 1
    neg_inf = -jnp.inf

    @pl.when(g == 0)
    def _route():
        hb = _rms(x_ref[...], g_ref[...], NORM_EPS).T.astype(BF16)
        hb_ref[...] = hb
        for h in range(PEER_HEADS):
            q = _dot(wq_ref[h * 2 * half:(h + 1) * 2 * half, :], hb).astype(BF16)
            a0_ref[h * N_KEYS:(h + 1) * N_KEYS, :] = _dot(keys_ref[0], q[0:half, :])
            a1_ref[h * N_KEYS:(h + 1) * N_KEYS, :] = _dot(keys_ref[1], q[half:2 * half, :])

        def top_values(s_ref, h, cs):
            v = [s_ref[h * N_KEYS + r * SUBLANES:h * N_KEYS + (r + 1) * SUBLANES, cs]
                 for r in range(N_KEYS // SUBLANES)]
            for i, j in _SORT_PAIRS:
                v[i], v[j] = jnp.maximum(v[i], v[j]), jnp.minimum(v[i], v[j])
            depth = len(v)
            out = []
            for k in range(ntop):
                m = v[0]
                for shift in (4, 2, 1):
                    m = jnp.maximum(m, pltpu.roll(m, shift, 0))
                out.append(m)
                hit = v[0] == m
                for r in range(min(depth, ntop - 1 - k)):
                    v[r] = jnp.where(hit, v[r + 1] if r + 1 < depth else neg_inf, v[r])
            return out

        for tc in range(tt // LANES):
            cs = slice(tc * LANES, (tc + 1) * LANES)
            for c, s_ref in enumerate((a0_ref, a1_ref)):
                for h in range(PEER_HEADS):
                    for r, m in enumerate(top_values(s_ref, h, cs)):
                        top_ref[c, r, h:h + 1, cs] = m[0:1, :]

            a = [top_ref[0, r, :, cs] for r in range(ntop)]
            b = [top_ref[1, r, :, cs] for r in range(ntop)]
            cand = [a[r1] + b[r2] for r1, r2 in _CAND_PAIRS]
            a0, b0 = a[0], b[0]
            m0 = a0 + b0
            z = jnp.zeros(m0.shape, F32)
            for k in range(ntop):
                m = functools.reduce(jnp.maximum, cand)
                if k < PEER_TOPK:
                    z = z + jnp.exp(m - m0)
                if k == PEER_TOPK - 1:
                    v16 = m
                if k < ntop - 1:
                    cand = [jnp.where(cv == m, neg_inf, cv) for cv in cand]
            v17 = m
            zinv = np.float32(math.sqrt(0.5)) / z
            gmin = jnp.exp(0.5 * (v16 + v17) - m0) * zinv
            for h in range(PEER_HEADS):
                hk = slice(h * N_KEYS, (h + 1) * N_KEYS)
                e1_ref[h, tc] = jnp.exp(a0_ref[hk, cs] - a0[h:h + 1, :])
                e2_ref[h, tc] = jnp.exp(a1_ref[hk, cs] - b0[h:h + 1, :]) * zinv[h:h + 1, :]
                gmin_ref[h, tc] = gmin[h:h + 1, :]

            if (tc + 1) % (mxu_cols // LANES) == 0:
                ps = slice((tc + 1) * LANES - mxu_cols, (tc + 1) * LANES)
                a0_ref[:, ps] = _dot(u0_ref[0:ec, :], hb_ref[:, ps])
                a1_ref[:, ps] = _dot(u0_ref[ec:2 * ec, :], hb_ref[:, ps])

        acca_ref[...] = jnp.zeros(acca_ref.shape, F32)
        accb_ref[...] = jnp.zeros(accb_ref.shape, F32)

    def gate_gelu(a_ref, p_ref, chunk, tc):
        i0 = pl.multiple_of(chunk * rows_per_chunk, SUBLANES)
        cs = slice(tc * LANES, (tc + 1) * LANES)
        e18 = [e1_ref[h, tc, pl.ds(i0, rows_per_chunk), :] for h in range(PEER_HEADS)]
        gm = [gmin_ref[h, tc] for h in range(PEER_HEADS)]
        for ii in range(rows_per_chunk):
            rows = slice(ii * N_KEYS, (ii + 1) * N_KEYS)
            w = jnp.zeros((N_KEYS, LANES), F32)
            for h in range(PEER_HEADS):
                gate = e18[h][ii:ii + 1, :] * e2_ref[h, tc]
                w = w + jnp.where(gate >= gm[h], gate, 0.0)
            av = a_ref[rows, cs]
            p_ref[rows, cs] = ((w * av) * (1.0 + lax.erf(av))).astype(BF16)

    for par, (a_ref, p_ref, acc_ref) in enumerate(((a0_ref, p0_ref, acca_ref),
                                                   (a1_ref, p1_ref, accb_ref))):
        es = slice(par * ec, (par + 1) * ec)
        for piece in range(n_piece):
            cs = slice(piece * mxu_cols, (piece + 1) * mxu_cols)
            for tc in range(piece * (mxu_cols // LANES), (piece + 1) * (mxu_cols // LANES)):
                gate_gelu(a_ref, p_ref, 2 * g + par, tc)
            acc_ref[:, cs] += _dot(vt_ref[:, es], p_ref[:, cs])
            a_ref[:, cs] = _dot(un_ref[es, :], hb_ref[:, cs])

    @pl.when(g == last)
    def _finish():
        o_ref[...] = x_ref[...] + (acca_ref[...] + accb_ref[...]).T


def _peer(x, g, wq_t, keys, u, vt, layer, tile, ec):
    T, D = x.shape
    E = u.shape[1]
    nchunk = E // ec
    assert E == N_KEYS * N_KEYS and ec == SUBLANES * N_KEYS and tile % LANES == 0
    assert ec == PEER_HEADS * N_KEYS
    assert nchunk % 2 == 0
    ntc = tile // LANES
    nstep = nchunk // 2
    return pl.pallas_call(
        _peer_kernel,
        grid=(T // tile, nstep),
        in_specs=[pl.BlockSpec((tile, D), lambda i, s: (i, 0)),
                  _const_spec((1, D)),
                  pl.BlockSpec((None,) + wq_t.shape[1:], lambda i, s: (layer, 0, 0),
                               pipeline_mode=pl.Buffered(1)),
                  pl.BlockSpec((None,) + keys.shape[1:], lambda i, s: (layer, 0, 0, 0),
                               pipeline_mode=pl.Buffered(1)),
                  pl.BlockSpec((None, 2 * ec, D), lambda i, s: (layer, 0, 0),
                               pipeline_mode=pl.Buffered(1)),
                  pl.BlockSpec((None, 2 * ec, D),
                               lambda i, s: (layer, jnp.minimum(s + 1, nstep - 1), 0)),
                  pl.BlockSpec((None, D, 2 * ec), lambda i, s: (layer, 0, s))],
        out_specs=pl.BlockSpec((tile, D), lambda i, s: (i, 0)),
        out_shape=jax.ShapeDtypeStruct((T, D), F32),
        scratch_shapes=[pltpu.VMEM((D, tile), BF16),
                        pltpu.VMEM((PEER_HEADS, ntc, N_KEYS, LANES), F32),
                        pltpu.VMEM((PEER_HEADS, ntc, N_KEYS, LANES), F32),
                        pltpu.VMEM((PEER_HEADS, ntc, 1, LANES), F32),
                        pltpu.VMEM((2, PEER_TOPK + 1, PEER_HEADS, tile), F32),
                        pltpu.VMEM((ec, tile), F32), pltpu.VMEM((ec, tile), F32),
                        pltpu.VMEM((ec, tile), BF16), pltpu.VMEM((ec, tile), BF16),
                        pltpu.VMEM((D, tile), F32), pltpu.VMEM((D, tile), F32)],
        compiler_params=_params("arbitrary", "arbitrary"),
        name="peer",
    )(x, g, wq_t, keys, u, u, vt)


def _rope_tables(pos_ref, inv_ref):
    ang = pos_ref[...].astype(F32) * inv_ref[...]
    lane = lax.broadcasted_iota(jnp.int32, ang.shape, 1) % HEAD_DIM
    cos = jnp.cos(ang)
    sin = jnp.sin(ang)
    half = ROT_DIM // 2
    sin_lo = jnp.where(lane < half, -sin, 0.0)
    sin_hi = jnp.where((lane >= half) & (lane < ROT_DIM), sin, 0.0)
    return cos, sin_lo, sin_hi


def _rope_apply(t, cos, sin_lo, sin_hi):
    half = ROT_DIM // 2
    up = pltpu.roll(t, LANES - half, 1)
    dn = pltpu.roll(t, half, 1)
    return t * cos + up * sin_lo + dn * sin_hi


def _ple_core(x_ref, p_ref, g_ref, proj_ref, gw_ref):
    x = x_ref[...]
    hn = _rms(x, g_ref[...], NORM_EPS).astype(BF16)
    gate = jax.nn.sigmoid(_dot(hn, gw_ref[...]))
    return x + _dot(p_ref[...].astype(BF16), proj_ref[...]) * gate


def _ple_kv_kernel(x_ref, p_ref, g_ref, proj_ref, gw_ref, kvg_ref, kvw_ref, kvb_ref, pos_ref,
                   inv_ref, o_ref, k_ref, v_ref):
    x2 = _ple_core(x_ref, p_ref, g_ref, proj_ref, gw_ref)
    o_ref[...] = x2
    kv = _dot(_rms(x2, kvg_ref[...], NORM_EPS).astype(BF16), kvw_ref[...]) + kvb_ref[...]
    kvd = k_ref.shape[1]
    cos, sin_lo, sin_hi = _rope_tables(pos_ref, inv_ref)
    k_ref[...] = _rope_apply(kv[:, :kvd], cos, sin_lo, sin_hi).astype(k_ref.dtype)
    v_ref[...] = kv[:, kvd:].astype(v_ref.dtype)


def _ple_final_kernel(x_ref, p_ref, g_ref, proj_ref, gw_ref, fg_ref, o_ref):
    x2 = _ple_core(x_ref, p_ref, g_ref, proj_ref, gw_ref)
    o_ref[...] = _rms(x2, fg_ref[...], NORM_EPS)


def _ple_kv(x, p, g, proj, gw, kvg, kvw, kvb, pos, inv, tile):
    T, D = x.shape
    P = p.shape[1]
    kvd = kvw.shape[1] // 2
    tok = lambda w: pl.BlockSpec((tile, w), lambda i: (i, 0))
    return pl.pallas_call(
        _ple_kv_kernel,
        grid=(T // tile,),
        in_specs=[tok(D), tok(P), _const_spec((1, D)), _const_spec(proj.shape),
                  _const_spec(gw.shape), _const_spec((1, D)), _const_spec(kvw.shape),
                  _const_spec(kvb.shape), tok(1), _const_spec(inv.shape)],
        out_specs=[tok(D), tok(kvd), tok(kvd)],
        out_shape=[jax.ShapeDtypeStruct((T, D), F32), jax.ShapeDtypeStruct((T, kvd), BF16),
                   jax.ShapeDtypeStruct((T, kvd), BF16)],
        compiler_params=_params("arbitrary"),
        name="ple_kv",
    )(x, p, g, proj, gw, kvg, kvw, kvb, pos, inv)


def _ple_final(x, p, g, proj, gw, fg, tile):
    T, D = x.shape
    P = p.shape[1]
    tok = lambda w: pl.BlockSpec((tile, w), lambda i: (i, 0))
    return pl.pallas_call(
        _ple_final_kernel,
        grid=(T // tile,),
        in_specs=[tok(D), tok(P), _const_spec((1, D)), _const_spec(proj.shape),
                  _const_spec(gw.shape), _const_spec((1, D))],
        out_specs=tok(D),
        out_shape=jax.ShapeDtypeStruct((T, D), F32),
        compiler_params=_params("arbitrary"),
        name="ple_final",
    )(x, p, g, proj, gw, fg)


def _attn_kernel(x_ref, g_ref, qw_ref, qb_ref, sink_ref, owt_ref, ob_ref, kc_ref, kp_ref, vc_ref,
                 vp_ref, pos_ref, invc_ref, o_ref, qt_ref, att_ref, *, n_q_heads):
    tile = x_ref.shape[0]
    W = WINDOW
    nblk = tile // W
    q_per_kv = n_q_heads // N_KV_HEADS
    half = ROT_DIM // 2
    scale = HEAD_DIM ** -0.5
    first = pl.program_id(1) == 0

    h = _rms(x_ref[...], g_ref[...], NORM_EPS).astype(BF16)
    q = (_dot(h, qw_ref[...]) + qb_ref[...]) * scale
    qt = q.T
    ang = invc_ref[...] * pos_ref[...].astype(F32)
    cos, sin = jnp.cos(ang), jnp.sin(ang)
    for hh in range(n_q_heads):
        base = hh * HEAD_DIM
        t1 = qt[base:base + half, :]
        t2 = qt[base + half:base + ROT_DIM, :]
        qt_ref[base:base + half, :] = (t1 * cos - t2 * sin).astype(BF16)
        qt_ref[base + half:base + ROT_DIM, :] = (t2 * cos + t1 * sin).astype(BF16)
        qt_ref[base + ROT_DIM:base + HEAD_DIM, :] = qt[base + ROT_DIM:base + HEAD_DIM, :].astype(BF16)

    kj = lax.broadcasted_iota(jnp.int32, (2 * W, W), 0)
    qi = lax.broadcasted_iota(jnp.int32, (2 * W, W), 1)
    first_lo = jnp.where(first, W, 0)
    band = jnp.where((kj > qi) & (kj <= qi + W), 0.0, -jnp.inf)
    band_first = jnp.where(kj >= first_lo, band, -jnp.inf)
    band = jnp.concatenate([band] * q_per_kv, axis=1)
    band_first = jnp.concatenate([band_first] * q_per_kv, axis=1)

    for n in range(nblk):
        cols = slice(n * W, (n + 1) * W)
        if n == 0:
            kprev, vprev = kp_ref[...], vp_ref[...]
        else:
            kprev, vprev = kc_ref[(n - 1) * W:n * W, :], vc_ref[(n - 1) * W:n * W, :]
        kblk = jnp.concatenate([kprev, kc_ref[cols, :]], axis=0)
        vblk = jnp.concatenate([vprev, vc_ref[cols, :]], axis=0)
        bias = band_first if n == 0 else band
        for g in range(N_KV_HEADS):
            ks = kblk[:, g * HEAD_DIM:(g + 1) * HEAD_DIM]
            vs = vblk[:, g * HEAD_DIM:(g + 1) * HEAD_DIM]
            heads = range(g * q_per_kv, (g + 1) * q_per_kv)
            qs = jnp.concatenate([qt_ref[hh * HEAD_DIM:(hh + 1) * HEAD_DIM, cols] for hh in heads],
                                 axis=1)
            sink = jnp.concatenate([jnp.broadcast_to(sink_ref[:, hh:hh + 1], (1, W)) for hh in heads],
                                   axis=1)
            s = _dot(ks, qs) + bias
            m = jnp.maximum(jnp.max(s, axis=0, keepdims=True), sink)
            e = jnp.exp(s - m)
            denom = jnp.sum(e, axis=0, keepdims=True) + jnp.exp(sink - m)
            o = _dot(vs, e.astype(BF16), _TN) / denom
            for r, hh in enumerate(heads):
                att_ref[hh * HEAD_DIM:(hh + 1) * HEAD_DIM, cols] = o[:, r * W:(r + 1) * W].astype(BF16)

    o_ref[...] = x_ref[...] + _dot(owt_ref[...], att_ref[...]).T + ob_ref[...]


def _attn(x, g, qw, qb, sinks, ow_t, ob, k, v, pos, inv_col, batch, tile):
    T, D = x.shape
    kvd = k.shape[1]
    nq = qw.shape[1] // HEAD_DIM
    nt = T // batch // tile
    bpt = tile // WINDOW
    row = lambda b, i: (b * nt + i, 0)
    prev = lambda b, i: (jnp.maximum((b * nt + i) * bpt - 1, 0), 0)
    kern = functools.partial(_attn_kernel, n_q_heads=nq)
    return pl.pallas_call(
        kern,
        grid=(batch, nt),
        in_specs=[pl.BlockSpec((tile, D), row), _const_spec((1, D)), _const_spec(qw.shape),
                  _const_spec(qb.shape), _const_spec(sinks.shape), _const_spec(ow_t.shape),
                  _const_spec(ob.shape),
                  pl.BlockSpec((tile, kvd), row), pl.BlockSpec((WINDOW, kvd), prev),
                  pl.BlockSpec((tile, kvd), row), pl.BlockSpec((WINDOW, kvd), prev),
                  pl.BlockSpec((None, 1, tile), lambda b, i: (b * nt + i, 0, 0)),
                  _const_spec(inv_col.shape)],
        out_specs=pl.BlockSpec((tile, D), row),
        out_shape=jax.ShapeDtypeStruct((T, D), F32),
        scratch_shapes=[pltpu.VMEM((qw.shape[1], tile), BF16),
                        pltpu.VMEM((qw.shape[1], tile), BF16)],
        compiler_params=_params("arbitrary", "arbitrary"),
        name="swa_attn",
    )(x, g, qw, qb, sinks, ow_t, ob, k, k, v, v, pos.reshape(T // tile, 1, tile), inv_col)


def _row(v):
    return v.reshape(1, -1)


def kernel(x, p, positions, ssm_norm, ssm_in_w, ssm_conv_w, ssm_conv_b, ssm_dt_bias, ssm_A_log, ssm_D, ssm_gate_norm, ssm_out_w, kv_norm, kv_w, kv_b, attn_norm, q_w, q_b, sinks, o_w, o_b, peer_norm, peer_q_w, peer_sub_keys, peer_u, peer_v, ple_norm, ple_proj, ple_gate_w, final_norm):
    B, S, D = x.shape
    T = B * S
    depth = p.shape[0]
    n_a = ssm_norm.shape[0]
    H = ssm_D.shape[1]
    d_inner = H * SSM_HEADDIM
    conv_dim = ssm_conv_w.shape[2]

    xt = x.reshape(T, D)
    pos = positions.reshape(T, 1)
    lane = np.arange(LANES) % HEAD_DIM
    inv = np.where(lane < ROT_DIM,
                   ROPE_THETA ** (-(2.0 * (lane % (ROT_DIM // 2))) / ROT_DIM), 0.0)
    inv = jnp.asarray(inv.reshape(1, LANES), F32)
    inv_col = inv[0, :ROT_DIM // 2].reshape(-1, 1)
    expand = jnp.asarray(np.tile(np.repeat(np.eye(H, dtype=np.float32), SSM_HEADDIM, axis=1),
                                 (3, 1)), BF16)
    tril = jnp.asarray(np.tril(np.ones((SSD_CHUNK, SSD_CHUNK), np.float32)))

    peer_wq_t = peer_q_w.astype(BF16).transpose(0, 2, 1)
    peer_keys = peer_sub_keys.astype(BF16)
    peer_u_b = (peer_u * np.float32(math.sqrt(0.5))).astype(BF16)
    peer_vt = peer_v.astype(BF16).transpose(0, 2, 1)

    k_sh = v_sh = None
    for i in range(depth):
        if i < n_a:
            z, xs, bc, dtr = _inproj(xt, _row(ssm_norm[i]), ssm_in_w[i].astype(BF16), ssm_conv_w[i],
                                     _row(ssm_conv_b[i]), d_inner=d_inner, tile=INPROJ_TILE, seq=S)
            y = _ssd(z, xs, bc, dtr, _row(ssm_dt_bias[i]), _row(ssm_A_log[i]),
                     _row(jnp.repeat(ssm_D[i], SSM_HEADDIM)), _row(ssm_gate_norm[i]), expand, tril,
                     batch=B)
            xt = _outproj(xt, y, ssm_out_w[i].astype(BF16), tile=OUTPROJ_TILE)
        else:
            j = i - n_a
            xt = _attn(xt, _row(attn_norm[j]), q_w[j].astype(BF16), _row(q_b[j]), _row(sinks[j]),
                       o_w[j].T.astype(BF16), _row(o_b[j]), k_sh, v_sh, pos, inv_col, batch=B,
                       tile=ATTN_TILE)
        xt = _peer(xt, _row(peer_norm[i]), peer_wq_t, peer_keys, peer_u_b, peer_vt, layer=i,
                   tile=PEER_TILE, ec=PEER_CHUNK)
        if i == n_a - 1:
            xt, k_sh, v_sh = _ple_kv(xt, p[i].reshape(T, -1), _row(ple_norm[i]),
                                     ple_proj[i].astype(BF16), ple_gate_w[i].astype(BF16),
                                     _row(kv_norm), kv_w.astype(BF16), _row(kv_b), pos, inv,
                                     tile=PLE_TILE)
        elif i == depth - 1:
            xt = _ple_final(xt, p[i].reshape(T, -1), _row(ple_norm[i]), ple_proj[i].astype(BF16),
                            ple_gate_w[i].astype(BF16), _row(final_norm), tile=PLE_TILE)
        else:
            raise NotImplementedError("PLE without K/V or final norm")
    return xt.reshape(B, S, D)
```

```python
import functools
import math

import jax
import jax.numpy as jnp
import numpy as np
from jax import lax
from jax.experimental import pallas as pl
from jax.experimental.pallas import tpu as pltpu

F32 = jnp.float32
BF16 = jnp.bfloat16

NORM_EPS = 1e-6
GATED_NORM_EPS = 1e-5
SSM_HEADDIM = 64
SSM_GROUPS = 8
SSM_STATE = 128
CONV_K = 4
SSD_CHUNK = 128
SEQ_PER_STEP = 2
CONV_COLS = 256
HEAD_DIM = 64
N_KV_HEADS = 2
WINDOW = 128
ROT_DIM = HEAD_DIM // 4
ROPE_THETA = 500000.0
PEER_HEADS = 8
N_KEYS = 128
PEER_TOPK = 16

LANES = 128
SUBLANES = 8
VMEM_LIMIT = 56 * 1024 * 1024

INPROJ_TILE = 256
OUTPROJ_TILE = 512
ATTN_TILE = 512
PLE_TILE = 512
PEER_TILE = 512
PEER_CHUNK = SUBLANES * N_KEYS


def _sort_network(n):
    pairs = []
    p = 1
    while p < n:
        k = p
        while k >= 1:
            for j in range(k % p, n - k, 2 * k):
                for i in range(min(k, n - j - k)):
                    if (i + j) // (2 * p) == (i + j + k) // (2 * p):
                        pairs.append((i + j, i + j + k))
            k //= 2
        p *= 2
    return pairs


_SORT_PAIRS = _sort_network(N_KEYS // SUBLANES)


def _params(*sem):
    return pltpu.CompilerParams(dimension_semantics=sem, vmem_limit_bytes=VMEM_LIMIT)


def _const_spec(shape):
    nd = len(shape)
    return pl.BlockSpec(shape, lambda *_: (0,) * nd, pipeline_mode=pl.Buffered(1))


def _rms(x, g, eps):
    return x * lax.rsqrt(jnp.mean(x * x, axis=-1, keepdims=True) + eps) * g


def _dot(a, b, dims=None, precision=None):
    if dims is None:
        dims = (((a.ndim - 1,), (0,)), ((), ()))
    return lax.dot_general(a, b, dims, precision=precision, preferred_element_type=F32)


_NT = (((1,), (1,)), ((), ()))
_TN = (((0,), (0,)), ((), ()))
_HI = lax.Precision.HIGHEST


def _inproj_kernel(x_ref, g_ref, w_ref, cw_ref, cb_ref, z_ref, xs_ref, bc_ref,
                   dt_ref, xb_ref, tail_ref, *, tiles_per_seq):
    tile = x_ref.shape[0]
    d_inner = xs_ref.shape[1]
    conv_dim = tail_ref.shape[1]
    tail = SUBLANES

    @pl.when(pl.program_id(0) % tiles_per_seq == 0)
    def _():
        tail_ref[...] = jnp.zeros(tail_ref.shape, F32)

    h = _rms(x_ref[...], g_ref[...], NORM_EPS).astype(BF16)
    dt_ref[...] = _dot(h, w_ref[:, d_inner + conv_dim:])

    for gi, lo in enumerate(range(0, conv_dim, CONV_COLS)):
        cs = slice(lo, lo + CONV_COLS)
        stage = xb_ref.at[gi % 2]
        stage[0:tail, :] = tail_ref[:, cs]
        stage[tail:tail + tile, :] = _dot(h, w_ref[:, d_inner + lo:d_inner + lo + CONV_COLS])
        if lo < d_inner:
            z_ref[:, cs] = _dot(h, w_ref[:, cs])
        u = stage[...]
        tail_ref[:, cs] = u[tile:tile + tail, :]
        acc = cb_ref[:, cs] + u[tail:, :] * cw_ref[CONV_K - 1:CONV_K, cs]
        for back in range(1, CONV_K):
            acc = acc + pltpu.roll(u, back, 0)[tail:, :] * cw_ref[CONV_K - 1 - back:CONV_K - back, cs]
        act = acc * jax.nn.sigmoid(acc)
        if lo < d_inner:
            xs_ref[:, cs] = act
        else:
            bc_ref[:, lo - d_inner:lo - d_inner + CONV_COLS] = act.astype(BF16)


def _inproj(x, g, w, cw, cb, d_inner, tile, seq):
    T, D = x.shape
    nz, nx = d_inner, cw.shape[1]
    nd = w.shape[1] - nz - nx
    assert nz % CONV_COLS == 0 and nx % CONV_COLS == 0 and seq % tile == 0
    tok = lambda w: pl.BlockSpec((tile, w), lambda i: (i, 0))
    kern = functools.partial(_inproj_kernel, tiles_per_seq=seq // tile)
    return pl.pallas_call(
        kern,
        grid=(T // tile,),
        in_specs=[tok(D), _const_spec((1, D)), _const_spec(w.shape), _const_spec(cw.shape),
                  _const_spec(cb.shape)],
        out_specs=[tok(nz), tok(nz), tok(nx - nz), tok(nd)],
        out_shape=[jax.ShapeDtypeStruct((T, nz), F32), jax.ShapeDtypeStruct((T, nz), F32),
                   jax.ShapeDtypeStruct((T, nx - nz), BF16), jax.ShapeDtypeStruct((T, nd), F32)],
        scratch_shapes=[pltpu.VMEM((2, tile + SUBLANES, CONV_COLS), F32),
                        pltpu.VMEM((SUBLANES, nx), F32)],
        compiler_params=_params("arbitrary"),
        name="ssm_inproj",
    )(x, g, w, cw, cb)


def _ssd_kernel(z_ref, xs_ref, bc_ref, dtr_ref, dtb_ref, alog_ref, dexp_ref, gn_ref,
                expand_ref, tril_ref, y_ref, st_ref, *, d_inner, n_heads):
    L = SSD_CHUNK
    gw = d_inner // SSM_GROUPS
    hpg = n_heads // SSM_GROUPS
    gn_state = SSM_GROUPS * SSM_STATE
    seqs = range(z_ref.shape[0])

    @pl.when(pl.program_id(1) == 0)
    def _():
        st_ref[...] = jnp.zeros(st_ref.shape, F32)

    tril = tril_ref[...]
    causal = tril > 0.5

    def per_channel(v):
        hi = v.astype(BF16)
        r1 = v - hi.astype(F32)
        mid = r1.astype(BF16)
        lo = (r1 - mid.astype(F32)).astype(BF16)
        return _dot(jnp.concatenate([hi, mid, lo], axis=1), expand_ref[...])

    a_cs, a_cs_t, a_cs_x, dt_x = [], [], [], []
    for q in seqs:
        dt_in = dtr_ref[q] + dtb_ref[...]
        dt = jnp.maximum(dt_in, 0.0) + jnp.log1p(jnp.exp(-jnp.abs(dt_in)))
        a = dt * (-jnp.exp(alog_ref[...]))
        acs = _dot(tril, a, precision=_HI)
        a_cs.append(acs)
        a_cs_t.append(acs.T)
        a_cs_x.append(per_channel(acs))
        dt_x.append(per_channel(dt))

    for g in range(SSM_GROUPS):
        lo = g * gw
        for q in seqs:
            xs = xs_ref[q, :, lo:lo + gw]
            bm = bc_ref[q, :, g * SSM_STATE:(g + 1) * SSM_STATE]
            cm = bc_ref[q, :, gn_state + g * SSM_STATE:gn_state + (g + 1) * SSM_STATE]
            xdt = xs * dt_x[q][:, lo:lo + gw]
            cb = _dot(cm, bm, _NT)
            yd = []
            for r in range(hpg):
                hh = g * hpg + r
                seg = a_cs[q][:, hh:hh + 1] - a_cs_t[q][hh:hh + 1, :]
                lmat = jnp.exp(jnp.where(causal, seg, -jnp.inf))
                m = (cb * lmat).astype(BF16)
                yd.append(_dot(m, xdt[:, r * SSM_HEADDIM:(r + 1) * SSM_HEADDIM].astype(BF16)))
            y = jnp.concatenate(yd, axis=1)
            acx = a_cs_x[q][:, lo:lo + gw]
            alx = a_cs_x[q][L - 1:L, lo:lo + gw]
            prev = st_ref[q, g]
            y = y + _dot(cm, prev.astype(BF16)) * jnp.exp(acx)
            xd = (xdt * jnp.exp(alx - acx)).astype(BF16)
            st_ref[q, g] = prev * jnp.exp(alx) + _dot(bm, xd, _TN)
            y = y + xs * dexp_ref[:, lo:lo + gw]
            zg = z_ref[q, :, lo:lo + gw]
            y = y * (zg * jax.nn.sigmoid(zg))
            y = y * lax.rsqrt(jnp.mean(y * y, axis=-1, keepdims=True) + GATED_NORM_EPS)
            y_ref[q, :, lo:lo + gw] = (y * gn_ref[:, lo:lo + gw]).astype(y_ref.dtype)


def _ssd(z, xs, bc, dtr, dtb, alog, dexp, gn, expand, tril, batch):
    T, d_inner = z.shape
    H = dtr.shape[1]
    L = SSD_CHUNK
    seq = T // batch
    assert batch % SEQ_PER_STEP == 0 and seq % L == 0
    blk = lambda w: pl.BlockSpec((SEQ_PER_STEP, L, w), lambda b, c: (b, c, 0))
    per_seq = lambda v: v.reshape(batch, seq, v.shape[1])
    kern = functools.partial(_ssd_kernel, d_inner=d_inner, n_heads=H)
    y = pl.pallas_call(
        kern,
        grid=(batch // SEQ_PER_STEP, seq // L),
        in_specs=[blk(d_inner), blk(d_inner), blk(bc.shape[1]), blk(H),
                  _const_spec(dtb.shape), _const_spec(alog.shape), _const_spec(dexp.shape),
                  _const_spec(gn.shape), _const_spec(expand.shape), _const_spec(tril.shape)],
        out_specs=blk(d_inner),
        out_shape=jax.ShapeDtypeStruct((batch, seq, d_inner), BF16),
        scratch_shapes=[pltpu.VMEM((SEQ_PER_STEP, SSM_GROUPS, SSM_STATE, d_inner // SSM_GROUPS), F32)],
        compiler_params=_params("arbitrary", "arbitrary"),
        name="ssd_scan",
    )(per_seq(z), per_seq(xs), per_seq(bc), per_seq(dtr), dtb, alog, dexp, gn, expand, tril)
    return y.reshape(T, d_inner)


def _outproj_kernel(x_ref, y_ref, w_ref, o_ref):
    o_ref[...] = x_ref[...] + _dot(y_ref[...], w_ref[...])


def _outproj(x, y, w, tile):
    T, D = x.shape
    K = y.shape[1]
    return pl.pallas_call(
        _outproj_kernel,
        grid=(T // tile,),
        in_specs=[pl.BlockSpec((tile, D), lambda i: (i, 0)),
                  pl.BlockSpec((tile, K), lambda i: (i, 0)), _const_spec(w.shape)],
        out_specs=pl.BlockSpec((tile, D), lambda i: (i, 0)),
        out_shape=jax.ShapeDtypeStruct((T, D), F32),
        compiler_params=_params("arbitrary"),
        name="ssm_outproj",
    )(x, y, w)


def _peer_kernel(x_ref, g_ref, wq_ref, keys_ref, u0_ref, un_ref, vt_ref, o_ref,
                 hb_ref, e1_ref, e2_ref, gmin_ref, top_ref,
                 a0_ref, a1_ref, p0_ref, p1_ref, acca_ref, accb_ref):
    g = pl.program_id(1)
    last = pl.num_programs(1) - 1
    tt = x_ref.shape[0]
    ec = un_ref.shape[0] // 2
    rows_per_chunk = ec // N_KEYS
    mxu_cols = 2 * LANES
    n_piece = tt // mxu_cols
    half = N_KEYS
    ntop = PEER_TOPK + 1
    neg_inf = -jnp.inf

    @pl.when(g == 0)
    def _route():
        hb = _rms(x_ref[...], g_ref[...], NORM_EPS).T.astype(BF16)
        hb_ref[...] = hb
        for h in range(PEER_HEADS):
            q = _dot(wq_ref[h * 2 * half:(h + 1) * 2 * half, :], hb).astype(BF16)
            a0_ref[h * N_KEYS:(h + 1) * N_KEYS, :] = _dot(keys_ref[0], q[0:half, :])
            a1_ref[h * N_KEYS:(h + 1) * N_KEYS, :] = _dot(keys_ref[1], q[half:2 * half, :])

        def top_values(s_ref, h, cs):
            v = [s_ref[h * N_KEYS + r * SUBLANES:h * N_KEYS + (r + 1) * SUBLANES, cs]
                 for r in range(N_KEYS // SUBLANES)]
            for i, j in _SORT_PAIRS:
                v[i], v[j] = jnp.maximum(v[i], v[j]), jnp.minimum(v[i], v[j])
            depth = len(v)
            out = []
            for k in range(ntop):
                m = v[0]
                for shift in (4, 2, 1):
                    m = jnp.maximum(m, pltpu.roll(m, shift, 0))
                out.append(m)
                hit = v[0] == m
                for r in range(min(depth, ntop - 1 - k)):
                    v[r] = jnp.where(hit, v[r + 1] if r + 1 < depth else neg_inf, v[r])
            return out

        for tc in range(tt // LANES):
            cs = slice(tc * LANES, (tc + 1) * LANES)
            for c, s_ref in enumerate((a0_ref, a1_ref)):
                for h in range(PEER_HEADS):
                    for r, m in enumerate(top_values(s_ref, h, cs)):
                        top_ref[c, r, h:h + 1, cs] = m[0:1, :]

            a = [top_ref[0, r, :, cs] for r in range(ntop)]
            b = [top_ref[1, r, :, cs] for r in range(ntop)]
            lists = [[a[r1] + b[r2] for r2 in range(ntop // (r1 + 1))] for r1 in range(ntop)]
            a0, b0 = a[0], b[0]
            m0 = a0 + b0
            z = jnp.zeros(m0.shape, F32)
            for k in range(ntop):
                m = functools.reduce(jnp.maximum, [lst[0] for lst in lists])
                if k < PEER_TOPK:
                    z = z + jnp.exp(m - m0)
                if k == PEER_TOPK - 1:
                    v16 = m
                if k < ntop - 1:
                    for lst in lists:
                        hit = lst[0] == m
                        for j in range(len(lst)):
                            lst[j] = jnp.where(hit, lst[j + 1] if j + 1 < len(lst) else neg_inf, lst[j])
            v17 = m
            zinv = np.float32(math.sqrt(0.5)) / z
            gmin = jnp.exp(0.5 * (v16 + v17) - m0) * zinv
            for h in range(PEER_HEADS):
                hk = slice(h * N_KEYS, (h + 1) * N_KEYS)
                e1_ref[h, tc] = jnp.exp(a0_ref[hk, cs] - a0[h:h + 1, :])
                e2_ref[h, tc] = jnp.exp(a1_ref[hk, cs] - b0[h:h + 1, :]) * zinv[h:h + 1, :]
                gmin_ref[h, tc] = gmin[h:h + 1, :]

            if (tc + 1) % (mxu_cols // LANES) == 0:
                ps = slice((tc + 1) * LANES - mxu_cols, (tc + 1) * LANES)
                a0_ref[:, ps] = _dot(u0_ref[0:ec, :], hb_ref[:, ps])
                a1_ref[:, ps] = _dot(u0_ref[ec:2 * ec, :], hb_ref[:, ps])

        acca_ref[...] = jnp.zeros(acca_ref.shape, F32)
        accb_ref[...] = jnp.zeros(accb_ref.shape, F32)

    def gate_gelu(a_ref, p_ref, chunk, tc):
        i0 = pl.multiple_of(chunk * rows_per_chunk, SUBLANES)
        cs = slice(tc * LANES, (tc + 1) * LANES)
        e18 = [e1_ref[h, tc, pl.ds(i0, rows_per_chunk), :] for h in range(PEER_HEADS)]
        gm = [gmin_ref[h, tc] for h in range(PEER_HEADS)]
        for ii in range(rows_per_chunk):
            rows = slice(ii * N_KEYS, (ii + 1) * N_KEYS)
            w = jnp.zeros((N_KEYS, LANES), F32)
            for h in range(PEER_HEADS):
                gate = e18[h][ii:ii + 1, :] * e2_ref[h, tc]
                w = w + jnp.where(gate >= gm[h], gate, 0.0)
            av = a_ref[rows, cs]
            p_ref[rows, cs] = ((w * av) * (1.0 + lax.erf(av))).astype(BF16)

    for par, (a_ref, p_ref, acc_ref) in enumerate(((a0_ref, p0_ref, acca_ref),
                                                   (a1_ref, p1_ref, accb_ref))):
        es = slice(par * ec, (par + 1) * ec)
        for piece in range(n_piece):
            cs = slice(piece * mxu_cols, (piece + 1) * mxu_cols)
            for tc in range(piece * (mxu_cols // LANES), (piece + 1) * (mxu_cols // LANES)):
                gate_gelu(a_ref, p_ref, 2 * g + par, tc)
            acc_ref[:, cs] += _dot(vt_ref[:, es], p_ref[:, cs])
            a_ref[:, cs] = _dot(un_ref[es, :], hb_ref[:, cs])

    @pl.when(g == last)
    def _finish():
        o_ref[...] = x_ref[...] + (acca_ref[...] + accb_ref[...]).T


def _peer(x, g, wq_t, keys, u, vt, layer, tile, ec):
    T, D = x.shape
    E = u.shape[1]
    nchunk = E // ec
    assert E == N_KEYS * N_KEYS and ec == SUBLANES * N_KEYS and tile % LANES == 0
    assert ec == PEER_HEADS * N_KEYS
    assert nchunk % 2 == 0
    ntc = tile // LANES
    nstep = nchunk // 2
    return pl.pallas_call(
        _peer_kernel,
        grid=(T // tile, nstep),
        in_specs=[pl.BlockSpec((tile, D), lambda i, s: (i, 0)),
                  _const_spec((1, D)),
                  pl.BlockSpec((None,) + wq_t.shape[1:], lambda i, s: (layer, 0, 0),
                               pipeline_mode=pl.Buffered(1)),
                  pl.BlockSpec((None,) + keys.shape[1:], lambda i, s: (layer, 0, 0, 0),
                               pipeline_mode=pl.Buffered(1)),
                  pl.BlockSpec((None, 2 * ec, D), lambda i, s: (layer, 0, 0),
                               pipeline_mode=pl.Buffered(1)),
                  pl.BlockSpec((None, 2 * ec, D),
                               lambda i, s: (layer, jnp.minimum(s + 1, nstep - 1), 0)),
                  pl.BlockSpec((None, D, 2 * ec), lambda i, s: (layer, 0, s))],
        out_specs=pl.BlockSpec((tile, D), lambda i, s: (i, 0)),
        out_shape=jax.ShapeDtypeStruct((T, D), F32),
        scratch_shapes=[pltpu.VMEM((D, tile), BF16),
                        pltpu.VMEM((PEER_HEADS, ntc, N_KEYS, LANES), F32),
                        pltpu.VMEM((PEER_HEADS, ntc, N_KEYS, LANES), F32),
                        pltpu.VMEM((PEER_HEADS, ntc, 1, LANES), F32),
                        pltpu.VMEM((2, PEER_TOPK + 1, PEER_HEADS, tile), F32),
                        pltpu.VMEM((ec, tile), F32), pltpu.VMEM((ec, tile), F32),
                        pltpu.VMEM((ec, tile), BF16), pltpu.VMEM((ec, tile), BF16),
                        pltpu.VMEM((D, tile), F32), pltpu.VMEM((D, tile), F32)],
        compiler_params=_params("arbitrary", "arbitrary"),
        name="peer",
    )(x, g, wq_t, keys, u, u, vt)


def _rope_tables(pos_ref, inv_ref):
    ang = pos_ref[...].astype(F32) * inv_ref[...]
    lane = lax.broadcasted_iota(jnp.int32, ang.shape, 1) % HEAD_DIM
    cos = jnp.cos(ang)
    sin = jnp.sin(ang)
    half = ROT_DIM // 2
    sin_lo = jnp.where(lane < half, -sin, 0.0)
    sin_hi = jnp.where((lane >= half) & (lane < ROT_DIM), sin, 0.0)
    return cos, sin_lo, sin_hi


def _rope_apply(t, cos, sin_lo, sin_hi):
    half = ROT_DIM // 2
    up = pltpu.roll(t, LANES - half, 1)
    dn = pltpu.roll(t, half, 1)
    return t * cos + up * sin_lo + dn * sin_hi


def _ple_core(x_ref, p_ref, g_ref, proj_ref, gw_ref):
    x = x_ref[...]
    hn = _rms(x, g_ref[...], NORM_EPS).astype(BF16)
    gate = jax.nn.sigmoid(_dot(hn, gw_ref[...]))
    return x + _dot(p_ref[...].astype(BF16), proj_ref[...]) * gate


def _ple_kv_kernel(x_ref, p_ref, g_ref, proj_ref, gw_ref, kvg_ref, kvw_ref, kvb_ref, pos_ref,
                   inv_ref, o_ref, k_ref, v_ref):
    x2 = _ple_core(x_ref, p_ref, g_ref, proj_ref, gw_ref)
    o_ref[...] = x2
    kv = _dot(_rms(x2, kvg_ref[...], NORM_EPS).astype(BF16), kvw_ref[...]) + kvb_ref[...]
    kvd = k_ref.shape[1]
    cos, sin_lo, sin_hi = _rope_tables(pos_ref, inv_ref)
    k_ref[...] = _rope_apply(kv[:, :kvd], cos, sin_lo, sin_hi).astype(k_ref.dtype)
    v_ref[...] = kv[:, kvd:].astype(v_ref.dtype)


def _ple_final_kernel(x_ref, p_ref, g_ref, proj_ref, gw_ref, fg_ref, o_ref):
    x2 = _ple_core(x_ref, p_ref, g_ref, proj_ref, gw_ref)
    o_ref[...] = _rms(x2, fg_ref[...], NORM_EPS)


def _ple_kv(x, p, g, proj, gw, kvg, kvw, kvb, pos, inv, tile):
    T, D = x.shape
    P = p.shape[1]
    kvd = kvw.shape[1] // 2
    tok = lambda w: pl.BlockSpec((tile, w), lambda i: (i, 0))
    return pl.pallas_call(
        _ple_kv_kernel,
        grid=(T // tile,),
        in_specs=[tok(D), tok(P), _const_spec((1, D)), _const_spec(proj.shape),
                  _const_spec(gw.shape), _const_spec((1, D)), _const_spec(kvw.shape),
                  _const_spec(kvb.shape), tok(1), _const_spec(inv.shape)],
        out_specs=[tok(D), tok(kvd), tok(kvd)],
        out_shape=[jax.ShapeDtypeStruct((T, D), F32), jax.ShapeDtypeStruct((T, kvd), BF16),
                   jax.ShapeDtypeStruct((T, kvd), BF16)],
        compiler_params=_params("arbitrary"),
        name="ple_kv",
    )(x, p, g, proj, gw, kvg, kvw, kvb, pos, inv)


def _ple_final(x, p, g, proj, gw, fg, tile):
    T, D = x.shape
    P = p.shape[1]
    tok = lambda w: pl.BlockSpec((tile, w), lambda i: (i, 0))
    return pl.pallas_call(
        _ple_final_kernel,
        grid=(T // tile,),
        in_specs=[tok(D), tok(P), _const_spec((1, D)), _const_spec(proj.shape),
                  _const_spec(gw.shape), _const_spec((1, D))],
        out_specs=tok(D),
        out_shape=jax.ShapeDtypeStruct((T, D), F32),
        compiler_params=_params("arbitrary"),
        name="ple_final",
    )(x, p, g, proj, gw, fg)


def _attn_kernel(x_ref, g_ref, qw_ref, qb_ref, sink_ref, owt_ref, ob_ref, kc_ref, kp_ref, vc_ref,
                 vp_ref, pos_ref, invc_ref, o_ref, qt_ref, att_ref, *, n_q_heads):
    tile = x_ref.shape[0]
    W = WINDOW
    nblk = tile // W
    q_per_kv = n_q_heads // N_KV_HEADS
    half = ROT_DIM // 2
    scale = HEAD_DIM ** -0.5
    first = pl.program_id(1) == 0

    h = _rms(x_ref[...], g_ref[...], NORM_EPS).astype(BF16)
    q = (_dot(h, qw_ref[...]) + qb_ref[...]) * scale
    qt = q.T
    ang = invc_ref[...] * pos_ref[...].astype(F32)
    cos, sin = jnp.cos(ang), jnp.sin(ang)
    for hh in range(n_q_heads):
        base = hh * HEAD_DIM
        t1 = qt[base:base + half, :]
        t2 = qt[base + half:base + ROT_DIM, :]
        qt_ref[base:base + half, :] = (t1 * cos - t2 * sin).astype(BF16)
        qt_ref[base + half:base + ROT_DIM, :] = (t2 * cos + t1 * sin).astype(BF16)
        qt_ref[base + ROT_DIM:base + HEAD_DIM, :] = qt[base + ROT_DIM:base + HEAD_DIM, :].astype(BF16)

    kj = lax.broadcasted_iota(jnp.int32, (2 * W, W), 0)
    qi = lax.broadcasted_iota(jnp.int32, (2 * W, W), 1)
    first_lo = jnp.where(first, W, 0)
    band = jnp.where((kj > qi) & (kj <= qi + W), 0.0, -jnp.inf)
    band_first = jnp.where(kj >= first_lo, band, -jnp.inf)
    band = jnp.concatenate([band] * q_per_kv, axis=1)
    band_first = jnp.concatenate([band_first] * q_per_kv, axis=1)

    for n in range(nblk):
        cols = slice(n * W, (n + 1) * W)
        if n == 0:
            kprev, vprev = kp_ref[...], vp_ref[...]
        else:
            kprev, vprev = kc_ref[(n - 1) * W:n * W, :], vc_ref[(n - 1) * W:n * W, :]
        kblk = jnp.concatenate([kprev, kc_ref[cols, :]], axis=0)
        vblk = jnp.concatenate([vprev, vc_ref[cols, :]], axis=0)
        bias = band_first if n == 0 else band
        for g in range(N_KV_HEADS):
            ks = kblk[:, g * HEAD_DIM:(g + 1) * HEAD_DIM]
            vs = vblk[:, g * HEAD_DIM:(g + 1) * HEAD_DIM]
            heads = range(g * q_per_kv, (g + 1) * q_per_kv)
            qs = jnp.concatenate([qt_ref[hh * HEAD_DIM:(hh + 1) * HEAD_DIM, cols] for hh in heads],
                                 axis=1)
            sink = jnp.concatenate([jnp.broadcast_to(sink_ref[:, hh:hh + 1], (1, W)) for hh in heads],
                                   axis=1)
            s = _dot(ks, qs) + bias
            m = jnp.maximum(jnp.max(s, axis=0, keepdims=True), sink)
            e = jnp.exp(s - m)
            denom = jnp.sum(e, axis=0, keepdims=True) + jnp.exp(sink - m)
            o = _dot(vs, e.astype(BF16), _TN) / denom
            for r, hh in enumerate(heads):
                att_ref[hh * HEAD_DIM:(hh + 1) * HEAD_DIM, cols] = o[:, r * W:(r + 1) * W].astype(BF16)

    o_ref[...] = x_ref[...] + _dot(owt_ref[...], att_ref[...]).T + ob_ref[...]


def _attn(x, g, qw, qb, sinks, ow_t, ob, k, v, pos, inv_col, batch, tile):
    T, D = x.shape
    kvd = k.shape[1]
    nq = qw.shape[1] // HEAD_DIM
    nt = T // batch // tile
    bpt = tile // WINDOW
    row = lambda b, i: (b * nt + i, 0)
    prev = lambda b, i: (jnp.maximum((b * nt + i) * bpt - 1, 0), 0)
    kern = functools.partial(_attn_kernel, n_q_heads=nq)
    return pl.pallas_call(
        kern,
        grid=(batch, nt),
        in_specs=[pl.BlockSpec((tile, D), row), _const_spec((1, D)), _const_spec(qw.shape),
                  _const_spec(qb.shape), _const_spec(sinks.shape), _const_spec(ow_t.shape),
                  _const_spec(ob.shape),
                  pl.BlockSpec((tile, kvd), row), pl.BlockSpec((WINDOW, kvd), prev),
                  pl.BlockSpec((tile, kvd), row), pl.BlockSpec((WINDOW, kvd), prev),
                  pl.BlockSpec((None, 1, tile), lambda b, i: (b * nt + i, 0, 0)),
                  _const_spec(inv_col.shape)],
        out_specs=pl.BlockSpec((tile, D), row),
        out_shape=jax.ShapeDtypeStruct((T, D), F32),
        scratch_shapes=[pltpu.VMEM((qw.shape[1], tile), BF16),
                        pltpu.VMEM((qw.shape[1], tile), BF16)],
        compiler_params=_params("arbitrary", "arbitrary"),
        name="swa_attn",
    )(x, g, qw, qb, sinks, ow_t, ob, k, k, v, v, pos.reshape(T // tile, 1, tile), inv_col)


def _row(v):
    return v.reshape(1, -1)


def kernel(x, p, positions, ssm_norm, ssm_in_w, ssm_conv_w, ssm_conv_b, ssm_dt_bias, ssm_A_log, ssm_D, ssm_gate_norm, ssm_out_w, kv_norm, kv_w, kv_b, attn_norm, q_w, q_b, sinks, o_w, o_b, peer_norm, peer_q_w, peer_sub_keys, peer_u, peer_v, ple_norm, ple_proj, ple_gate_w, final_norm):
    B, S, D = x.shape
    T = B * S
    depth = p.shape[0]
    n_a = ssm_norm.shape[0]
    H = ssm_D.shape[1]
    d_inner = H * SSM_HEADDIM
    conv_dim = ssm_conv_w.shape[2]

    xt = x.reshape(T, D)
    pos = positions.reshape(T, 1)
    lane = np.arange(LANES) % HEAD_DIM
    inv = np.where(lane < ROT_DIM,
                   ROPE_THETA ** (-(2.0 * (lane % (ROT_DIM // 2))) / ROT_DIM), 0.0)
    inv = jnp.asarray(inv.reshape(1, LANES), F32)
    inv_col = inv[0, :ROT_DIM // 2].reshape(-1, 1)
    expand = jnp.asarray(np.tile(np.repeat(np.eye(H, dtype=np.float32), SSM_HEADDIM, axis=1),
                                 (3, 1)), BF16)
    tril = jnp.asarray(np.tril(np.ones((SSD_CHUNK, SSD_CHUNK), np.float32)))

    peer_wq_t = peer_q_w.astype(BF16).transpose(0, 2, 1)
    peer_keys = peer_sub_keys.astype(BF16)
    peer_u_b = (peer_u * np.float32(math.sqrt(0.5))).astype(BF16)
    peer_vt = peer_v.astype(BF16).transpose(0, 2, 1)

    k_sh = v_sh = None
    for i in range(depth):
        if i < n_a:
            z, xs, bc, dtr = _inproj(xt, _row(ssm_norm[i]), ssm_in_w[i].astype(BF16), ssm_conv_w[i],
                                     _row(ssm_conv_b[i]), d_inner=d_inner, tile=INPROJ_TILE, seq=S)
            y = _ssd(z, xs, bc, dtr, _row(ssm_dt_bias[i]), _row(ssm_A_log[i]),
                     _row(jnp.repeat(ssm_D[i], SSM_HEADDIM)), _row(ssm_gate_norm[i]), expand, tril,
                     batch=B)
            xt = _outproj(xt, y, ssm_out_w[i].astype(BF16), tile=OUTPROJ_TILE)
        else:
            j = i - n_a
            xt = _attn(xt, _row(attn_norm[j]), q_w[j].astype(BF16), _row(q_b[j]), _row(sinks[j]),
                       o_w[j].T.astype(BF16), _row(o_b[j]), k_sh, v_sh, pos, inv_col, batch=B,
                       tile=ATTN_TILE)
        xt = _peer(xt, _row(peer_norm[i]), peer_wq_t, peer_keys, peer_u_b, peer_vt, layer=i,
                   tile=PEER_TILE, ec=PEER_CHUNK)
        if i == n_a - 1:
            xt, k_sh, v_sh = _ple_kv(xt, p[i].reshape(T, -1), _row(ple_norm[i]),
                                     ple_proj[i].astype(BF16), ple_gate_w[i].astype(BF16),
                                     _row(kv_norm), kv_w.astype(BF16), _row(kv_b), pos, inv,
                                     tile=PLE_TILE)
        elif i == depth - 1:
            xt = _ple_final(xt, p[i].reshape(T, -1), _row(ple_norm[i]), ple_proj[i].astype(BF16),
                            ple_gate_w[i].astype(BF16), _row(final_norm), tile=PLE_TILE)
        else:
            raise NotImplementedError("PLE without K/V or final norm")
    return xt.reshape(B, S, D)
```

```python
import functools
import math

import jax
import jax.numpy as jnp
import numpy as np
from jax import lax
from jax.experimental import pallas as pl
from jax.experimental.pallas import tpu as pltpu

F32 = jnp.float32
BF16 = jnp.bfloat16

NORM_EPS = 1e-6
GATED_NORM_EPS = 1e-5
SSM_HEADDIM = 64
SSM_GROUPS = 8
SSM_STATE = 128
CONV_K = 4
SSD_CHUNK = 128
SEQ_PER_STEP = 2
CONV_COLS = 256
HEAD_DIM = 64
N_KV_HEADS = 2
WINDOW = 128
ROT_DIM = HEAD_DIM // 4
ROPE_THETA = 500000.0
PEER_HEADS = 8
N_KEYS = 128
PEER_TOPK = 16

LANES = 128
SUBLANES = 8
VMEM_LIMIT = 56 * 1024 * 1024

INPROJ_TILE = 256
OUTPROJ_TILE = 512
ATTN_TILE = 512
PLE_TILE = 512
PEER_TILE = 512
PEER_CHUNK = SUBLANES * N_KEYS


def _sort_network(n):
    pairs = []
    p = 1
    while p < n:
        k = p
        while k >= 1:
            for j in range(k % p, n - k, 2 * k):
                for i in range(min(k, n - j - k)):
                    if (i + j) // (2 * p) == (i + j + k) // (2 * p):
                        pairs.append((i + j, i + j + k))
            k //= 2
        p *= 2
    return pairs


_SORT_PAIRS = _sort_network(N_KEYS // SUBLANES)


def _params(*sem):
    return pltpu.CompilerParams(dimension_semantics=sem, vmem_limit_bytes=VMEM_LIMIT)


def _const_spec(shape):
    nd = len(shape)
    return pl.BlockSpec(shape, lambda *_: (0,) * nd, pipeline_mode=pl.Buffered(1))


def _rms(x, g, eps):
    return x * lax.rsqrt(jnp.mean(x * x, axis=-1, keepdims=True) + eps) * g


def _dot(a, b, dims=None, precision=None):
    if dims is None:
        dims = (((a.ndim - 1,), (0,)), ((), ()))
    return lax.dot_general(a, b, dims, precision=precision, preferred_element_type=F32)


_NT = (((1,), (1,)), ((), ()))
_TN = (((0,), (0,)), ((), ()))
_HI = lax.Precision.HIGHEST


def _inproj_kernel(x_ref, g_ref, w_ref, cw_ref, cb_ref, z_ref, xs_ref, bc_ref,
                   dt_ref, xb_ref, tail_ref, *, tiles_per_seq):
    tile = x_ref.shape[0]
    d_inner = xs_ref.shape[1]
    conv_dim = tail_ref.shape[1]
    tail = SUBLANES

    @pl.when(pl.program_id(0) % tiles_per_seq == 0)
    def _():
        tail_ref[...] = jnp.zeros(tail_ref.shape, F32)

    h = _rms(x_ref[...], g_ref[...], NORM_EPS).astype(BF16)
    dt_ref[...] = _dot(h, w_ref[:, d_inner + conv_dim:])

    for gi, lo in enumerate(range(0, conv_dim, CONV_COLS)):
        cs = slice(lo, lo + CONV_COLS)
        stage = xb_ref.at[gi % 2]
        stage[0:tail, :] = tail_ref[:, cs]
        stage[tail:tail + tile, :] = _dot(h, w_ref[:, d_inner + lo:d_inner + lo + CONV_COLS])
        if lo < d_inner:
            z_ref[:, cs] = _dot(h, w_ref[:, cs])
        u = stage[...]
        tail_ref[:, cs] = u[tile:tile + tail, :]
        acc = cb_ref[:, cs] + u[tail:, :] * cw_ref[CONV_K - 1:CONV_K, cs]
        for back in range(1, CONV_K):
            acc = acc + pltpu.roll(u, back, 0)[tail:, :] * cw_ref[CONV_K - 1 - back:CONV_K - back, cs]
        act = acc * jax.nn.sigmoid(acc)
        if lo < d_inner:
            xs_ref[:, cs] = act
        else:
            bc_ref[:, lo - d_inner:lo - d_inner + CONV_COLS] = act.astype(BF16)


def _inproj(x, g, w, cw, cb, d_inner, tile, seq):
    T, D = x.shape
    nz, nx = d_inner, cw.shape[1]
    nd = w.shape[1] - nz - nx
    assert nz % CONV_COLS == 0 and nx % CONV_COLS == 0 and seq % tile == 0
    tok = lambda w: pl.BlockSpec((tile, w), lambda i: (i, 0))
    kern = functools.partial(_inproj_kernel, tiles_per_seq=seq // tile)
    return pl.pallas_call(
        kern,
        grid=(T // tile,),
        in_specs=[tok(D), _const_spec((1, D)), _const_spec(w.shape), _const_spec(cw.shape),
                  _const_spec(cb.shape)],
        out_specs=[tok(nz), tok(nz), tok(nx - nz), tok(nd)],
        out_shape=[jax.ShapeDtypeStruct((T, nz), F32), jax.ShapeDtypeStruct((T, nz), F32),
                   jax.ShapeDtypeStruct((T, nx - nz), BF16), jax.ShapeDtypeStruct((T, nd), F32)],
        scratch_shapes=[pltpu.VMEM((2, tile + SUBLANES, CONV_COLS), F32),
                        pltpu.VMEM((SUBLANES, nx), F32)],
        compiler_params=_params("arbitrary"),
        name="ssm_inproj",
    )(x, g, w, cw, cb)


def _ssd_kernel(z_ref, xs_ref, bc_ref, dtr_ref, dtb_ref, alog_ref, dexp_ref, gn_ref,
                expand_ref, tril_ref, y_ref, st_ref, *, d_inner, n_heads):
    L = SSD_CHUNK
    gw = d_inner // SSM_GROUPS
    hpg = n_heads // SSM_GROUPS
    gn_state = SSM_GROUPS * SSM_STATE
    seqs = range(z_ref.shape[0])

    @pl.when(pl.program_id(1) == 0)
    def _():
        st_ref[...] = jnp.zeros(st_ref.shape, F32)

    tril = tril_ref[...]
    causal = tril > 0.5

    def per_channel(v):
        hi = v.astype(BF16)
        r1 = v - hi.astype(F32)
        mid = r1.astype(BF16)
        lo = (r1 - mid.astype(F32)).astype(BF16)
        return _dot(jnp.concatenate([hi, mid, lo], axis=1), expand_ref[...])

    a_cs, a_cs_t, a_cs_x, dt_x = [], [], [], []
    for q in seqs:
        dt_in = dtr_ref[q] + dtb_ref[...]
        dt = jnp.maximum(dt_in, 0.0) + jnp.log1p(jnp.exp(-jnp.abs(dt_in)))
        a = dt * (-jnp.exp(alog_ref[...]))
        acs = _dot(tril, a, precision=_HI)
        a_cs.append(acs)
        a_cs_t.append(acs.T)
        a_cs_x.append(per_channel(acs))
        dt_x.append(per_channel(dt))

    for g in range(SSM_GROUPS):
        lo = g * gw
        for q in seqs:
            xs = xs_ref[q, :, lo:lo + gw]
            bm = bc_ref[q, :, g * SSM_STATE:(g + 1) * SSM_STATE]
            cm = bc_ref[q, :, gn_state + g * SSM_STATE:gn_state + (g + 1) * SSM_STATE]
            xdt = xs * dt_x[q][:, lo:lo + gw]
            cb = _dot(cm, bm, _NT)
            yd = []
            for r in range(hpg):
                hh = g * hpg + r
                seg = a_cs[q][:, hh:hh + 1] - a_cs_t[q][hh:hh + 1, :]
                lmat = jnp.exp(jnp.where(causal, seg, -jnp.inf))
                m = (cb * lmat).astype(BF16)
                yd.append(_dot(m, xdt[:, r * SSM_HEADDIM:(r + 1) * SSM_HEADDIM].astype(BF16)))
            y = jnp.concatenate(yd, axis=1)
            acx = a_cs_x[q][:, lo:lo + gw]
            alx = a_cs_x[q][L - 1:L, lo:lo + gw]
            prev = st_ref[q, g]
            y = y + _dot(cm, prev.astype(BF16)) * jnp.exp(acx)
            xd = (xdt * jnp.exp(alx - acx)).astype(BF16)
            st_ref[q, g] = prev * jnp.exp(alx) + _dot(bm, xd, _TN)
            y = y + xs * dexp_ref[:, lo:lo + gw]
            zg = z_ref[q, :, lo:lo + gw]
            y = y * (zg * jax.nn.sigmoid(zg))
            y = y * lax.rsqrt(jnp.mean(y * y, axis=-1, keepdims=True) + GATED_NORM_EPS)
            y_ref[q, :, lo:lo + gw] = (y * gn_ref[:, lo:lo + gw]).astype(y_ref.dtype)


def _ssd(z, xs, bc, dtr, dtb, alog, dexp, gn, expand, tril, batch):
    T, d_inner = z.shape
    H = dtr.shape[1]
    L = SSD_CHUNK
    seq = T // batch
    assert batch % SEQ_PER_STEP == 0 and seq % L == 0
    blk = lambda w: pl.BlockSpec((SEQ_PER_STEP, L, w), lambda b, c: (b, c, 0))
    per_seq = lambda v: v.reshape(batch, seq, v.shape[1])
    kern = functools.partial(_ssd_kernel, d_inner=d_inner, n_heads=H)
    y = pl.pallas_call(
        kern,
        grid=(batch // SEQ_PER_STEP, seq // L),
        in_specs=[blk(d_inner), blk(d_inner), blk(bc.shape[1]), blk(H),
                  _const_spec(dtb.shape), _const_spec(alog.shape), _const_spec(dexp.shape),
                  _const_spec(gn.shape), _const_spec(expand.shape), _const_spec(tril.shape)],
        out_specs=blk(d_inner),
        out_shape=jax.ShapeDtypeStruct((batch, seq, d_inner), BF16),
        scratch_shapes=[pltpu.VMEM((SEQ_PER_STEP, SSM_GROUPS, SSM_STATE, d_inner // SSM_GROUPS), F32)],
        compiler_params=_params("arbitrary", "arbitrary"),
        name="ssd_scan",
    )(per_seq(z), per_seq(xs), per_seq(bc), per_seq(dtr), dtb, alog, dexp, gn, expand, tril)
    return y.reshape(T, d_inner)


def _outproj_kernel(x_ref, y_ref, w_ref, o_ref):
    o_ref[...] = x_ref[...] + _dot(y_ref[...], w_ref[...])


def _outproj(x, y, w, tile):
    T, D = x.shape
    K = y.shape[1]
    return pl.pallas_call(
        _outproj_kernel,
        grid=(T // tile,),
        in_specs=[pl.BlockSpec((tile, D), lambda i: (i, 0)),
                  pl.BlockSpec((tile, K), lambda i: (i, 0)), _const_spec(w.shape)],
        out_specs=pl.BlockSpec((tile, D), lambda i: (i, 0)),
        out_shape=jax.ShapeDtypeStruct((T, D), F32),
        compiler_params=_params("arbitrary"),
        name="ssm_outproj",
    )(x, y, w)


def _peer_kernel(x_ref, g_ref, wq_ref, keys_ref, u0_ref, un_ref, vt_ref, o_ref,
                 hb_ref, e1_ref, e2_ref, gmin_ref, top_ref,
                 a0_ref, a1_ref, p0_ref, p1_ref, acca_ref, accb_ref):
    g = pl.program_id(1)
    last = pl.num_programs(1) - 1
    tt = x_ref.shape[0]
    ec = un_ref.shape[0] // 2
    rows_per_chunk = ec // N_KEYS
    mxu_cols = 2 * LANES
    n_piece = tt // mxu_cols
    half = N_KEYS
    ntop = PEER_TOPK + 1
    neg_inf = -jnp.inf

    @pl.when(g == 0)
    def _route():
        hb = _rms(x_ref[...], g_ref[...], NORM_EPS).T.astype(BF16)
        hb_ref[...] = hb
        for h in range(PEER_HEADS):
            q = _dot(wq_ref[h * 2 * half:(h + 1) * 2 * half, :], hb).astype(BF16)
            a0_ref[h * N_KEYS:(h + 1) * N_KEYS, :] = _dot(keys_ref[0], q[0:half, :])
            a1_ref[h * N_KEYS:(h + 1) * N_KEYS, :] = _dot(keys_ref[1], q[half:2 * half, :])

        def top_values(s_ref, h, cs):
            v = [s_ref[h * N_KEYS + r * SUBLANES:h * N_KEYS + (r + 1) * SUBLANES, cs]
                 for r in range(N_KEYS // SUBLANES)]
            for i, j in _SORT_PAIRS:
                v[i], v[j] = jnp.maximum(v[i], v[j]), jnp.minimum(v[i], v[j])
            depth = len(v)
            out = []
            for k in range(ntop):
                m = v[0]
                for shift in (4, 2, 1):
                    m = jnp.maximum(m, pltpu.roll(m, shift, 0))
                out.append(m)
                hit = v[0] == m
                for r in range(min(depth, ntop - 1 - k)):
                    v[r] = jnp.where(hit, v[r + 1] if r + 1 < depth else neg_inf, v[r])
            return out

        for tc in range(tt // LANES):
            cs = slice(tc * LANES, (tc + 1) * LANES)
            for c, s_ref in enumerate((a0_ref, a1_ref)):
                for h in range(PEER_HEADS):
                    for r, m in enumerate(top_values(s_ref, h, cs)):
                        top_ref[c, r, h:h + 1, cs] = m[0:1, :]

            a = [top_ref[0, r, :, cs] for r in range(ntop)]
            b = [top_ref[1, r, :, cs] for r in range(ntop)]
            lists = [[a[r1] + b[r2] for r2 in range(ntop // (r1 + 1))] for r1 in range(ntop)]
            a0, b0 = a[0], b[0]
            m0 = a0 + b0
            z = jnp.zeros(m0.shape, F32)
            for k in range(ntop):
                m = functools.reduce(jnp.maximum, [lst[0] for lst in lists])
                if k < PEER_TOPK:
                    z = z + jnp.exp(m - m0)
                if k == PEER_TOPK - 1:
                    v16 = m
                if k < ntop - 1:
                    for lst in lists:
                        hit = lst[0] == m
                        for j in range(len(lst)):
                            lst[j] = jnp.where(hit, lst[j + 1] if j + 1 < len(lst) else neg_inf, lst[j])
            v17 = m
            zinv = np.float32(math.sqrt(0.5)) / z
            gmin = jnp.exp(0.5 * (v16 + v17) - m0) * zinv
            for h in range(PEER_HEADS):
                hk = slice(h * N_KEYS, (h + 1) * N_KEYS)
                e1_ref[h, tc] = jnp.exp(a0_ref[hk, cs] - a0[h:h + 1, :])
                e2_ref[h, tc] = jnp.exp(a1_ref[hk, cs] - b0[h:h + 1, :]) * zinv[h:h + 1, :]
                gmin_ref[h, tc] = gmin[h:h + 1, :]

            if (tc + 1) % (mxu_cols // LANES) == 0:
                ps = slice((tc + 1) * LANES - mxu_cols, (tc + 1) * LANES)
                a0_ref[:, ps] = _dot(u0_ref[0:ec, :], hb_ref[:, ps])
                a1_ref[:, ps] = _dot(u0_ref[ec:2 * ec, :], hb_ref[:, ps])

        acca_ref[...] = jnp.zeros(acca_ref.shape, F32)
        accb_ref[...] = jnp.zeros(accb_ref.shape, F32)

    def gate_gelu(a_ref, p_ref, chunk, tc):
        i0 = pl.multiple_of(chunk * rows_per_chunk, SUBLANES)
        cs = slice(tc * LANES, (tc + 1) * LANES)
        e18 = [e1_ref[h, tc, pl.ds(i0, rows_per_chunk), :] for h in range(PEER_HEADS)]
        gm = [gmin_ref[h, tc] for h in range(PEER_HEADS)]
        for ii in range(rows_per_chunk):
            rows = slice(ii * N_KEYS, (ii + 1) * N_KEYS)
            w = jnp.zeros((N_KEYS, LANES), F32)
            for h in range(PEER_HEADS):
                gate = e18[h][ii:ii + 1, :] * e2_ref[h, tc]
                w = w + jnp.where(gate >= gm[h], gate, 0.0)
            av = a_ref[rows, cs]
            p_ref[rows, cs] = ((w * av) * (1.0 + lax.erf(av))).astype(BF16)

    for par, (a_ref, p_ref, acc_ref) in enumerate(((a0_ref, p0_ref, acca_ref),
                                                   (a1_ref, p1_ref, accb_ref))):
        es = slice(par * ec, (par + 1) * ec)
        for piece in range(n_piece):
            cs = slice(piece * mxu_cols, (piece + 1) * mxu_cols)
            for tc in range(piece * (mxu_cols // LANES), (piece + 1) * (mxu_cols // LANES)):
                gate_gelu(a_ref, p_ref, 2 * g + par, tc)
            acc_ref[:, cs] += _dot(vt_ref[:, es], p_ref[:, cs])
            a_ref[:, cs] = _dot(un_ref[es, :], hb_ref[:, cs])

    @pl.when(g == last)
    def _finish():
        o_ref[...] = x_ref[...] + (acca_ref[...] + accb_ref[...]).T


def _peer(x, g, wq_t, keys, u, vt, layer, tile, ec):
    T, D = x.shape
    E = u.shape[1]
    nchunk = E // ec
    assert E == N_KEYS * N_KEYS and ec == SUBLANES * N_KEYS and tile % LANES == 0
    assert ec == PEER_HEADS * N_KEYS
    assert nchunk % 2 == 0
    ntc = tile // LANES
    nstep = nchunk // 2
    return pl.pallas_call(
        _peer_kernel,
        grid=(T // tile, nstep),
        in_specs=[pl.BlockSpec((tile, D), lambda i, s: (i, 0)),
                  _const_spec((1, D)),
                  pl.BlockSpec((None,) + wq_t.shape[1:], lambda i, s: (layer, 0, 0),
                               pipeline_mode=pl.Buffered(1)),
                  pl.BlockSpec((None,) + keys.shape[1:], lambda i, s: (layer, 0, 0, 0),
                               pipeline_mode=pl.Buffered(1)),
                  pl.BlockSpec((None, 2 * ec, D), lambda i, s: (layer, 0, 0),
                               pipeline_mode=pl.Buffered(1)),
                  pl.BlockSpec((None, 2 * ec, D),
                               lambda i, s: (layer, jnp.minimum(s + 1, nstep - 1), 0)),
                  pl.BlockSpec((None, D, 2 * ec), lambda i, s: (layer, 0, s))],
        out_specs=pl.BlockSpec((tile, D), lambda i, s: (i, 0)),
        out_shape=jax.ShapeDtypeStruct((T, D), F32),
        scratch_shapes=[pltpu.VMEM((D, tile), BF16),
                        pltpu.VMEM((PEER_HEADS, ntc, N_KEYS, LANES), F32),
                        pltpu.VMEM((PEER_HEADS, ntc, N_KEYS, LANES), F32),
                        pltpu.VMEM((PEER_HEADS, ntc, 1, LANES), F32),
                        pltpu.VMEM((2, PEER_TOPK + 1, PEER_HEADS, tile), F32),
                        pltpu.VMEM((ec, tile), F32), pltpu.VMEM((ec, tile), F32),
                        pltpu.VMEM((ec, tile), BF16), pltpu.VMEM((ec, tile), BF16),
                        pltpu.VMEM((D, tile), F32), pltpu.VMEM((D, tile), F32)],
        compiler_params=_params("arbitrary", "arbitrary"),
        name="peer",
    )(x, g, wq_t, keys, u, u, vt)


def _rope_rows(tt, pos_ref, invc_ref):
    half = ROT_DIM // 2
    ang = invc_ref[...] * pos_ref[...].astype(F32)
    cos, sin = jnp.cos(ang), jnp.sin(ang)
    rows = []
    for base in range(0, tt.shape[0], HEAD_DIM):
        t1 = tt[base:base + half, :]
        t2 = tt[base + half:base + ROT_DIM, :]
        rows += [t1 * cos - t2 * sin, t2 * cos + t1 * sin, tt[base + ROT_DIM:base + HEAD_DIM, :]]
    return rows


def _ple_core(x_ref, p_ref, g_ref, proj_ref, gw_ref):
    x = x_ref[...]
    hn = _rms(x, g_ref[...], NORM_EPS).astype(BF16)
    gate = jax.nn.sigmoid(_dot(hn, gw_ref[...]))
    return x + _dot(p_ref[...].astype(BF16), proj_ref[...]) * gate


def _ple_kv_kernel(x_ref, p_ref, g_ref, proj_ref, gw_ref, kvg_ref, kvw_ref, kvb_ref, pos_ref,
                   invc_ref, o_ref, k_ref, v_ref):
    x2 = _ple_core(x_ref, p_ref, g_ref, proj_ref, gw_ref)
    o_ref[...] = x2
    kv = _dot(_rms(x2, kvg_ref[...], NORM_EPS).astype(BF16), kvw_ref[...]) + kvb_ref[...]
    kvd = k_ref.shape[1]
    kt = jnp.concatenate(_rope_rows(kv[:, :kvd].T, pos_ref, invc_ref), axis=0)
    k_ref[...] = kt.T.astype(k_ref.dtype)
    v_ref[...] = kv[:, kvd:].astype(v_ref.dtype)


def _ple_final_kernel(x_ref, p_ref, g_ref, proj_ref, gw_ref, fg_ref, o_ref):
    x2 = _ple_core(x_ref, p_ref, g_ref, proj_ref, gw_ref)
    o_ref[...] = _rms(x2, fg_ref[...], NORM_EPS)


def _ple_kv(x, p, g, proj, gw, kvg, kvw, kvb, pos, inv_col, tile):
    T, D = x.shape
    P = p.shape[1]
    kvd = kvw.shape[1] // 2
    tok = lambda w: pl.BlockSpec((tile, w), lambda i: (i, 0))
    return pl.pallas_call(
        _ple_kv_kernel,
        grid=(T // tile,),
        in_specs=[tok(D), tok(P), _const_spec((1, D)), _const_spec(proj.shape),
                  _const_spec(gw.shape), _const_spec((1, D)), _const_spec(kvw.shape),
                  _const_spec(kvb.shape),
                  pl.BlockSpec((None, 1, tile), lambda i: (i, 0, 0)), _const_spec(inv_col.shape)],
        out_specs=[tok(D), tok(kvd), tok(kvd)],
        out_shape=[jax.ShapeDtypeStruct((T, D), F32), jax.ShapeDtypeStruct((T, kvd), BF16),
                   jax.ShapeDtypeStruct((T, kvd), BF16)],
        compiler_params=_params("arbitrary"),
        name="ple_kv",
    )(x, p, g, proj, gw, kvg, kvw, kvb, pos.reshape(T // tile, 1, tile), inv_col)


def _ple_final(x, p, g, proj, gw, fg, tile):
    T, D = x.shape
    P = p.shape[1]
    tok = lambda w: pl.BlockSpec((tile, w), lambda i: (i, 0))
    return pl.pallas_call(
        _ple_final_kernel,
        grid=(T // tile,),
        in_specs=[tok(D), tok(P), _const_spec((1, D)), _const_spec(proj.shape),
                  _const_spec(gw.shape), _const_spec((1, D))],
        out_specs=tok(D),
        out_shape=jax.ShapeDtypeStruct((T, D), F32),
        compiler_params=_params("arbitrary"),
        name="ple_final",
    )(x, p, g, proj, gw, fg)


def _attn_kernel(x_ref, g_ref, qw_ref, qb_ref, sink_ref, owt_ref, ob_ref, kc_ref, kp_ref, vc_ref,
                 vp_ref, pos_ref, invc_ref, o_ref, qt_ref, att_ref, *, n_q_heads):
    tile = x_ref.shape[0]
    W = WINDOW
    nblk = tile // W
    q_per_kv = n_q_heads // N_KV_HEADS
    half = ROT_DIM // 2
    scale = HEAD_DIM ** -0.5
    first = pl.program_id(1) == 0

    h = _rms(x_ref[...], g_ref[...], NORM_EPS).astype(BF16)
    q = (_dot(h, qw_ref[...]) + qb_ref[...]) * scale
    qt = q.T
    ang = invc_ref[...] * pos_ref[...].astype(F32)
    cos, sin = jnp.cos(ang), jnp.sin(ang)
    for hh in range(n_q_heads):
        base = hh * HEAD_DIM
        t1 = qt[base:base + half, :]
        t2 = qt[base + half:base + ROT_DIM, :]
        qt_ref[base:base + half, :] = (t1 * cos - t2 * sin).astype(BF16)
        qt_ref[base + half:base + ROT_DIM, :] = (t2 * cos + t1 * sin).astype(BF16)
        qt_ref[base + ROT_DIM:base + HEAD_DIM, :] = qt[base + ROT_DIM:base + HEAD_DIM, :].astype(BF16)

    kj = lax.broadcasted_iota(jnp.int32, (2 * W, W), 0)
    qi = lax.broadcasted_iota(jnp.int32, (2 * W, W), 1)
    first_lo = jnp.where(first, W, 0)
    band = jnp.where((kj > qi) & (kj <= qi + W), 0.0, -jnp.inf)
    band_first = jnp.where(kj >= first_lo, band, -jnp.inf)
    band = jnp.concatenate([band] * q_per_kv, axis=1)
    band_first = jnp.concatenate([band_first] * q_per_kv, axis=1)

    for n in range(nblk):
        cols = slice(n * W, (n + 1) * W)
        if n == 0:
            kprev, vprev = kp_ref[...], vp_ref[...]
        else:
            kprev, vprev = kc_ref[(n - 1) * W:n * W, :], vc_ref[(n - 1) * W:n * W, :]
        kblk = jnp.concatenate([kprev, kc_ref[cols, :]], axis=0)
        vblk = jnp.concatenate([vprev, vc_ref[cols, :]], axis=0)
        bias = band_first if n == 0 else band
        for g in range(N_KV_HEADS):
            ks = kblk[:, g * HEAD_DIM:(g + 1) * HEAD_DIM]
            vs = vblk[:, g * HEAD_DIM:(g + 1) * HEAD_DIM]
            heads = range(g * q_per_kv, (g + 1) * q_per_kv)
            qs = jnp.concatenate([qt_ref[hh * HEAD_DIM:(hh + 1) * HEAD_DIM, cols] for hh in heads],
                                 axis=1)
            sink = jnp.concatenate([jnp.broadcast_to(sink_ref[:, hh:hh + 1], (1, W)) for hh in heads],
                                   axis=1)
            s = _dot(ks, qs) + bias
            m = jnp.maximum(jnp.max(s, axis=0, keepdims=True), sink)
            e = jnp.exp(s - m)
            denom = jnp.sum(e, axis=0, keepdims=True) + jnp.exp(sink - m)
            o = _dot(vs, e.astype(BF16), _TN) / denom
            for r, hh in enumerate(heads):
                att_ref[hh * HEAD_DIM:(hh + 1) * HEAD_DIM, cols] = o[:, r * W:(r + 1) * W].astype(BF16)

    o_ref[...] = x_ref[...] + _dot(owt_ref[...], att_ref[...]).T + ob_ref[...]


def _attn(x, g, qw, qb, sinks, ow_t, ob, k, v, pos, inv_col, batch, tile):
    T, D = x.shape
    kvd = k.shape[1]
    nq = qw.shape[1] // HEAD_DIM
    nt = T // batch // tile
    bpt = tile // WINDOW
    row = lambda b, i: (b * nt + i, 0)
    prev = lambda b, i: (jnp.maximum((b * nt + i) * bpt - 1, 0), 0)
    kern = functools.partial(_attn_kernel, n_q_heads=nq)
    return pl.pallas_call(
        kern,
        grid=(batch, nt),
        in_specs=[pl.BlockSpec((tile, D), row), _const_spec((1, D)), _const_spec(qw.shape),
                  _const_spec(qb.shape), _const_spec(sinks.shape), _const_spec(ow_t.shape),
                  _const_spec(ob.shape),
                  pl.BlockSpec((tile, kvd), row), pl.BlockSpec((WINDOW, kvd), prev),
                  pl.BlockSpec((tile, kvd), row), pl.BlockSpec((WINDOW, kvd), prev),
                  pl.BlockSpec((None, 1, tile), lambda b, i: (b * nt + i, 0, 0)),
                  _const_spec(inv_col.shape)],
        out_specs=pl.BlockSpec((tile, D), row),
        out_shape=jax.ShapeDtypeStruct((T, D), F32),
        scratch_shapes=[pltpu.VMEM((qw.shape[1], tile), BF16),
                        pltpu.VMEM((qw.shape[1], tile), BF16)],
        compiler_params=_params("arbitrary", "arbitrary"),
        name="swa_attn",
    )(x, g, qw, qb, sinks, ow_t, ob, k, k, v, v, pos.reshape(T // tile, 1, tile), inv_col)


def _row(v):
    return v.reshape(1, -1)


def kernel(x, p, positions, ssm_norm, ssm_in_w, ssm_conv_w, ssm_conv_b, ssm_dt_bias, ssm_A_log, ssm_D, ssm_gate_norm, ssm_out_w, kv_norm, kv_w, kv_b, attn_norm, q_w, q_b, sinks, o_w, o_b, peer_norm, peer_q_w, peer_sub_keys, peer_u, peer_v, ple_norm, ple_proj, ple_gate_w, final_norm):
    B, S, D = x.shape
    T = B * S
    depth = p.shape[0]
    n_a = ssm_norm.shape[0]
    H = ssm_D.shape[1]
    d_inner = H * SSM_HEADDIM
    conv_dim = ssm_conv_w.shape[2]

    xt = x.reshape(T, D)
    pos = positions.reshape(T, 1)
    inv_col = jnp.asarray(
        ROPE_THETA ** (-np.arange(0, ROT_DIM, 2, dtype=np.float64) / ROT_DIM), F32).reshape(-1, 1)
    expand = jnp.asarray(np.tile(np.repeat(np.eye(H, dtype=np.float32), SSM_HEADDIM, axis=1),
                                 (3, 1)), BF16)
    tril = jnp.asarray(np.tril(np.ones((SSD_CHUNK, SSD_CHUNK), np.float32)))

    peer_wq_t = peer_q_w.astype(BF16).transpose(0, 2, 1)
    peer_keys = peer_sub_keys.astype(BF16)
    peer_u_b = (peer_u * np.float32(math.sqrt(0.5))).astype(BF16)
    peer_vt = peer_v.astype(BF16).transpose(0, 2, 1)

    k_sh = v_sh = None
    for i in range(depth):
        if i < n_a:
            z, xs, bc, dtr = _inproj(xt, _row(ssm_norm[i]), ssm_in_w[i].astype(BF16), ssm_conv_w[i],
                                     _row(ssm_conv_b[i]), d_inner=d_inner, tile=INPROJ_TILE, seq=S)
            y = _ssd(z, xs, bc, dtr, _row(ssm_dt_bias[i]), _row(ssm_A_log[i]),
                     _row(jnp.repeat(ssm_D[i], SSM_HEADDIM)), _row(ssm_gate_norm[i]), expand, tril,
                     batch=B)
            xt = _outproj(xt, y, ssm_out_w[i].astype(BF16), tile=OUTPROJ_TILE)
        else:
            j = i - n_a
            xt = _attn(xt, _row(attn_norm[j]), q_w[j].astype(BF16), _row(q_b[j]), _row(sinks[j]),
                       o_w[j].T.astype(BF16), _row(o_b[j]), k_sh, v_sh, pos, inv_col, batch=B,
                       tile=ATTN_TILE)
        xt = _peer(xt, _row(peer_norm[i]), peer_wq_t, peer_keys, peer_u_b, peer_vt, layer=i,
                   tile=PEER_TILE, ec=PEER_CHUNK)
        if i == n_a - 1:
            xt, k_sh, v_sh = _ple_kv(xt, p[i].reshape(T, -1), _row(ple_norm[i]),
                                     ple_proj[i].astype(BF16), ple_gate_w[i].astype(BF16),
                                     _row(kv_norm), kv_w.astype(BF16), _row(kv_b), pos, inv_col,
                                     tile=PLE_TILE)
        elif i == depth - 1:
            xt = _ple_final(xt, p[i].reshape(T, -1), _row(ple_norm[i]), ple_proj[i].astype(BF16),
                            ple_gate_w[i].astype(BF16), _row(final_norm), tile=PLE_TILE)
        else:
            raise NotImplementedError("PLE without K/V or final norm")
    return xt.reshape(B, S, D)
```

```python
import functools
import math

import jax
import jax.numpy as jnp
import numpy as np
from jax import lax
from jax.experimental import pallas as pl
from jax.experimental.pallas import tpu as pltpu

F32 = jnp.float32
BF16 = jnp.bfloat16

NORM_EPS = 1e-6
GATED_NORM_EPS = 1e-5
SSM_HEADDIM = 64
SSM_GROUPS = 8
SSM_STATE = 128
CONV_K = 4
SSD_CHUNK = 128
SEQ_PER_STEP = 2
CONV_COLS = 256
HEAD_DIM = 64
N_KV_HEADS = 2
WINDOW = 128
ROT_DIM = HEAD_DIM // 4
ROPE_THETA = 500000.0
PEER_HEADS = 8
N_KEYS = 128
PEER_TOPK = 16

LANES = 128
SUBLANES = 8
VMEM_LIMIT = 56 * 1024 * 1024

INPROJ_TILE = 256
OUTPROJ_TILE = 1024
ATTN_TILE = 512
PLE_TILE = 1024
PEER_TILE = 512
PEER_CHUNK = SUBLANES * N_KEYS


def _sort_network(n):
    pairs = []
    p = 1
    while p < n:
        k = p
        while k >= 1:
            for j in range(k % p, n - k, 2 * k):
                for i in range(min(k, n - j - k)):
                    if (i + j) // (2 * p) == (i + j + k) // (2 * p):
                        pairs.append((i + j, i + j + k))
            k //= 2
        p *= 2
    return pairs


_SORT_PAIRS = _sort_network(N_KEYS // SUBLANES)


def _params(*sem):
    return pltpu.CompilerParams(dimension_semantics=sem, vmem_limit_bytes=VMEM_LIMIT)


def _const_spec(shape):
    nd = len(shape)
    return pl.BlockSpec(shape, lambda *_: (0,) * nd, pipeline_mode=pl.Buffered(1))


def _rms(x, g, eps):
    return x * lax.rsqrt(jnp.mean(x * x, axis=-1, keepdims=True) + eps) * g


def _dot(a, b, dims=None, precision=None):
    if dims is None:
        dims = (((a.ndim - 1,), (0,)), ((), ()))
    return lax.dot_general(a, b, dims, precision=precision, preferred_element_type=F32)


_NT = (((1,), (1,)), ((), ()))
_TN = (((0,), (0,)), ((), ()))
_HI = lax.Precision.HIGHEST


def _inproj_kernel(x_ref, g_ref, w_ref, cw_ref, cb_ref, z_ref, xs_ref, bc_ref,
                   dt_ref, xb_ref, tail_ref, *, tiles_per_seq):
    tile = x_ref.shape[0]
    d_inner = xs_ref.shape[1]
    conv_dim = tail_ref.shape[1]
    tail = SUBLANES

    @pl.when(pl.program_id(0) % tiles_per_seq == 0)
    def _():
        tail_ref[...] = jnp.zeros(tail_ref.shape, F32)

    h = _rms(x_ref[...], g_ref[...], NORM_EPS).astype(BF16)
    dt_ref[...] = _dot(h, w_ref[:, d_inner + conv_dim:])

    for gi, lo in enumerate(range(0, conv_dim, CONV_COLS)):
        cs = slice(lo, lo + CONV_COLS)
        stage = xb_ref.at[gi % 2]
        stage[0:tail, :] = tail_ref[:, cs]
        stage[tail:tail + tile, :] = _dot(h, w_ref[:, d_inner + lo:d_inner + lo + CONV_COLS])
        if lo < d_inner:
            z_ref[:, cs] = _dot(h, w_ref[:, cs])
        u = stage[...]
        tail_ref[:, cs] = u[tile:tile + tail, :]
        acc = cb_ref[:, cs] + u[tail:, :] * cw_ref[CONV_K - 1:CONV_K, cs]
        for back in range(1, CONV_K):
            acc = acc + pltpu.roll(u, back, 0)[tail:, :] * cw_ref[CONV_K - 1 - back:CONV_K - back, cs]
        act = acc * jax.nn.sigmoid(acc)
        if lo < d_inner:
            xs_ref[:, cs] = act
        else:
            bc_ref[:, lo - d_inner:lo - d_inner + CONV_COLS] = act.astype(BF16)


def _inproj(x, g, w, cw, cb, d_inner, tile, seq):
    T, D = x.shape
    nz, nx = d_inner, cw.shape[1]
    nd = w.shape[1] - nz - nx
    assert nz % CONV_COLS == 0 and nx % CONV_COLS == 0 and seq % tile == 0
    tok = lambda w: pl.BlockSpec((tile, w), lambda i: (i, 0))
    kern = functools.partial(_inproj_kernel, tiles_per_seq=seq // tile)
    return pl.pallas_call(
        kern,
        grid=(T // tile,),
        in_specs=[tok(D), _const_spec((1, D)), _const_spec(w.shape), _const_spec(cw.shape),
                  _const_spec(cb.shape)],
        out_specs=[tok(nz), tok(nz), tok(nx - nz), tok(nd)],
        out_shape=[jax.ShapeDtypeStruct((T, nz), F32), jax.ShapeDtypeStruct((T, nz), F32),
                   jax.ShapeDtypeStruct((T, nx - nz), BF16), jax.ShapeDtypeStruct((T, nd), F32)],
        scratch_shapes=[pltpu.VMEM((2, tile + SUBLANES, CONV_COLS), F32),
                        pltpu.VMEM((SUBLANES, nx), F32)],
        compiler_params=_params("arbitrary"),
        name="ssm_inproj",
    )(x, g, w, cw, cb)


def _ssd_kernel(z_ref, xs_ref, bc_ref, dtr_ref, dtb_ref, alog_ref, dexp_ref, gn_ref,
                expand_ref, tril_ref, y_ref, st_ref, *, d_inner, n_heads):
    L = SSD_CHUNK
    gw = d_inner // SSM_GROUPS
    hpg = n_heads // SSM_GROUPS
    gn_state = SSM_GROUPS * SSM_STATE
    seqs = range(z_ref.shape[0])

    @pl.when(pl.program_id(1) == 0)
    def _():
        st_ref[...] = jnp.zeros(st_ref.shape, F32)

    tril = tril_ref[...]
    causal = tril > 0.5

    def per_channel(v):
        hi = v.astype(BF16)
        r1 = v - hi.astype(F32)
        mid = r1.astype(BF16)
        lo = (r1 - mid.astype(F32)).astype(BF16)
        return _dot(jnp.concatenate([hi, mid, lo], axis=1), expand_ref[...])

    a_cs, a_cs_t, a_cs_x, dt_x = [], [], [], []
    for q in seqs:
        dt_in = dtr_ref[q] + dtb_ref[...]
        dt = jnp.maximum(dt_in, 0.0) + jnp.log1p(jnp.exp(-jnp.abs(dt_in)))
        a = dt * (-jnp.exp(alog_ref[...]))
        acs = _dot(tril, a, precision=_HI)
        a_cs.append(acs)
        a_cs_t.append(acs.T)
        a_cs_x.append(per_channel(acs))
        dt_x.append(per_channel(dt))

    for g in range(SSM_GROUPS):
        lo = g * gw
        for q in seqs:
            xs = xs_ref[q, :, lo:lo + gw]
            bm = bc_ref[q, :, g * SSM_STATE:(g + 1) * SSM_STATE]
            cm = bc_ref[q, :, gn_state + g * SSM_STATE:gn_state + (g + 1) * SSM_STATE]
            xdt = xs * dt_x[q][:, lo:lo + gw]
            cb = _dot(cm, bm, _NT)
            yd = []
            for r in range(hpg):
                hh = g * hpg + r
                seg = a_cs[q][:, hh:hh + 1] - a_cs_t[q][hh:hh + 1, :]
                lmat = jnp.exp(jnp.where(causal, seg, -jnp.inf))
                m = (cb * lmat).astype(BF16)
                yd.append(_dot(m, xdt[:, r * SSM_HEADDIM:(r + 1) * SSM_HEADDIM].astype(BF16)))
            y = jnp.concatenate(yd, axis=1)
            acx = a_cs_x[q][:, lo:lo + gw]
            alx = a_cs_x[q][L - 1:L, lo:lo + gw]
            prev = st_ref[q, g]
            y = y + _dot(cm, prev.astype(BF16)) * jnp.exp(acx)
            xd = (xdt * jnp.exp(alx - acx)).astype(BF16)
            st_ref[q, g] = prev * jnp.exp(alx) + _dot(bm, xd, _TN)
            y = y + xs * dexp_ref[:, lo:lo + gw]
            zg = z_ref[q, :, lo:lo + gw]
            y = y * (zg * jax.nn.sigmoid(zg))
            y = y * lax.rsqrt(jnp.mean(y * y, axis=-1, keepdims=True) + GATED_NORM_EPS)
            y_ref[q, :, lo:lo + gw] = (y * gn_ref[:, lo:lo + gw]).astype(y_ref.dtype)


def _ssd(z, xs, bc, dtr, dtb, alog, dexp, gn, expand, tril, batch):
    T, d_inner = z.shape
    H = dtr.shape[1]
    L = SSD_CHUNK
    seq = T // batch
    assert batch % SEQ_PER_STEP == 0 and seq % L == 0
    blk = lambda w: pl.BlockSpec((SEQ_PER_STEP, L, w), lambda b, c: (b, c, 0))
    per_seq = lambda v: v.reshape(batch, seq, v.shape[1])
    kern = functools.partial(_ssd_kernel, d_inner=d_inner, n_heads=H)
    y = pl.pallas_call(
        kern,
        grid=(batch // SEQ_PER_STEP, seq // L),
        in_specs=[blk(d_inner), blk(d_inner), blk(bc.shape[1]), blk(H),
                  _const_spec(dtb.shape), _const_spec(alog.shape), _const_spec(dexp.shape),
                  _const_spec(gn.shape), _const_spec(expand.shape), _const_spec(tril.shape)],
        out_specs=blk(d_inner),
        out_shape=jax.ShapeDtypeStruct((batch, seq, d_inner), BF16),
        scratch_shapes=[pltpu.VMEM((SEQ_PER_STEP, SSM_GROUPS, SSM_STATE, d_inner // SSM_GROUPS), F32)],
        compiler_params=_params("arbitrary", "arbitrary"),
        name="ssd_scan",
    )(per_seq(z), per_seq(xs), per_seq(bc), per_seq(dtr), dtb, alog, dexp, gn, expand, tril)
    return y.reshape(T, d_inner)


def _outproj_kernel(x_ref, y_ref, w_ref, o_ref):
    o_ref[...] = x_ref[...] + _dot(y_ref[...], w_ref[...])


def _outproj(x, y, w, tile):
    T, D = x.shape
    K = y.shape[1]
    return pl.pallas_call(
        _outproj_kernel,
        grid=(T // tile,),
        in_specs=[pl.BlockSpec((tile, D), lambda i: (i, 0)),
                  pl.BlockSpec((tile, K), lambda i: (i, 0)), _const_spec(w.shape)],
        out_specs=pl.BlockSpec((tile, D), lambda i: (i, 0)),
        out_shape=jax.ShapeDtypeStruct((T, D), F32),
        compiler_params=_params("arbitrary"),
        name="ssm_outproj",
    )(x, y, w)


def _peer_kernel(x_ref, g_ref, wq_ref, keys_ref, u0_ref, un_ref, vt_ref, o_ref,
                 hb_ref, e1_ref, e2_ref, gmin_ref, top_ref,
                 a0_ref, a1_ref, p0_ref, p1_ref, acca_ref, accb_ref):
    g = pl.program_id(1)
    last = pl.num_programs(1) - 1
    tt = x_ref.shape[0]
    ec = un_ref.shape[0] // 2
    rows_per_chunk = ec // N_KEYS
    mxu_cols = 2 * LANES
    n_piece = tt // mxu_cols
    half = N_KEYS
    ntop = PEER_TOPK + 1
    neg_inf = -jnp.inf

    @pl.when(g == 0)
    def _route():
        hb = _rms(x_ref[...], g_ref[...], NORM_EPS).T.astype(BF16)
        hb_ref[...] = hb
        for h in range(PEER_HEADS):
            q = _dot(wq_ref[h * 2 * half:(h + 1) * 2 * half, :], hb).astype(BF16)
            a0_ref[h * N_KEYS:(h + 1) * N_KEYS, :] = _dot(keys_ref[0], q[0:half, :])
            a1_ref[h * N_KEYS:(h + 1) * N_KEYS, :] = _dot(keys_ref[1], q[half:2 * half, :])

        def top_values(s_ref, h, cs):
            v = [s_ref[h * N_KEYS + r * SUBLANES:h * N_KEYS + (r + 1) * SUBLANES, cs]
                 for r in range(N_KEYS // SUBLANES)]
            for i, j in _SORT_PAIRS:
                v[i], v[j] = jnp.maximum(v[i], v[j]), jnp.minimum(v[i], v[j])
            depth = len(v)
            out = []
            for k in range(ntop):
                m = v[0]
                for shift in (4, 2, 1):
                    m = jnp.maximum(m, pltpu.roll(m, shift, 0))
                out.append(m)
                hit = v[0] == m
                for r in range(min(depth, ntop - 1 - k)):
                    v[r] = jnp.where(hit, v[r + 1] if r + 1 < depth else neg_inf, v[r])
            return out

        for tc in range(tt // LANES):
            cs = slice(tc * LANES, (tc + 1) * LANES)
            for c, s_ref in enumerate((a0_ref, a1_ref)):
                for h in range(PEER_HEADS):
                    for r, m in enumerate(top_values(s_ref, h, cs)):
                        top_ref[c, r, h:h + 1, cs] = m[0:1, :]

            a = [top_ref[0, r, :, cs] for r in range(ntop)]
            b = [top_ref[1, r, :, cs] for r in range(ntop)]
            lists = [[a[r1] + b[r2] for r2 in range(ntop // (r1 + 1))] for r1 in range(ntop)]
            a0, b0 = a[0], b[0]
            m0 = a0 + b0
            z = jnp.zeros(m0.shape, F32)
            for k in range(ntop):
                m = functools.reduce(jnp.maximum, [lst[0] for lst in lists])
                if k < PEER_TOPK:
                    z = z + jnp.exp(m - m0)
                if k == PEER_TOPK - 1:
                    v16 = m
                if k < ntop - 1:
                    for lst in lists:
                        hit = lst[0] == m
                        for j in range(len(lst)):
                            lst[j] = jnp.where(hit, lst[j + 1] if j + 1 < len(lst) else neg_inf, lst[j])
            v17 = m
            zinv = np.float32(math.sqrt(0.5)) / z
            gmin = jnp.exp(0.5 * (v16 + v17) - m0) * zinv
            for h in range(PEER_HEADS):
                hk = slice(h * N_KEYS, (h + 1) * N_KEYS)
                e1_ref[h, tc] = jnp.exp(a0_ref[hk, cs] - a0[h:h + 1, :])
                e2_ref[h, tc] = jnp.exp(a1_ref[hk, cs] - b0[h:h + 1, :]) * zinv[h:h + 1, :]
                gmin_ref[h, tc] = gmin[h:h + 1, :]

            if (tc + 1) % (mxu_cols // LANES) == 0:
                ps = slice((tc + 1) * LANES - mxu_cols, (tc + 1) * LANES)
                a0_ref[:, ps] = _dot(u0_ref[0:ec, :], hb_ref[:, ps])
                a1_ref[:, ps] = _dot(u0_ref[ec:2 * ec, :], hb_ref[:, ps])

        acca_ref[...] = jnp.zeros(acca_ref.shape, F32)
        accb_ref[...] = jnp.zeros(accb_ref.shape, F32)

    def gate_gelu(a_ref, p_ref, chunk, tc):
        i0 = pl.multiple_of(chunk * rows_per_chunk, SUBLANES)
        cs = slice(tc * LANES, (tc + 1) * LANES)
        e18 = [e1_ref[h, tc, pl.ds(i0, rows_per_chunk), :] for h in range(PEER_HEADS)]
        gm = [gmin_ref[h, tc] for h in range(PEER_HEADS)]
        for ii in range(rows_per_chunk):
            rows = slice(ii * N_KEYS, (ii + 1) * N_KEYS)
            w = jnp.zeros((N_KEYS, LANES), F32)
            for h in range(PEER_HEADS):
                gate = e18[h][ii:ii + 1, :] * e2_ref[h, tc]
                w = w + jnp.where(gate >= gm[h], gate, 0.0)
            av = a_ref[rows, cs]
            p_ref[rows, cs] = ((w * av) * (1.0 + lax.erf(av))).astype(BF16)

    for par, (a_ref, p_ref, acc_ref) in enumerate(((a0_ref, p0_ref, acca_ref),
                                                   (a1_ref, p1_ref, accb_ref))):
        es = slice(par * ec, (par + 1) * ec)
        for piece in range(n_piece):
            cs = slice(piece * mxu_cols, (piece + 1) * mxu_cols)
            for tc in range(piece * (mxu_cols // LANES), (piece + 1) * (mxu_cols // LANES)):
                gate_gelu(a_ref, p_ref, 2 * g + par, tc)
            acc_ref[:, cs] += _dot(vt_ref[:, es], p_ref[:, cs])
            a_ref[:, cs] = _dot(un_ref[es, :], hb_ref[:, cs])

    @pl.when(g == last)
    def _finish():
        o_ref[...] = x_ref[...] + (acca_ref[...] + accb_ref[...]).T


def _peer(x, g, wq_t, keys, u, vt, layer, tile, ec):
    T, D = x.shape
    E = u.shape[1]
    nchunk = E // ec
    assert E == N_KEYS * N_KEYS and ec == SUBLANES * N_KEYS and tile % LANES == 0
    assert ec == PEER_HEADS * N_KEYS
    assert nchunk % 2 == 0
    ntc = tile // LANES
    nstep = nchunk // 2
    return pl.pallas_call(
        _peer_kernel,
        grid=(T // tile, nstep),
        in_specs=[pl.BlockSpec((tile, D), lambda i, s: (i, 0)),
                  _const_spec((1, D)),
                  pl.BlockSpec((None,) + wq_t.shape[1:], lambda i, s: (layer, 0, 0),
                               pipeline_mode=pl.Buffered(1)),
                  pl.BlockSpec((None,) + keys.shape[1:], lambda i, s: (layer, 0, 0, 0),
                               pipeline_mode=pl.Buffered(1)),
                  pl.BlockSpec((None, 2 * ec, D), lambda i, s: (layer, 0, 0),
                               pipeline_mode=pl.Buffered(1)),
                  pl.BlockSpec((None, 2 * ec, D),
                               lambda i, s: (layer, jnp.minimum(s + 1, nstep - 1), 0)),
                  pl.BlockSpec((None, D, 2 * ec), lambda i, s: (layer, 0, s))],
        out_specs=pl.BlockSpec((tile, D), lambda i, s: (i, 0)),
        out_shape=jax.ShapeDtypeStruct((T, D), F32),
        scratch_shapes=[pltpu.VMEM((D, tile), BF16),
                        pltpu.VMEM((PEER_HEADS, ntc, N_KEYS, LANES), F32),
                        pltpu.VMEM((PEER_HEADS, ntc, N_KEYS, LANES), F32),
                        pltpu.VMEM((PEER_HEADS, ntc, 1, LANES), F32),
                        pltpu.VMEM((2, PEER_TOPK + 1, PEER_HEADS, tile), F32),
                        pltpu.VMEM((ec, tile), F32), pltpu.VMEM((ec, tile), F32),
                        pltpu.VMEM((ec, tile), BF16), pltpu.VMEM((ec, tile), BF16),
                        pltpu.VMEM((D, tile), F32), pltpu.VMEM((D, tile), F32)],
        compiler_params=_params("arbitrary", "arbitrary"),
        name="peer",
    )(x, g, wq_t, keys, u, u, vt)


def _rope_rows(tt, pos_ref, invc_ref):
    half = ROT_DIM // 2
    ang = invc_ref[...] * pos_ref[...].astype(F32)
    cos, sin = jnp.cos(ang), jnp.sin(ang)
    rows = []
    for base in range(0, tt.shape[0], HEAD_DIM):
        t1 = tt[base:base + half, :]
        t2 = tt[base + half:base + ROT_DIM, :]
        rows += [t1 * cos - t2 * sin, t2 * cos + t1 * sin, tt[base + ROT_DIM:base + HEAD_DIM, :]]
    return rows


def _ple_core(x_ref, p_ref, g_ref, proj_ref, gw_ref):
    x = x_ref[...]
    hn = _rms(x, g_ref[...], NORM_EPS).astype(BF16)
    gate = jax.nn.sigmoid(_dot(hn, gw_ref[...]))
    return x + _dot(p_ref[...].astype(BF16), proj_ref[...]) * gate


def _ple_kv_kernel(x_ref, p_ref, g_ref, proj_ref, gw_ref, kvg_ref, kvw_ref, kvb_ref, pos_ref,
                   invc_ref, o_ref, k_ref, v_ref):
    x2 = _ple_core(x_ref, p_ref, g_ref, proj_ref, gw_ref)
    o_ref[...] = x2
    kv = _dot(_rms(x2, kvg_ref[...], NORM_EPS).astype(BF16), kvw_ref[...]) + kvb_ref[...]
    kvd = k_ref.shape[1]
    kt = jnp.concatenate(_rope_rows(kv[:, :kvd].T, pos_ref, invc_ref), axis=0)
    k_ref[...] = kt.T.astype(k_ref.dtype)
    v_ref[...] = kv[:, kvd:].astype(v_ref.dtype)


def _ple_final_kernel(x_ref, p_ref, g_ref, proj_ref, gw_ref, fg_ref, o_ref):
    x2 = _ple_core(x_ref, p_ref, g_ref, proj_ref, gw_ref)
    o_ref[...] = _rms(x2, fg_ref[...], NORM_EPS)


def _ple_kv(x, p, g, proj, gw, kvg, kvw, kvb, pos, inv_col, tile):
    T, D = x.shape
    P = p.shape[1]
    kvd = kvw.shape[1] // 2
    tok = lambda w: pl.BlockSpec((tile, w), lambda i: (i, 0))
    return pl.pallas_call(
        _ple_kv_kernel,
        grid=(T // tile,),
        in_specs=[tok(D), tok(P), _const_spec((1, D)), _const_spec(proj.shape),
                  _const_spec(gw.shape), _const_spec((1, D)), _const_spec(kvw.shape),
                  _const_spec(kvb.shape),
                  pl.BlockSpec((None, 1, tile), lambda i: (i, 0, 0)), _const_spec(inv_col.shape)],
        out_specs=[tok(D), tok(kvd), tok(kvd)],
        out_shape=[jax.ShapeDtypeStruct((T, D), F32), jax.ShapeDtypeStruct((T, kvd), BF16),
                   jax.ShapeDtypeStruct((T, kvd), BF16)],
        compiler_params=_params("arbitrary"),
        name="ple_kv",
    )(x, p, g, proj, gw, kvg, kvw, kvb, pos.reshape(T // tile, 1, tile), inv_col)


def _ple_final(x, p, g, proj, gw, fg, tile):
    T, D = x.shape
    P = p.shape[1]
    tok = lambda w: pl.BlockSpec((tile, w), lambda i: (i, 0))
    return pl.pallas_call(
        _ple_final_kernel,
        grid=(T // tile,),
        in_specs=[tok(D), tok(P), _const_spec((1, D)), _const_spec(proj.shape),
                  _const_spec(gw.shape), _const_spec((1, D))],
        out_specs=tok(D),
        out_shape=jax.ShapeDtypeStruct((T, D), F32),
        compiler_params=_params("arbitrary"),
        name="ple_final",
    )(x, p, g, proj, gw, fg)


def _attn_kernel(x_ref, g_ref, qw_ref, qb_ref, sink_ref, owt_ref, ob_ref, kc_ref, kp_ref, vc_ref,
                 vp_ref, pos_ref, invc_ref, o_ref, qt_ref, att_ref, *, n_q_heads):
    tile = x_ref.shape[0]
    W = WINDOW
    nblk = tile // W
    q_per_kv = n_q_heads // N_KV_HEADS
    half = ROT_DIM // 2
    scale = HEAD_DIM ** -0.5
    first = pl.program_id(1) == 0

    h = _rms(x_ref[...], g_ref[...], NORM_EPS).astype(BF16)
    q = (_dot(h, qw_ref[...]) + qb_ref[...]) * scale
    qt = q.T
    ang = invc_ref[...] * pos_ref[...].astype(F32)
    cos, sin = jnp.cos(ang), jnp.sin(ang)
    for hh in range(n_q_heads):
        base = hh * HEAD_DIM
        t1 = qt[base:base + half, :]
        t2 = qt[base + half:base + ROT_DIM, :]
        qt_ref[base:base + half, :] = (t1 * cos - t2 * sin).astype(BF16)
        qt_ref[base + half:base + ROT_DIM, :] = (t2 * cos + t1 * sin).astype(BF16)
        qt_ref[base + ROT_DIM:base + HEAD_DIM, :] = qt[base + ROT_DIM:base + HEAD_DIM, :].astype(BF16)

    kj = lax.broadcasted_iota(jnp.int32, (2 * W, W), 0)
    qi = lax.broadcasted_iota(jnp.int32, (2 * W, W), 1)
    first_lo = jnp.where(first, W, 0)
    band = jnp.where((kj > qi) & (kj <= qi + W), 0.0, -jnp.inf)
    band_first = jnp.where(kj >= first_lo, band, -jnp.inf)
    band = jnp.concatenate([band] * q_per_kv, axis=1)
    band_first = jnp.concatenate([band_first] * q_per_kv, axis=1)

    for n in range(nblk):
        cols = slice(n * W, (n + 1) * W)
        if n == 0:
            kprev, vprev = kp_ref[...], vp_ref[...]
        else:
            kprev, vprev = kc_ref[(n - 1) * W:n * W, :], vc_ref[(n - 1) * W:n * W, :]
        kblk = jnp.concatenate([kprev, kc_ref[cols, :]], axis=0)
        vblk = jnp.concatenate([vprev, vc_ref[cols, :]], axis=0)
        bias = band_first if n == 0 else band
        for g in range(N_KV_HEADS):
            ks = kblk[:, g * HEAD_DIM:(g + 1) * HEAD_DIM]
            vs = vblk[:, g * HEAD_DIM:(g + 1) * HEAD_DIM]
            heads = range(g * q_per_kv, (g + 1) * q_per_kv)
            qs = jnp.concatenate([qt_ref[hh * HEAD_DIM:(hh + 1) * HEAD_DIM, cols] for hh in heads],
                                 axis=1)
            sink = jnp.concatenate([jnp.broadcast_to(sink_ref[:, hh:hh + 1], (1, W)) for hh in heads],
                                   axis=1)
            s = _dot(ks, qs) + bias
            m = jnp.maximum(jnp.max(s, axis=0, keepdims=True), sink)
            e = jnp.exp(s - m)
            denom = jnp.sum(e, axis=0, keepdims=True) + jnp.exp(sink - m)
            o = _dot(vs, e.astype(BF16), _TN) / denom
            for r, hh in enumerate(heads):
                att_ref[hh * HEAD_DIM:(hh + 1) * HEAD_DIM, cols] = o[:, r * W:(r + 1) * W].astype(BF16)

    o_ref[...] = x_ref[...] + _dot(owt_ref[...], att_ref[...]).T + ob_ref[...]


def _attn(x, g, qw, qb, sinks, ow_t, ob, k, v, pos, inv_col, batch, tile):
    T, D = x.shape
    kvd = k.shape[1]
    nq = qw.shape[1] // HEAD_DIM
    nt = T // batch // tile
    bpt = tile // WINDOW
    row = lambda b, i: (b * nt + i, 0)
    prev = lambda b, i: (jnp.maximum((b * nt + i) * bpt - 1, 0), 0)
    kern = functools.partial(_attn_kernel, n_q_heads=nq)
    return pl.pallas_call(
        kern,
        grid=(batch, nt),
        in_specs=[pl.BlockSpec((tile, D), row), _const_spec((1, D)), _const_spec(qw.shape),
                  _const_spec(qb.shape), _const_spec(sinks.shape), _const_spec(ow_t.shape),
                  _const_spec(ob.shape),
                  pl.BlockSpec((tile, kvd), row), pl.BlockSpec((WINDOW, kvd), prev),
                  pl.BlockSpec((tile, kvd), row), pl.BlockSpec((WINDOW, kvd), prev),
                  pl.BlockSpec((None, 1, tile), lambda b, i: (b * nt + i, 0, 0)),
                  _const_spec(inv_col.shape)],
        out_specs=pl.BlockSpec((tile, D), row),
        out_shape=jax.ShapeDtypeStruct((T, D), F32),
        scratch_shapes=[pltpu.VMEM((qw.shape[1], tile), BF16),
                        pltpu.VMEM((qw.shape[1], tile), BF16)],
        compiler_params=_params("arbitrary", "arbitrary"),
        name="swa_attn",
    )(x, g, qw, qb, sinks, ow_t, ob, k, k, v, v, pos.reshape(T // tile, 1, tile), inv_col)


def _row(v):
    return v.reshape(1, -1)


def kernel(x, p, positions, ssm_norm, ssm_in_w, ssm_conv_w, ssm_conv_b, ssm_dt_bias, ssm_A_log, ssm_D, ssm_gate_norm, ssm_out_w, kv_norm, kv_w, kv_b, attn_norm, q_w, q_b, sinks, o_w, o_b, peer_norm, peer_q_w, peer_sub_keys, peer_u, peer_v, ple_norm, ple_proj, ple_gate_w, final_norm):
    B, S, D = x.shape
    T = B * S
    depth = p.shape[0]
    n_a = ssm_norm.shape[0]
    H = ssm_D.shape[1]
    d_inner = H * SSM_HEADDIM
    conv_dim = ssm_conv_w.shape[2]

    xt = x.reshape(T, D)
    pos = positions.reshape(T, 1)
    inv_col = jnp.asarray(
        ROPE_THETA ** (-np.arange(0, ROT_DIM, 2, dtype=np.float64) / ROT_DIM), F32).reshape(-1, 1)
    expand = jnp.asarray(np.tile(np.repeat(np.eye(H, dtype=np.float32), SSM_HEADDIM, axis=1),
                                 (3, 1)), BF16)
    tril = jnp.asarray(np.tril(np.ones((SSD_CHUNK, SSD_CHUNK), np.float32)))

    peer_wq_t = peer_q_w.astype(BF16).transpose(0, 2, 1)
    peer_keys = peer_sub_keys.astype(BF16)
    peer_u_b = (peer_u * np.float32(math.sqrt(0.5))).astype(BF16)
    peer_vt = peer_v.astype(BF16).transpose(0, 2, 1)

    k_sh = v_sh = None
    for i in range(depth):
        if i < n_a:
            z, xs, bc, dtr = _inproj(xt, _row(ssm_norm[i]), ssm_in_w[i].astype(BF16), ssm_conv_w[i],
                                     _row(ssm_conv_b[i]), d_inner=d_inner, tile=INPROJ_TILE, seq=S)
            y = _ssd(z, xs, bc, dtr, _row(ssm_dt_bias[i]), _row(ssm_A_log[i]),
                     _row(jnp.repeat(ssm_D[i], SSM_HEADDIM)), _row(ssm_gate_norm[i]), expand, tril,
                     batch=B)
            xt = _outproj(xt, y, ssm_out_w[i].astype(BF16), tile=OUTPROJ_TILE)
        else:
            j = i - n_a
            xt = _attn(xt, _row(attn_norm[j]), q_w[j].astype(BF16), _row(q_b[j]), _row(sinks[j]),
                       o_w[j].T.astype(BF16), _row(o_b[j]), k_sh, v_sh, pos, inv_col, batch=B,
                       tile=ATTN_TILE)
        xt = _peer(xt, _row(peer_norm[i]), peer_wq_t, peer_keys, peer_u_b, peer_vt, layer=i,
                   tile=PEER_TILE, ec=PEER_CHUNK)
        if i == n_a - 1:
            xt, k_sh, v_sh = _ple_kv(xt, p[i].reshape(T, -1), _row(ple_norm[i]),
                                     ple_proj[i].astype(BF16), ple_gate_w[i].astype(BF16),
                                     _row(kv_norm), kv_w.astype(BF16), _row(kv_b), pos, inv_col,
                                     tile=PLE_TILE)
        elif i == depth - 1:
            xt = _ple_final(xt, p[i].reshape(T, -1), _row(ple_norm[i]), ple_proj[i].astype(BF16),
                            ple_gate_w[i].astype(BF16), _row(final_norm), tile=PLE_TILE)
        else:
            raise NotImplementedError("PLE without K/V or final norm")
    return xt.reshape(B, S, D)
```
